```python
import math
import jax, jax.numpy as jnp
from jax import lax
import numpy as np

D_MODEL = 1024
BATCH = 8
SEQ = 16384
DEPTH = 2

N_A_LAYERS = DEPTH // 2
N_B_LAYERS = DEPTH - N_A_LAYERS
CONV_WIDTH = 3
N_HEADS = 8
HEAD_DIM = 64
ATT_WIDTH = N_HEADS * HEAD_DIM
Q_BLOCK = 128
RMS_EPS = 1e-6

kernel_name = "yoco_shortconv_stickbreaking"


def _rmsnorm(x, g):
    xf = x.astype(jnp.float32)
    y = xf * lax.rsqrt(jnp.mean(xf * xf, axis=-1, keepdims=True) + RMS_EPS)
    return (y * g.astype(jnp.float32)).astype(x.dtype)


def _causal_depthwise_conv(u, w):
    return lax.conv_general_dilated(
        u, w[:, None, :].astype(u.dtype),
        window_strides=(1,),
        padding=[(CONV_WIDTH - 1, 0)],
        dimension_numbers=("NWC", "WIO", "NWC"),
        feature_group_count=u.shape[-1])


def _short_conv_mixer(h, w_in, w_conv, w_out):
    u = h @ w_in
    b_gate, c_gate, xin, g = jnp.split(u, 4, axis=-1)
    y = b_gate * _causal_depthwise_conv(c_gate * xin, w_conv)
    return (y * jax.nn.silu(g)) @ w_out


def _stick_breaking_attention(q, k, v):
    seq = q.shape[2]
    n_blocks = seq // Q_BLOCK
    scale = HEAD_DIM ** -0.5
    qf = q.astype(jnp.float32)
    kf = k.astype(jnp.float32)
    vf = v.astype(jnp.float32)
    outs = []
    for blk in range(n_blocks):
        q0 = blk * Q_BLOCK
        kl = q0 + Q_BLOCK
        qb = qf[:, :, q0:kl]
        z = jnp.einsum("bhqd,bhkd->bhqk", qb, kf[:, :, :kl]) * scale
        q_pos = q0 + jnp.arange(Q_BLOCK)
        mask = jnp.arange(kl)[None, :] < q_pos[:, None]
        log_one_minus = jnp.where(mask, jax.nn.log_sigmoid(-z), 0.0)
        rev = lax.cumsum(log_one_minus, axis=3, reverse=True)
        weights = jnp.exp(jnp.where(mask, z + rev, -jnp.inf))
        outs.append(jnp.einsum("bhqk,bhkd->bhqd", weights, vf[:, :, :kl]))
    o = jnp.concatenate(outs, axis=2)
    return o.astype(q.dtype)


def _fwd_setup_inputs(seed: int = 0) -> dict:
    key = jax.random.key(seed)
    ks = jax.random.split(key, 12)
    D = D_MODEL
    s = D ** -0.5
    f32 = jnp.float32
    return {
        "x": jax.random.normal(ks[0], (BATCH, SEQ, D), f32),
        "norm_a": 1.0 + 0.02 * jax.random.normal(ks[1], (N_A_LAYERS, D), f32),
        "w_in_a": s * jax.random.normal(ks[2], (N_A_LAYERS, D, 4 * D), f32),
        "conv_a": (CONV_WIDTH ** -0.5) * jax.random.normal(ks[3], (N_A_LAYERS, CONV_WIDTH, D), f32),
        "w_out_a": s * jax.random.normal(ks[4], (N_A_LAYERS, D, D), f32),
        "norm_kv": 1.0 + 0.02 * jax.random.normal(ks[5], (D,), f32),
        "w_kv": s * jax.random.normal(ks[6], (D, 2 * ATT_WIDTH), f32),
        "norm_b": 1.0 + 0.02 * jax.random.normal(ks[7], (N_B_LAYERS, D), f32),
        "w_in_b": s * jax.random.normal(ks[8], (N_B_LAYERS, D, 2 * ATT_WIDTH), f32),
        "w_out_b": (ATT_WIDTH ** -0.5) * jax.random.normal(ks[9], (N_B_LAYERS, ATT_WIDTH, D), f32),
        "norm_f": 1.0 + 0.02 * jax.random.normal(ks[10], (D,), f32),
    }


def _fwd_reference(x, norm_a, w_in_a, conv_a, w_out_a, norm_kv, w_kv, norm_b, w_in_b, w_out_b, norm_f):
    bsz, seq, _ = x.shape
    k = v = None
    for layer in range(DEPTH):
        if layer < N_A_LAYERS:
            i = layer
            h = _rmsnorm(x, norm_a[i])
            x = x + _short_conv_mixer(h, w_in_a[i], conv_a[i], w_out_a[i])
            if layer == N_A_LAYERS - 1:
                kv = _rmsnorm(x, norm_kv) @ w_kv
                k, v = jnp.split(kv, 2, axis=-1)
                k = k.reshape(bsz, seq, N_HEADS, HEAD_DIM).transpose(0, 2, 1, 3)
                v = v.reshape(bsz, seq, N_HEADS, HEAD_DIM).transpose(0, 2, 1, 3)
        else:
            j = layer - N_A_LAYERS
            h = _rmsnorm(x, norm_b[j])
            q, g = jnp.split(h @ w_in_b[j], 2, axis=-1)
            q = q.reshape(bsz, seq, N_HEADS, HEAD_DIM).transpose(0, 2, 1, 3)
            o = _stick_breaking_attention(q, k, v)
            o = o.transpose(0, 2, 1, 3).reshape(bsz, seq, ATT_WIDTH)
            x = x + (o * jax.nn.silu(g)) @ w_out_b[j]
    return _rmsnorm(x, norm_f)


import jax as _jax
import jax.numpy as _jnp

TWIN_FORMAT = 'train_step'
FWD_PARAMS = ['x', 'norm_a', 'w_in_a', 'conv_a', 'w_out_a', 'norm_kv', 'w_kv', 'norm_b', 'w_in_b', 'w_out_b', 'norm_f']
TWIN_WEIGHTS = ['norm_a', 'w_in_a', 'conv_a', 'w_out_a', 'norm_kv', 'w_kv', 'norm_b', 'w_in_b', 'w_out_b', 'norm_f']
TWIN_DIFF_INPUT = 'x'
TWIN_INPUTS = ['x', 'norm_a', 'w_in_a', 'conv_a', 'w_out_a', 'norm_kv', 'w_kv', 'norm_b', 'w_in_b', 'w_out_b', 'norm_f', 'loss_target', 'm_norm_a', 'm_w_in_a', 'm_conv_a', 'm_w_out_a', 'm_norm_kv', 'm_w_kv', 'm_norm_b', 'm_w_in_b', 'm_w_out_b', 'm_norm_f', 'v_norm_a', 'v_w_in_a', 'v_conv_a', 'v_w_out_a', 'v_norm_kv', 'v_w_kv', 'v_norm_b', 'v_w_in_b', 'v_w_out_b', 'v_norm_f']
TWIN_OUTPUTS = ['loss', 'grad_x', 'grad_norm_a', 'grad_w_in_a', 'grad_conv_a', 'grad_w_out_a', 'grad_norm_kv', 'grad_w_kv', 'grad_norm_b', 'grad_w_in_b', 'grad_w_out_b', 'grad_norm_f', 'delta_norm_a', 'delta_w_in_a', 'delta_conv_a', 'delta_w_out_a', 'delta_norm_kv', 'delta_w_kv', 'delta_norm_b', 'delta_w_in_b', 'delta_w_out_b', 'delta_norm_f', 'new_m_norm_a', 'new_m_w_in_a', 'new_m_conv_a', 'new_m_w_out_a', 'new_m_norm_kv', 'new_m_w_kv', 'new_m_norm_b', 'new_m_w_in_b', 'new_m_w_out_b', 'new_m_norm_f', 'new_v_norm_a', 'new_v_w_in_a', 'new_v_conv_a', 'new_v_w_out_a', 'new_v_norm_kv', 'new_v_w_kv', 'new_v_norm_b', 'new_v_w_in_b', 'new_v_w_out_b', 'new_v_norm_f']
TWIN_LEAF_KINDS = {'loss': 'loss', 'grad_x': 'grad_x', 'grad_norm_a': 'grad_w', 'grad_w_in_a': 'grad_w', 'grad_conv_a': 'grad_w', 'grad_w_out_a': 'grad_w', 'grad_norm_kv': 'grad_w', 'grad_w_kv': 'grad_w', 'grad_norm_b': 'grad_w', 'grad_w_in_b': 'grad_w', 'grad_w_out_b': 'grad_w', 'grad_norm_f': 'grad_w', 'delta_norm_a': 'delta_w', 'delta_w_in_a': 'delta_w', 'delta_conv_a': 'delta_w', 'delta_w_out_a': 'delta_w', 'delta_norm_kv': 'delta_w', 'delta_w_kv': 'delta_w', 'delta_norm_b': 'delta_w', 'delta_w_in_b': 'delta_w', 'delta_w_out_b': 'delta_w', 'delta_norm_f': 'delta_w', 'new_m_norm_a': 'new_m', 'new_m_w_in_a': 'new_m', 'new_m_conv_a': 'new_m', 'new_m_w_out_a': 'new_m', 'new_m_norm_kv': 'new_m', 'new_m_w_kv': 'new_m', 'new_m_norm_b': 'new_m', 'new_m_w_in_b': 'new_m', 'new_m_w_out_b': 'new_m', 'new_m_norm_f': 'new_m', 'new_v_norm_a': 'new_v', 'new_v_w_in_a': 'new_v', 'new_v_conv_a': 'new_v', 'new_v_w_out_a': 'new_v', 'new_v_norm_kv': 'new_v', 'new_v_w_kv': 'new_v', 'new_v_norm_b': 'new_v', 'new_v_w_in_b': 'new_v', 'new_v_w_out_b': 'new_v', 'new_v_norm_f': 'new_v'}


def _forward(args):
    return _fwd_reference(*[args[k] for k in FWD_PARAMS])


def _output_shape():
    def fwd():
        inp = _fwd_setup_inputs(0)
        return _fwd_reference(*[inp[k] for k in FWD_PARAMS])
    out = _jax.eval_shape(fwd)
    return out.shape, out.dtype

N_MICROBATCH = 1
ADAM_LR = 0.001
ADAM_B1 = 0.9
ADAM_B2 = 0.999
ADAM_EPS = 1e-08
ADAM_WD = 0.01
ADAM_STEP = 10
PER_EXAMPLE_BATCH_AXIS = {'x': 0, 'loss_target': 0}
SHARED_INPUTS = []
_WEIGHT_DTYPES = {'norm_a': _jnp.float32, 'w_in_a': _jnp.float32, 'conv_a': _jnp.float32, 'w_out_a': _jnp.float32, 'norm_kv': _jnp.float32, 'w_kv': _jnp.float32, 'norm_b': _jnp.float32, 'w_in_b': _jnp.float32, 'w_out_b': _jnp.float32, 'norm_f': _jnp.float32}
MOMENT_SCALE = {'norm_a': 3.900039e-01, 'w_in_a': 1.918657e-01, 'conv_a': 2.010861e-01, 'w_out_a': 1.915397e-01, 'norm_kv': 1.227189e-01, 'w_kv': 1.239155e-01, 'norm_b': 1.254405e-01, 'w_in_b': 1.269781e-01, 'w_out_b': 1.124882e-01, 'norm_f': 1.280427e+02}


def _to_microbatches(a, axis):
    t = _jnp.moveaxis(a, axis, 0)
    t = t.reshape((N_MICROBATCH, t.shape[0] // N_MICROBATCH) + t.shape[1:])
    return _jnp.moveaxis(t, 1, axis + 1)


def setup_inputs(seed: int = 0) -> dict:
    inp = _fwd_setup_inputs(seed)
    key = _jax.random.fold_in(_jax.random.key(seed), 7919)
    shape, _ = _output_shape()
    out = dict(inp)
    out["loss_target"] = _jax.random.normal(_jax.random.fold_in(key, 0), shape, _jnp.float32)
    for i, name in enumerate(TWIN_WEIGHTS):
        w = inp[name].astype(_jnp.float32)
        if MOMENT_SCALE is None:
            s = _jnp.sqrt(_jnp.mean(_jnp.square(w)) + 1e-30)
        else:
            s = MOMENT_SCALE[name]
        km, kv = _jax.random.split(_jax.random.fold_in(key, i + 1))
        out[name] = w
        out["m_" + name] = s * _jax.random.normal(km, w.shape, _jnp.float32)
        out["v_" + name] = (s * s) * _jax.random.uniform(kv, w.shape, _jnp.float32, 0.5, 1.5)
    if N_MICROBATCH > 1:
        for name, axis in PER_EXAMPLE_BATCH_AXIS.items():
            out[name] = _to_microbatches(out[name], axis)
    return {'x': out['x'], 'norm_a': out['norm_a'], 'w_in_a': out['w_in_a'], 'conv_a': out['conv_a'], 'w_out_a': out['w_out_a'], 'norm_kv': out['norm_kv'], 'w_kv': out['w_kv'], 'norm_b': out['norm_b'], 'w_in_b': out['w_in_b'], 'w_out_b': out['w_out_b'], 'norm_f': out['norm_f'], 'loss_target': out['loss_target'], 'm_norm_a': out['m_norm_a'], 'm_w_in_a': out['m_w_in_a'], 'm_conv_a': out['m_conv_a'], 'm_w_out_a': out['m_w_out_a'], 'm_norm_kv': out['m_norm_kv'], 'm_w_kv': out['m_w_kv'], 'm_norm_b': out['m_norm_b'], 'm_w_in_b': out['m_w_in_b'], 'm_w_out_b': out['m_w_out_b'], 'm_norm_f': out['m_norm_f'], 'v_norm_a': out['v_norm_a'], 'v_w_in_a': out['v_w_in_a'], 'v_conv_a': out['v_conv_a'], 'v_w_out_a': out['v_w_out_a'], 'v_norm_kv': out['v_norm_kv'], 'v_w_kv': out['v_w_kv'], 'v_norm_b': out['v_norm_b'], 'v_w_in_b': out['v_w_in_b'], 'v_w_out_b': out['v_w_out_b'], 'v_norm_f': out['v_norm_f']}


def _loss(weights, diff, rest, loss_target):
    with _jax.named_scope("forward"):
        args = {**rest, TWIN_DIFF_INPUT: diff, **{k: w.astype(_WEIGHT_DTYPES[k]) for k, w in weights.items()}}
        y = _forward(args)
    with _jax.named_scope("loss_head"):
        err = _jnp.square(y.astype(_jnp.float32) - loss_target)
        return 0.5 * _jnp.sum(_jnp.mean(err, axis=-1)) if err.ndim else 0.5 * err


def _adamw(w, g, m, v):
    m = ADAM_B1 * m + (1.0 - ADAM_B1) * g
    v = ADAM_B2 * v + (1.0 - ADAM_B2) * _jnp.square(g)
    m_hat = m / (1.0 - ADAM_B1 ** ADAM_STEP)
    v_hat = v / (1.0 - ADAM_B2 ** ADAM_STEP)
    delta = -ADAM_LR * (m_hat / (_jnp.sqrt(v_hat) + ADAM_EPS) + ADAM_WD * w)
    return delta, m, v


def reference(x, norm_a, w_in_a, conv_a, w_out_a, norm_kv, w_kv, norm_b, w_in_b, w_out_b, norm_f, loss_target, m_norm_a, m_w_in_a, m_conv_a, m_w_out_a, m_norm_kv, m_w_kv, m_norm_b, m_w_in_b, m_w_out_b, m_norm_f, v_norm_a, v_w_in_a, v_conv_a, v_w_out_a, v_norm_kv, v_w_kv, v_norm_b, v_w_in_b, v_w_out_b, v_norm_f):
    given = dict(x=x, norm_a=norm_a, w_in_a=w_in_a, conv_a=conv_a, w_out_a=w_out_a, norm_kv=norm_kv, w_kv=w_kv, norm_b=norm_b, w_in_b=w_in_b, w_out_b=w_out_b, norm_f=norm_f, loss_target=loss_target, m_norm_a=m_norm_a, m_w_in_a=m_w_in_a, m_conv_a=m_conv_a, m_w_out_a=m_w_out_a, m_norm_kv=m_norm_kv, m_w_kv=m_w_kv, m_norm_b=m_norm_b, m_w_in_b=m_w_in_b, m_w_out_b=m_w_out_b, m_norm_f=m_norm_f, v_norm_a=v_norm_a, v_w_in_a=v_w_in_a, v_conv_a=v_conv_a, v_w_out_a=v_w_out_a, v_norm_kv=v_norm_kv, v_w_kv=v_w_kv, v_norm_b=v_norm_b, v_w_in_b=v_w_in_b, v_w_out_b=v_w_out_b, v_norm_f=v_norm_f)
    weights = {n: given[n] for n in TWIN_WEIGHTS}
    shared = {n: given[n] for n in SHARED_INPUTS}
    per_example = {n: given[n] for n in ['x']}
    grad_fn = _jax.value_and_grad(_loss, argnums=(0, 1))

    def one_microbatch(ex, loss_target):
        ex = dict(ex)
        diff = ex.pop(TWIN_DIFF_INPUT)
        return grad_fn(weights, diff, {**shared, **ex}, loss_target)

    if N_MICROBATCH == 1:
        loss, (grad_w, grad_x) = one_microbatch(per_example, given["loss_target"])
    else:
        def body(carry, xs):
            loss_sum, grad_sum = carry
            l_k, (gw_k, gx_k) = one_microbatch(xs[0], xs[1])
            with _jax.named_scope("update"):
                return (loss_sum + l_k, _jax.tree.map(_jnp.add, grad_sum, gw_k)), gx_k

        init = (_jnp.zeros((), _jnp.float32), _jax.tree.map(_jnp.zeros_like, weights))
        (loss, grad_w), grad_x = _jax.lax.scan(body, init, (per_example, given["loss_target"]))
    with _jax.named_scope("update"):
        delta_w, new_m, new_v = {}, {}, {}
        for n in TWIN_WEIGHTS:
            delta_w[n], new_m[n], new_v[n] = _adamw(weights[n], grad_w[n], given["m_" + n], given["v_" + n])
    return (loss, grad_x, *[grad_w[n] for n in TWIN_WEIGHTS], *[delta_w[n] for n in TWIN_WEIGHTS],
            *[new_m[n] for n in TWIN_WEIGHTS], *[new_v[n] for n in TWIN_WEIGHTS])
```

```python
import functools

import jax
import jax.numpy as jnp
from jax import lax
from jax.experimental import pallas as pl
from jax.experimental.pallas import tpu as pltpu

F32 = jnp.float32
BF16 = jnp.bfloat16
MESH = pl.DeviceIdType.MESH

HEAD_DIM = 64
HEAD_PAIR = 2 * HEAD_DIM
RMS_EPS = 1e-6
ADAM_LR = 0.001
ADAM_B1 = 0.9
ADAM_B2 = 0.999
ADAM_EPS = 1e-08
ADAM_WD = 0.01
ADAM_STEP = 10
N_BLOCKS = 4
SMALL_ROWS = 8
VMEM_LIMIT_V7X = 56 * 1024 * 1024


def _params(*sem):
    return pltpu.CompilerParams(dimension_semantics=sem if sem else None, vmem_limit_bytes=VMEM_LIMIT_V7X)


def _sigmoid(x):
    return 1.0 / (1.0 + jnp.exp(-x))


def _rms_normalize(x):
    rstd = lax.rsqrt(jnp.mean(x * x, axis=-1, keepdims=True) + RMS_EPS)
    return x * rstd, rstd


def _norm_matmul(x, gain, w, *, out_dtype, name, tm=512, tn=1024):
    t, d = x.shape
    n = w.shape[1]
    tm, tn = min(tm, t), min(tn, n)

    def body(x_ref, g_ref, w_ref, o_ref):
        xn, _ = _rms_normalize(x_ref[...])
        h = (xn * g_ref[...]).astype(BF16)
        o_ref[...] = jnp.dot(h, w_ref[...], preferred_element_type=F32).astype(o_ref.dtype)

    return pl.pallas_call(
        body, name=name, grid=(t // tm, n // tn),
        in_specs=[pl.BlockSpec((tm, d), lambda i, j: (i, 0)),
                  pl.BlockSpec((1, d), lambda i, j: (0, 0)),
                  pl.BlockSpec((d, tn), lambda i, j: (0, j))],
        out_specs=pl.BlockSpec((tm, tn), lambda i, j: (i, j)),
        out_shape=jax.ShapeDtypeStruct((t, n), out_dtype),
        compiler_params=_params("parallel", "arbitrary"),
    )(x, gain, w)


def _matmul(a, w, *, name, trans_b=False, res=None, out_dtype=F32, tm=512, tn=1024, tk=1024):
    t, k = a.shape
    n = w.shape[0] if trans_b else w.shape[1]
    tm, tn, tk = min(tm, t), min(tn, n), min(tk, k)
    nk = k // tk

    def body(*refs):
        if res is None:
            a_ref, w_ref, o_ref, acc = refs
        else:
            a_ref, w_ref, r_ref, o_ref, acc = refs
        kk = pl.program_id(2)

        @pl.when(kk == 0)
        def _():
            acc[...] = jnp.zeros_like(acc)

        av = a_ref[...].astype(BF16)
        if trans_b:
            acc[...] += lax.dot_general(av, w_ref[...], (((1,), (1,)), ((), ())), preferred_element_type=F32)
        else:
            acc[...] += jnp.dot(av, w_ref[...], preferred_element_type=F32)

        @pl.when(kk == nk - 1)
        def _():
            r = acc[...]
            if res is not None:
                r = r + r_ref[...]
            o_ref[...] = r.astype(o_ref.dtype)

    in_specs = [pl.BlockSpec((tm, tk), lambda i, j, kk: (i, kk)),
                pl.BlockSpec((tn, tk), lambda i, j, kk: (j, kk)) if trans_b
                else pl.BlockSpec((tk, tn), lambda i, j, kk: (kk, j))]
    args = [a, w]
    if res is not None:
        in_specs.append(pl.BlockSpec((tm, tn), lambda i, j, kk: (i, j)))
        args.append(res)
    return pl.pallas_call(
        body, name=name, grid=(t // tm, n // tn, nk),
        in_specs=in_specs,
        out_specs=pl.BlockSpec((tm, tn), lambda i, j, kk: (i, j)),
        out_shape=jax.ShapeDtypeStruct((t, n), out_dtype),
        scratch_shapes=[pltpu.VMEM((tm, tn), F32)],
        compiler_params=_params("parallel", "parallel", "arbitrary"),
    )(*args)


def _matmul_tn(a, b, *, name, gain=None, blocked_out=False, tt=512, tn=512):
    t, k = a.shape
    n = b.shape[1]
    tt, tn = min(tt, t), min(tn, n)

    def body(*refs):
        if gain is None:
            a_ref, b_ref, o_ref = refs
            av = a_ref[...].astype(BF16)
        else:
            a_ref, g_ref, b_ref, o_ref = refs
            xn, _ = _rms_normalize(a_ref[...])
            av = (xn * g_ref[...]).astype(BF16)

        @pl.when(pl.program_id(1) == 0)
        def _():
            o_ref[...] = jnp.zeros_like(o_ref)

        o_ref[...] += lax.dot_general(av, b_ref[...].astype(BF16), (((0,), (0,)), ((), ())),
                                      preferred_element_type=F32)

    in_specs = [pl.BlockSpec((tt, k), lambda j, s: (s, 0))]
    args = [a]
    if gain is not None:
        in_specs.append(pl.BlockSpec((1, k), lambda j, s: (0, 0)))
        args.append(gain)
    in_specs.append(pl.BlockSpec((tt, tn), lambda j, s: (s, j)))
    args.append(b)
    if blocked_out:
        out_spec = pl.BlockSpec((None, k, tn), lambda j, s: (j, 0, 0))
        out_shape = jax.ShapeDtypeStruct((n // tn, k, tn), F32)
    else:
        out_spec = pl.BlockSpec((k, tn), lambda j, s: (0, j))
        out_shape = jax.ShapeDtypeStruct((k, n), F32)
    return pl.pallas_call(
        body, name=name, grid=(n // tn, t // tt),
        in_specs=in_specs, out_specs=out_spec, out_shape=out_shape,
        compiler_params=_params("parallel", "arbitrary"),
    )(*args)


HALO = 16


def _shift_down(v, halo, k):
    rows = lax.broadcasted_iota(jnp.int32, v.shape, 0)
    out = pltpu.roll(v, k, 0)
    for r in range(k):
        out = jnp.where(rows == r, halo[HALO - k + r:HALO - k + r + 1, :], out)
    return out


def _shift_up(v, halo, k):
    n = v.shape[0]
    rows = lax.broadcasted_iota(jnp.int32, v.shape, 0)
    out = pltpu.roll(v, n - k, 0)
    for r in range(k):
        out = jnp.where(rows == n - k + r, halo[r:r + 1, :], out)
    return out


def _mixer_fwd(u, conv, *, tm=256):
    t, d4 = u.shape
    d = d4 // 4
    tm = min(tm, t)
    hb = tm // HALO

    def body(b_ref, c_ref, x_ref, g_ref, ch_ref, xh_ref, w_ref, m_ref):
        i = pl.program_id(0)
        y1 = c_ref[...].astype(F32) * x_ref[...].astype(F32)
        prev = ch_ref[...].astype(F32) * xh_ref[...].astype(F32)
        prev = jnp.where(i == 0, 0.0, prev)
        w = w_ref[...]
        yc = w[2:3, :] * y1 + w[1:2, :] * _shift_down(y1, prev, 1) + w[0:1, :] * _shift_down(y1, prev, 2)
        g = g_ref[...].astype(F32)
        m_ref[...] = (b_ref[...].astype(F32) * yc * (g * _sigmoid(g))).astype(m_ref.dtype)

    def col(c):
        return pl.BlockSpec((tm, d), lambda i: (i, c))

    def prev_rows(c):
        return pl.BlockSpec((HALO, d), lambda i: (jnp.maximum(i * hb - 1, 0), c))

    return pl.pallas_call(
        body, name="mixer_fwd", grid=(t // tm,),
        in_specs=[col(0), col(1), col(2), col(3), prev_rows(1), prev_rows(2), pl.BlockSpec((3, d), lambda i: (0, 0))],
        out_specs=pl.BlockSpec((tm, d), lambda i: (i, 0)),
        out_shape=jax.ShapeDtypeStruct((t, d), BF16),
        compiler_params=_params("parallel"),
    )(u, u, u, u, u, u, conv)


def _mixer_bwd(u, dm, conv, *, tm=256):
    t, d4 = u.shape
    d = d4 // 4
    tm = min(tm, t)
    hb = tm // HALO
    nb = t // tm

    def body(b_ref, c_ref, x_ref, g_ref, dm_ref, ch_ref, xh_ref, bn_ref, gn_ref, dmn_ref, w_ref, du_ref, gw_ref):
        i = pl.program_id(0)
        w = w_ref[...]
        b = b_ref[...].astype(F32)
        c = c_ref[...].astype(F32)
        xin = x_ref[...].astype(F32)
        g = g_ref[...].astype(F32)
        dm_v = dm_ref[...]
        y1 = c * xin
        prev = jnp.where(i == 0, 0.0, ch_ref[...].astype(F32) * xh_ref[...].astype(F32))
        y1m1 = _shift_down(y1, prev, 1)
        y1m2 = _shift_down(y1, prev, 2)
        yc = w[2:3, :] * y1 + w[1:2, :] * y1m1 + w[0:1, :] * y1m2
        sg = _sigmoid(g)
        s = g * sg
        ds = sg * (1.0 + g * (1.0 - sg))
        dyc = dm_v * b * s
        gn = gn_ref[...].astype(F32)
        nxt = dmn_ref[...] * bn_ref[...].astype(F32) * (gn * _sigmoid(gn))
        nxt = jnp.where(i == nb - 1, 0.0, nxt)
        dy1 = w[2:3, :] * dyc + w[1:2, :] * _shift_up(dyc, nxt, 1) + w[0:1, :] * _shift_up(dyc, nxt, 2)
        du_ref[:, 0:d] = (dm_v * yc * s).astype(du_ref.dtype)
        du_ref[:, d:2 * d] = (dy1 * xin).astype(du_ref.dtype)
        du_ref[:, 2 * d:3 * d] = (dy1 * c).astype(du_ref.dtype)
        du_ref[:, 3 * d:4 * d] = (dm_v * b * yc * ds).astype(du_ref.dtype)

        @pl.when(i == 0)
        def _():
            gw_ref[...] = jnp.zeros_like(gw_ref)

        gw_ref[0:1, :] += jnp.sum(dyc * y1m2, axis=0, keepdims=True)
        gw_ref[1:2, :] += jnp.sum(dyc * y1m1, axis=0, keepdims=True)
        gw_ref[2:3, :] += jnp.sum(dyc * y1, axis=0, keepdims=True)

    def col(c):
        return pl.BlockSpec((tm, d), lambda i: (i, c))

    def prev_rows(c):
        return pl.BlockSpec((HALO, d), lambda i: (jnp.maximum(i * hb - 1, 0), c))

    def next_rows(c):
        return pl.BlockSpec((HALO, d), lambda i: (jnp.minimum((i + 1) * hb, nb * hb - 1), c))

    return pl.pallas_call(
        body, name="mixer_bwd", grid=(nb,),
        in_specs=[col(0), col(1), col(2), col(3), pl.BlockSpec((tm, d), lambda i: (i, 0)),
                  prev_rows(1), prev_rows(2), next_rows(0), next_rows(3),
                  pl.BlockSpec((HALO, d), lambda i: (jnp.minimum((i + 1) * hb, nb * hb - 1), 0)),
                  pl.BlockSpec((3, d), lambda i: (0, 0))],
        out_specs=[pl.BlockSpec((tm, d4), lambda i: (i, 0)), pl.BlockSpec((SMALL_ROWS, d), lambda i: (0, 0))],
        out_shape=[jax.ShapeDtypeStruct((t, d4), BF16), jax.ShapeDtypeStruct((SMALL_ROWS, d), F32)],
        compiler_params=_params("arbitrary"),
    )(u, u, u, u, dm, u, u, u, u, dm, conv)


def _gate_fwd(o, qg, *, tm=512):
    t, a = o.shape
    tm = min(tm, t)

    def body(o_ref, g_ref, out_ref):
        g = g_ref[...].astype(F32)
        out_ref[...] = (o_ref[...] * (g * _sigmoid(g))).astype(out_ref.dtype)

    return pl.pallas_call(
        body, name="gate_fwd", grid=(t // tm,),
        in_specs=[pl.BlockSpec((tm, a), lambda i: (i, 0)), pl.BlockSpec((tm, a), lambda i: (i, 1))],
        out_specs=pl.BlockSpec((tm, a), lambda i: (i, 0)),
        out_shape=jax.ShapeDtypeStruct((t, a), BF16),
        compiler_params=_params("parallel"),
    )(o, qg)


def _gate_bwd(da, o, qg, *, tm=512):
    t, a = o.shape
    tm = min(tm, t)

    def body(da_ref, o_ref, g_ref, do_ref, dg_ref):
        g = g_ref[...].astype(F32)
        sg = _sigmoid(g)
        da_v = da_ref[...]
        do_ref[...] = (da_v * (g * sg)).astype(do_ref.dtype)
        dg_ref[...] = (da_v * o_ref[...] * (sg * (1.0 + g * (1.0 - sg)))).astype(dg_ref.dtype)

    blk = pl.BlockSpec((tm, a), lambda i: (i, 0))
    return pl.pallas_call(
        body, name="gate_bwd", grid=(t // tm,),
        in_specs=[blk, blk, pl.BlockSpec((tm, a), lambda i: (i, 1))],
        out_specs=[blk, blk],
        out_shape=[jax.ShapeDtypeStruct((t, a), F32), jax.ShapeDtypeStruct((t, a), BF16)],
        compiler_params=_params("parallel"),
    )(da, o, qg)


def _loss_head(x2, gain, target, *, tm=512):
    t, d = x2.shape
    tm = min(tm, t)

    def body(x_ref, g_ref, y_ref, dx_ref, loss_ref, gg_ref):
        @pl.when(pl.program_id(0) == 0)
        def _():
            loss_ref[...] = jnp.zeros_like(loss_ref)
            gg_ref[...] = jnp.zeros_like(gg_ref)

        xn, rstd = _rms_normalize(x_ref[...])
        gv = g_ref[...]
        err = xn * gv - y_ref[...]
        loss_ref[...] += jnp.sum(err * err, axis=0, keepdims=True) * (0.5 / d)
        dy = err * (1.0 / d)
        gg_ref[...] += jnp.sum(dy * xn, axis=0, keepdims=True)
        dxn = dy * gv
        dx_ref[...] = rstd * (dxn - xn * jnp.mean(dxn * xn, axis=-1, keepdims=True))

    blk = pl.BlockSpec((tm, d), lambda i: (i, 0))
    row = pl.BlockSpec((1, d), lambda i: (0, 0))
    return pl.pallas_call(
        body, name="loss_head", grid=(t // tm,),
        in_specs=[blk, row, blk], out_specs=[blk, row, row],
        out_shape=[jax.ShapeDtypeStruct((t, d), F32), jax.ShapeDtypeStruct((1, d), F32),
                   jax.ShapeDtypeStruct((1, d), F32)],
        compiler_params=_params("arbitrary"),
    )(x2, gain, target)


def _rms_bwd(x, dhs, gains, dres, *, name, tm=512):
    t, d = x.shape
    tm = min(tm, t)
    nk = len(dhs)

    def body(*refs):
        x_ref = refs[0]
        dh_refs = refs[1:1 + nk]
        g_refs = refs[1 + nk:1 + 2 * nk]
        dres_ref = refs[1 + 2 * nk]
        dx_ref = refs[2 + 2 * nk]
        gg_refs = refs[3 + 2 * nk:]

        @pl.when(pl.program_id(0) == 0)
        def _():
            for gg in gg_refs:
                gg[...] = jnp.zeros_like(gg)

        xn, rstd = _rms_normalize(x_ref[...])
        dxn = None
        for dh_ref, g_ref, gg in zip(dh_refs, g_refs, gg_refs):
            dh = dh_ref[...]
            gg[...] += jnp.sum(dh * xn, axis=0, keepdims=True)
            term = dh * g_ref[...]
            dxn = term if dxn is None else dxn + term
        dx_ref[...] = dres_ref[...] + rstd * (dxn - xn * jnp.mean(dxn * xn, axis=-1, keepdims=True))

    blk = pl.BlockSpec((tm, d), lambda i: (i, 0))
    row = pl.BlockSpec((1, d), lambda i: (0, 0))
    return pl.pallas_call(
        body, name=name, grid=(t // tm,),
        in_specs=[blk] + [blk] * nk + [row] * nk + [blk],
        out_specs=[blk] + [row] * nk,
        out_shape=[jax.ShapeDtypeStruct((t, d), F32)] + [jax.ShapeDtypeStruct((1, d), F32)] * nk,
        compiler_params=_params("arbitrary"),
    )(x, *dhs, *gains, dres)


def _suffix_ones(n):
    r = lax.broadcasted_iota(jnp.int32, (n, n), 0)
    c = lax.broadcasted_iota(jnp.int32, (n, n), 1)
    return (r >= c).astype(BF16)


def _suffix_sum(l, ones):
    hi = l.astype(BF16)
    lo = (l - hi.astype(F32)).astype(BF16)
    return (jnp.dot(hi, ones, preferred_element_type=F32) + jnp.dot(lo, ones, preferred_element_type=F32))


def _log_one_minus_beta(z):
    return -(jnp.maximum(z, 0.0) + jnp.log(1.0 + jnp.exp(-jnp.abs(z))))


def _nt(a, b):
    return lax.dot_general(a, b, (((1,), (1,)), ((), ())), preferred_element_type=F32)


def _tn(a, b):
    return lax.dot_general(a, b, (((0,), (0,)), ((), ())), preferred_element_type=F32)


def _attention_fwd(qg, kv, *, tq=256):
    t, a2 = qg.shape
    a = a2 // 2
    npair = a // HEAD_PAIR
    tq = min(tq, t)
    scale = HEAD_DIM ** -0.5

    nq = t // tq

    def body(q_ref, k_ref, v_ref, ones_ref, o_ref, cin_ref):
        i = pl.program_id(1)
        ones = ones_ref[...]
        q_all = q_ref[...] * jnp.asarray(scale, BF16)
        qs = [q_all[:, h * HEAD_DIM:(h + 1) * HEAD_DIM] for h in range(2)]
        rows = lax.broadcasted_iota(jnp.int32, (tq, tq), 0)
        cols = lax.broadcasted_iota(jnp.int32, (tq, tq), 1)
        causal = cols < rows
        block_of_lane = lax.broadcasted_iota(jnp.int32, (tq, nq), 1)
        cin_ref[...] = jnp.zeros_like(cin_ref)

        def tile(j, carry, masked):
            start = pl.multiple_of(j * tq, tq)
            k_all = k_ref[pl.ds(start, tq), :]
            v_all = v_ref[pl.ds(start, tq), :]
            out = []
            for h in range(2):
                c, acc = carry[h]
                kh = k_all[:, h * HEAD_DIM:(h + 1) * HEAD_DIM]
                vh = v_all[:, h * HEAD_DIM:(h + 1) * HEAD_DIM]
                z = _nt(qs[h], kh)
                l = _log_one_minus_beta(z)
                if masked:
                    l = jnp.where(causal, l, 0.0)
                r_loc = _suffix_sum(l, ones)
                logw = z + (c + r_loc)
                if masked:
                    logw = jnp.where(causal, logw, -jnp.inf)
                w = jnp.exp(logw).astype(BF16)
                acc = acc + jnp.dot(w, vh, preferred_element_type=F32)
                c = c + r_loc[:, 0:1]
                cin_ref[h] = jnp.where(block_of_lane == j - 1, c, cin_ref[h])
                out.append((c, acc))
            return tuple(out)

        init = tuple((jnp.zeros((tq, 1), F32), jnp.zeros((tq, HEAD_DIM), F32)) for _ in range(2))
        carry = tile(i, init, True)
        carry = lax.fori_loop(0, i, lambda n, cr: tile(i - 1 - n, cr, False), carry)
        o_ref[...] = jnp.concatenate([carry[0][1], carry[1][1]], axis=1)

    return pl.pallas_call(
        body, name="attention_fwd", grid=(npair, nq),
        in_specs=[pl.BlockSpec((tq, HEAD_PAIR), lambda p, i: (i, p)),
                  pl.BlockSpec((t, HEAD_PAIR), lambda p, i: (0, p)),
                  pl.BlockSpec((t, HEAD_PAIR), lambda p, i: (0, npair + p)),
                  pl.BlockSpec((tq, tq), lambda p, i: (0, 0))],
        out_specs=[pl.BlockSpec((tq, HEAD_PAIR), lambda p, i: (i, p)),
                   pl.BlockSpec((None, 2, tq, nq), lambda p, i: (p, 0, i, 0))],
        out_shape=[jax.ShapeDtypeStruct((t, a), F32), jax.ShapeDtypeStruct((npair, 2, t, nq), F32)],
        compiler_params=_params("parallel", "arbitrary"),
    )(qg, kv, kv, _suffix_ones(tq))


def _attention_bwd(qg, kv, cin, do, *, tq=256):
    t, a2 = qg.shape
    a = a2 // 2
    npair = a // HEAD_PAIR
    tq = min(tq, t)
    nq = t // tq
    scale = HEAD_DIM ** -0.5

    def body(q_ref, k_ref, v_ref, cin_ref, do_ref, ones_ref, ones_t_ref, dq_ref, dkv_ref, dk_acc, dv_acc, sem):
        p = pl.program_id(0)
        i = pl.program_id(1)

        @pl.when(i == 0)
        def _():
            dk_acc[...] = jnp.zeros_like(dk_acc)
            dv_acc[...] = jnp.zeros_like(dv_acc)

        ones = ones_ref[...]
        ones_t = ones_t_ref[...]
        q_all = q_ref[...] * jnp.asarray(scale, BF16)
        do_bf = do_ref[...].astype(BF16)
        qs = [q_all[:, h * HEAD_DIM:(h + 1) * HEAD_DIM] for h in range(2)]
        dos = [do_bf[:, h * HEAD_DIM:(h + 1) * HEAD_DIM] for h in range(2)]
        rows = lax.broadcasted_iota(jnp.int32, (tq, tq), 0)
        cols = lax.broadcasted_iota(jnp.int32, (tq, tq), 1)
        causal = cols < rows
        block_of_lane = lax.broadcasted_iota(jnp.int32, (tq, nq), 1)

        def tile(j, carry, masked):
            start = pl.multiple_of(j * tq, tq)
            k_all = k_ref[pl.ds(start, tq), :]
            v_all = v_ref[pl.ds(start, tq), :]
            out, dks, dvs = [], [], []
            for h in range(2):
                gp, dq = carry[h]
                kh = k_all[:, h * HEAD_DIM:(h + 1) * HEAD_DIM]
                vh = v_all[:, h * HEAD_DIM:(h + 1) * HEAD_DIM]
                z = _nt(qs[h], kh)
                l = _log_one_minus_beta(z)
                if masked:
                    l = jnp.where(causal, l, 0.0)
                beta = jnp.exp(z + l)
                r_loc = _suffix_sum(l, ones)
                if masked:
                    logw = jnp.where(causal, z + r_loc, -jnp.inf)
                else:
                    c = jnp.sum(jnp.where(block_of_lane == j, cin_ref[h], 0.0), axis=1, keepdims=True)
                    logw = z + (c + r_loc)
                w = jnp.exp(logw)
                g = w * _nt(dos[h], vh)
                dvs.append(_tn(w.astype(BF16), dos[h]))
                g_pre = jnp.dot(g.astype(BF16), ones_t, preferred_element_type=F32)
                dz = g - beta * (gp + g_pre)
                if masked:
                    dz = jnp.where(causal, dz, 0.0)
                dz = dz.astype(BF16)
                dq = dq + jnp.dot(dz, kh, preferred_element_type=F32)
                dks.append(_tn(dz, qs[h]))
                out.append((gp + g_pre[:, tq - 1:tq], dq))
            dk_acc[pl.ds(start, tq), :] += jnp.concatenate(dks, axis=1)
            dv_acc[pl.ds(start, tq), :] += jnp.concatenate(dvs, axis=1)
            return tuple(out)

        init = tuple((jnp.zeros((tq, 1), F32), jnp.zeros((tq, HEAD_DIM), F32)) for _ in range(2))
        carry = lax.fori_loop(0, i, lambda j, cr: tile(j, cr, False), init)
        carry = tile(i, carry, True)
        dq_ref[...] = (jnp.concatenate([carry[0][1], carry[1][1]], axis=1) * scale).astype(dq_ref.dtype)

        @pl.when(i == nq - 1)
        def _():
            k_cols = pl.multiple_of(p * HEAD_PAIR, HEAD_PAIR)
            v_cols = pl.multiple_of((npair + p) * HEAD_PAIR, HEAD_PAIR)
            ck = pltpu.make_async_copy(dk_acc, dkv_ref.at[:, pl.ds(k_cols, HEAD_PAIR)], sem.at[0])
            cv = pltpu.make_async_copy(dv_acc, dkv_ref.at[:, pl.ds(v_cols, HEAD_PAIR)], sem.at[1])
            ck.start()
            cv.start()
            ck.wait()
            cv.wait()

    blk = pl.BlockSpec((tq, HEAD_PAIR), lambda p, i: (i, p))
    tri = pl.BlockSpec((tq, tq), lambda p, i: (0, 0))
    ones = _suffix_ones(tq)
    return pl.pallas_call(
        body, name="attention_bwd", grid=(npair, nq),
        in_specs=[blk,
                  pl.BlockSpec((t, HEAD_PAIR), lambda p, i: (0, p)),
                  pl.BlockSpec((t, HEAD_PAIR), lambda p, i: (0, npair + p)),
                  pl.BlockSpec((None, 2, tq, nq), lambda p, i: (p, 0, i, 0)),
                  blk, tri, tri],
        out_specs=[blk, pl.BlockSpec(memory_space=pl.ANY)],
        out_shape=[jax.ShapeDtypeStruct((t, a), BF16), jax.ShapeDtypeStruct((t, a2), F32)],
        scratch_shapes=[pltpu.VMEM((t, HEAD_PAIR), F32), pltpu.VMEM((t, HEAD_PAIR), F32),
                        pltpu.SemaphoreType.DMA((2,))],
        compiler_params=_params("arbitrary", "arbitrary"),
    )(qg, kv, kv, cin, do, ones, ones.T)


def _other_chips(x, y):
    return [(1 - x, y), (x, 1 - y), (1 - x, 1 - y)]


def _gather_blocks(wide, small):
    def body(w_ref, s_ref, wo_ref, so_ref, send_sems, recv_sems, local_sems):
        x, y, c = lax.axis_index("x"), lax.axis_index("y"), lax.axis_index("c")
        mine = 2 * x + y
        local = [pltpu.make_async_copy(w_ref, wo_ref.at[mine], local_sems.at[0]),
                 pltpu.make_async_copy(s_ref, so_ref.at[mine], local_sems.at[1])]
        for cp in local:
            cp.start()
        sends = []
        for k, (px, py) in enumerate(_other_chips(x, y)):
            for n, (src, dst) in enumerate(((w_ref, wo_ref), (s_ref, so_ref))):
                sends.append(pltpu.make_async_remote_copy(
                    src_ref=src, dst_ref=dst.at[mine], send_sem=send_sems.at[2 * k + n],
                    recv_sem=recv_sems.at[2 * k + n], device_id=(px, py, c), device_id_type=MESH))
        for cp in sends:
            cp.start()
        for k, (px, py) in enumerate(_other_chips(x, y)):
            for n, (src, dst) in enumerate(((w_ref, wo_ref), (s_ref, so_ref))):
                pltpu.make_async_remote_copy(
                    src_ref=src, dst_ref=dst.at[2 * px + py], send_sem=send_sems.at[2 * k + n],
                    recv_sem=recv_sems.at[2 * k + n], device_id=(px, py, c), device_id_type=MESH).wait_recv()
        for cp in sends:
            cp.wait_send()
        for cp in local:
            cp.wait()

    any_spec = pl.BlockSpec(memory_space=pl.ANY)
    return pl.pallas_call(
        body, name="gather_weights",
        in_specs=[any_spec, any_spec], out_specs=[any_spec, any_spec],
        out_shape=[jax.ShapeDtypeStruct((N_BLOCKS,) + wide.shape, wide.dtype),
                   jax.ShapeDtypeStruct((N_BLOCKS,) + small.shape, small.dtype)],
        scratch_shapes=[pltpu.SemaphoreType.DMA((6,)), pltpu.SemaphoreType.DMA((6,)), pltpu.SemaphoreType.DMA((2,))],
    )(wide, small)


def _scatter_blocks(grads):
    def body(g_ref, out_ref, send_sems, recv_sems, local_sem):
        x, y, c = lax.axis_index("x"), lax.axis_index("y"), lax.axis_index("c")
        mine = 2 * x + y
        local = pltpu.make_async_copy(g_ref.at[mine], out_ref.at[mine], local_sem)
        local.start()
        sends = []
        for k, (px, py) in enumerate(_other_chips(x, y)):
            sends.append(pltpu.make_async_remote_copy(
                src_ref=g_ref.at[2 * px + py], dst_ref=out_ref.at[mine], send_sem=send_sems.at[k],
                recv_sem=recv_sems.at[k], device_id=(px, py, c), device_id_type=MESH))
        for cp in sends:
            cp.start()
        for k, (px, py) in enumerate(_other_chips(x, y)):
            pltpu.make_async_remote_copy(
                src_ref=g_ref.at[mine], dst_ref=out_ref.at[2 * px + py], send_sem=send_sems.at[k],
                recv_sem=recv_sems.at[k], device_id=(px, py, c), device_id_type=MESH).wait_recv()
        for cp in sends:
            cp.wait_send()
        local.wait()

    any_spec = pl.BlockSpec(memory_space=pl.ANY)
    return pl.pallas_call(
        body, name="scatter_grads",
        in_specs=[any_spec], out_specs=any_spec,
        out_shape=jax.ShapeDtypeStruct(grads.shape, grads.dtype),
        scratch_shapes=[pltpu.SemaphoreType.DMA((3,)), pltpu.SemaphoreType.DMA((3,)), pltpu.SemaphoreType.DMA],
    )(grads)


def _swap_with_sibling(part):
    def body(p_ref, out_ref, send_sem, recv_sem):
        x, y, c = lax.axis_index("x"), lax.axis_index("y"), lax.axis_index("c")
        cp = pltpu.make_async_remote_copy(src_ref=p_ref, dst_ref=out_ref, send_sem=send_sem, recv_sem=recv_sem,
                                          device_id=(x, y, 1 - c), device_id_type=MESH)
        cp.start()
        cp.wait()

    any_spec = pl.BlockSpec(memory_space=pl.ANY)
    return pl.pallas_call(
        body, name="swap_sibling",
        in_specs=[any_spec], out_specs=any_spec,
        out_shape=jax.ShapeDtypeStruct(part.shape, part.dtype),
        scratch_shapes=[pltpu.SemaphoreType.DMA, pltpu.SemaphoreType.DMA],
    )(part)


def _sum_slots(parts, *, tr=256):
    _, r, d = parts.shape
    tr = tr if r % tr == 0 else SMALL_ROWS

    def body(p_ref, o_ref):
        o_ref[...] = ((p_ref[0] + p_ref[1]) + p_ref[2]) + p_ref[3]

    return pl.pallas_call(
        body, name="sum_slots", grid=(r // tr,),
        in_specs=[pl.BlockSpec((N_BLOCKS, tr, d), lambda i: (0, i, 0))],
        out_specs=pl.BlockSpec((tr, d), lambda i: (i, 0)),
        out_shape=jax.ShapeDtypeStruct((r, d), F32),
        compiler_params=_params("parallel"),
    )(parts)


def _adamw(part_mine, part_sibling, w, m, v, *, tr=256):
    r, d = w.shape
    tr = tr if r % tr == 0 else SMALL_ROWS
    m_scale = 1.0 / (1.0 - ADAM_B1 ** ADAM_STEP)
    v_scale = 1.0 / (1.0 - ADAM_B2 ** ADAM_STEP)

    def body(a_ref, b_ref, w_ref, m_ref, v_ref, g_ref, d_ref, mo_ref, vo_ref):
        g = a_ref[...] + b_ref[...]
        m_new = ADAM_B1 * m_ref[...] + (1.0 - ADAM_B1) * g
        v_new = ADAM_B2 * v_ref[...] + (1.0 - ADAM_B2) * (g * g)
        g_ref[...] = g
        mo_ref[...] = m_new
        vo_ref[...] = v_new
        d_ref[...] = -ADAM_LR * ((m_new * m_scale) / (jnp.sqrt(v_new * v_scale) + ADAM_EPS) + ADAM_WD * w_ref[...])

    blk = pl.BlockSpec((tr, d), lambda i: (i, 0))
    return pl.pallas_call(
        body, name="adamw", grid=(r // tr,),
        in_specs=[blk] * 5, out_specs=[blk] * 4,
        out_shape=[jax.ShapeDtypeStruct((r, d), F32)] * 4,
        compiler_params=_params("parallel"),
    )(part_mine, part_sibling, w, m, v)


def _pack_wide(w_in_a, w_out_a, w_kv, w_in_b, w_out_b):
    d = w_in_a.shape[-1]
    return jnp.concatenate([w_in_a[0], w_out_a[0], w_kv, w_in_b[0], w_out_b[0].reshape(-1, d)], axis=0)


def _pack_small(conv_a, norm_a, norm_kv, norm_b, norm_f):
    d = norm_kv.shape[-1]
    rows = [jnp.concatenate([conv_a[0].reshape(-1), norm_a[0]])[None, :], norm_kv.reshape(1, d),
            norm_b.reshape(1, d), norm_f.reshape(1, d), jnp.zeros((SMALL_ROWS - 4, d), F32)]
    return jnp.concatenate(rows, axis=0)


def _unpack(packed, shapes):
    d = packed.shape[1]
    dq = d // N_BLOCKS
    att = shapes["w_out_b"][1]
    o = 0
    out = {}
    for name, rows in (("w_in_a", d), ("w_out_a", dq), ("w_kv", dq), ("w_in_b", dq), ("w_out_b", att // N_BLOCKS)):
        out[name] = packed[o:o + rows].reshape(shapes[name])
        o += rows
    out["conv_a"] = packed[o, :3 * dq].reshape(shapes["conv_a"])
    out["norm_a"] = packed[o, 3 * dq:].reshape(shapes["norm_a"])
    out["norm_kv"] = packed[o + 1].reshape(shapes["norm_kv"])
    out["norm_b"] = packed[o + 2].reshape(shapes["norm_b"])
    out["norm_f"] = packed[o + 3].reshape(shapes["norm_f"])
    return out


WEIGHTS = ["norm_a", "w_in_a", "conv_a", "w_out_a", "norm_kv", "w_kv", "norm_b", "w_in_b", "w_out_b", "norm_f"]


def kernel(x, norm_a, w_in_a, conv_a, w_out_a, norm_kv, w_kv, norm_b, w_in_b, w_out_b, norm_f, loss_target, m_norm_a, m_w_in_a, m_conv_a, m_w_out_a, m_norm_kv, m_w_kv, m_norm_b, m_w_in_b, m_w_out_b, m_norm_f, v_norm_a, v_w_in_a, v_conv_a, v_w_out_a, v_norm_kv, v_w_kv, v_norm_b, v_w_in_b, v_w_out_b, v_norm_f):
    d = x.shape[-1]
    dq = d // N_BLOCKS
    att = w_out_b.shape[1]
    xs = x[0]
    target = loss_target[0]
    shapes = dict(norm_a=norm_a.shape, w_in_a=w_in_a.shape, conv_a=conv_a.shape, w_out_a=w_out_a.shape,
                  norm_kv=norm_kv.shape, w_kv=w_kv.shape, norm_b=norm_b.shape, w_in_b=w_in_b.shape,
                  w_out_b=w_out_b.shape, norm_f=norm_f.shape)

    w_pack = jnp.concatenate([_pack_wide(w_in_a, w_out_a, w_kv, w_in_b, w_out_b),
                              _pack_small(conv_a, norm_a, norm_kv, norm_b, norm_f)], axis=0)
    m_pack = jnp.concatenate([_pack_wide(m_w_in_a, m_w_out_a, m_w_kv, m_w_in_b, m_w_out_b),
                              _pack_small(m_conv_a, m_norm_a, m_norm_kv, m_norm_b, m_norm_f)], axis=0)
    v_pack = jnp.concatenate([_pack_wide(v_w_in_a, v_w_out_a, v_w_kv, v_w_in_b, v_w_out_b),
                              _pack_small(v_conv_a, v_norm_a, v_norm_kv, v_norm_b, v_norm_f)], axis=0)
    n_wide = w_pack.shape[0] - SMALL_ROWS
    wide_all, small_all = _gather_blocks(w_pack[:n_wide].astype(BF16), w_pack[n_wide:])

    o1, o2, o3, o4 = d, d + dq, d + 2 * dq, d + 3 * dq
    wf_in_a = jnp.concatenate([wide_all[b, 0:o1] for b in range(N_BLOCKS)], axis=1)
    wf_out_a = wide_all[:, o1:o2].reshape(d, d)
    wf_kv = wide_all[:, o2:o3].reshape(d, 2 * att)
    wf_in_b = wide_all[:, o3:o4].reshape(d, 2 * att)
    wf_out_b = jnp.concatenate([wide_all[b, o4:].reshape(att, dq) for b in range(N_BLOCKS)], axis=1)
    conv_full = jnp.concatenate([small_all[b, 0, :3 * dq].reshape(3, dq) for b in range(N_BLOCKS)], axis=1)
    gain_a = jnp.concatenate([small_all[b, 0, 3 * dq:] for b in range(N_BLOCKS)])[None, :]
    gain_kv = norm_kv.reshape(1, d)
    gain_b = norm_b.reshape(1, d)
    gain_f = norm_f.reshape(1, d)

    u = _norm_matmul(xs, gain_a, wf_in_a, out_dtype=BF16, name="in_proj_a")
    mix = _mixer_fwd(u, conv_full)
    x1 = _matmul(mix, wf_out_a, res=xs, name="out_proj_a")
    kv = _norm_matmul(x1, gain_kv, wf_kv, out_dtype=BF16, name="kv_proj")
    qg = _norm_matmul(x1, gain_b, wf_in_b, out_dtype=BF16, name="in_proj_b")
    o, cin = _attention_fwd(qg, kv)
    gated = _gate_fwd(o, qg)
    x2 = _matmul(gated, wf_out_b, res=x1, name="out_proj_b")
    dx2, loss_cols, g_norm_f = _loss_head(x2, gain_f, target)
    loss = lax.psum(jnp.sum(loss_cols), ("x", "y", "c"))

    g_w_out_b = _matmul_tn(gated, dx2, name="grad_w_out_b")
    d_gated = _matmul(dx2, wf_out_b, trans_b=True, name="d_gated")
    do, dg = _gate_bwd(d_gated, o, qg)
    dq_att, dkv = _attention_bwd(qg, kv, cin, do)
    dqg = jnp.concatenate([dq_att, dg], axis=1)
    g_w_in_b = _matmul_tn(x1, dqg, gain=gain_b, name="grad_w_in_b")
    g_w_kv = _matmul_tn(x1, dkv, gain=gain_kv, name="grad_w_kv")
    dh_b = _matmul(dqg, wf_in_b, trans_b=True, name="d_h_b")
    dh_kv = _matmul(dkv, wf_kv, trans_b=True, name="d_h_kv")
    dx1, g_norm_b, g_norm_kv = _rms_bwd(x1, [dh_b, dh_kv], [gain_b, gain_kv], dx2, name="rms_bwd_b")
    g_w_out_a = _matmul_tn(mix, dx1, name="grad_w_out_a")
    dmix = _matmul(dx1, wf_out_a, trans_b=True, name="d_mix")
    du, g_conv = _mixer_bwd(u, dmix, conv_full)
    g_w_in_a = _matmul_tn(xs, du, gain=gain_a, blocked_out=True, tn=d, name="grad_w_in_a")
    dh_a = _matmul(du, wf_in_a, trans_b=True, name="d_h_a")
    dx, g_norm_a = _rms_bwd(xs, [dh_a], [gain_a], dx1, name="rms_bwd_a")

    small_rows = jnp.concatenate([
        jnp.concatenate([g_conv[:3].reshape(3, N_BLOCKS, dq).transpose(1, 0, 2).reshape(N_BLOCKS, 3 * dq),
                         g_norm_a.reshape(N_BLOCKS, dq)], axis=1)[:, None, :],
        jnp.broadcast_to(g_norm_kv[None], (N_BLOCKS, 1, d)),
        jnp.broadcast_to(g_norm_b[None], (N_BLOCKS, 1, d)),
        jnp.broadcast_to(g_norm_f[None], (N_BLOCKS, 1, d)),
        jnp.zeros((N_BLOCKS, SMALL_ROWS - 4, d), F32)], axis=1)
    g_pack = jnp.concatenate([
        g_w_in_a,
        g_w_out_a.reshape(N_BLOCKS, dq, d),
        g_w_kv.reshape(N_BLOCKS, dq, d),
        g_w_in_b.reshape(N_BLOCKS, dq, d),
        g_w_out_b.reshape(att, N_BLOCKS, dq).transpose(1, 0, 2).reshape(N_BLOCKS, att // N_BLOCKS, d),
        small_rows], axis=1)

    mine = _sum_slots(_scatter_blocks(g_pack))
    theirs = _swap_with_sibling(mine)
    g_tot, delta, m_new, v_new = _adamw(mine, theirs, w_pack, m_pack, v_pack)

    grads = _unpack(g_tot, shapes)
    deltas = _unpack(delta, shapes)
    new_m = _unpack(m_new, shapes)
    new_v = _unpack(v_new, shapes)
    return (loss, dx[None], *[grads[n] for n in WEIGHTS], *[deltas[n] for n in WEIGHTS],
            *[new_m[n] for n in WEIGHTS], *[new_v[n] for n in WEIGHTS])
```

```python
import functools

import jax
import jax.numpy as jnp
from jax import lax
from jax.experimental import pallas as pl
from jax.experimental.pallas import tpu as pltpu

F32 = jnp.float32
BF16 = jnp.bfloat16
MESH = pl.DeviceIdType.MESH

HEAD_DIM = 64
HEAD_PAIR = 2 * HEAD_DIM
RMS_EPS = 1e-6
ADAM_LR = 0.001
ADAM_B1 = 0.9
ADAM_B2 = 0.999
ADAM_EPS = 1e-08
ADAM_WD = 0.01
ADAM_STEP = 10
N_BLOCKS = 4
SMALL_ROWS = 8
ROW_TILE = 256
VMEM_LIMIT_V7X = 56 * 1024 * 1024


def _params(*sem):
    return pltpu.CompilerParams(dimension_semantics=sem if sem else None, vmem_limit_bytes=VMEM_LIMIT_V7X)


def _sigmoid(x):
    return 1.0 / (1.0 + jnp.exp(-x))


def _rms_normalize(x):
    rstd = lax.rsqrt(jnp.mean(x * x, axis=-1, keepdims=True) + RMS_EPS)
    return x * rstd, rstd


def _norm_matmul(x, gain, w, *, out_dtype, name, tm=1024, tn=1024):
    t, d = x.shape
    n = w.shape[1]
    tm, tn = min(tm, t), min(tn, n)

    def body(x_ref, g_ref, w_ref, o_ref):
        xn, _ = _rms_normalize(x_ref[...])
        h = (xn * g_ref[...]).astype(BF16)
        o_ref[...] = jnp.dot(h, w_ref[...], preferred_element_type=F32).astype(o_ref.dtype)

    return pl.pallas_call(
        body, name=name, grid=(t // tm, n // tn),
        in_specs=[pl.BlockSpec((tm, d), lambda i, j: (i, 0)),
                  pl.BlockSpec((1, d), lambda i, j: (0, 0)),
                  pl.BlockSpec((d, tn), lambda i, j: (0, j))],
        out_specs=pl.BlockSpec((tm, tn), lambda i, j: (i, j)),
        out_shape=jax.ShapeDtypeStruct((t, n), out_dtype),
        compiler_params=_params("parallel", "arbitrary"),
    )(x, gain, w)


def _matmul(a, w, *, name, trans_b=False, res=None, out_dtype=F32, tm=1024, tn=1024, tk=1024):
    t, k = a.shape
    n = w.shape[0] if trans_b else w.shape[1]
    tm, tn, tk = min(tm, t), min(tn, n), min(tk, k)
    nk = k // tk

    def body(*refs):
        if res is None:
            a_ref, w_ref, o_ref, acc = refs
        else:
            a_ref, w_ref, r_ref, o_ref, acc = refs
        kk = pl.program_id(2)

        @pl.when(kk == 0)
        def _():
            acc[...] = jnp.zeros_like(acc)

        av = a_ref[...].astype(BF16)
        if trans_b:
            acc[...] += lax.dot_general(av, w_ref[...], (((1,), (1,)), ((), ())), preferred_element_type=F32)
        else:
            acc[...] += jnp.dot(av, w_ref[...], preferred_element_type=F32)

        @pl.when(kk == nk - 1)
        def _():
            r = acc[...]
            if res is not None:
                r = r + r_ref[...]
            o_ref[...] = r.astype(o_ref.dtype)

    in_specs = [pl.BlockSpec((tm, tk), lambda i, j, kk: (i, kk)),
                pl.BlockSpec((tn, tk), lambda i, j, kk: (j, kk)) if trans_b
                else pl.BlockSpec((tk, tn), lambda i, j, kk: (kk, j))]
    args = [a, w]
    if res is not None:
        in_specs.append(pl.BlockSpec((tm, tn), lambda i, j, kk: (i, j)))
        args.append(res)
    return pl.pallas_call(
        body, name=name, grid=(t // tm, n // tn, nk),
        in_specs=in_specs,
        out_specs=pl.BlockSpec((tm, tn), lambda i, j, kk: (i, j)),
        out_shape=jax.ShapeDtypeStruct((t, n), out_dtype),
        scratch_shapes=[pltpu.VMEM((tm, tn), F32)],
        compiler_params=_params("parallel", "parallel", "arbitrary"),
    )(*args)


def _matmul_tn(a, b, *, name, gain=None, blocked_out=False, tt=1024, tn=512):
    t, k = a.shape
    n = b.shape[1]
    tt, tn = min(tt, t), min(tn, n)

    def body(*refs):
        if gain is None:
            a_ref, b_ref, o_ref = refs
            av = a_ref[...].astype(BF16)
        else:
            a_ref, g_ref, b_ref, o_ref = refs
            xn, _ = _rms_normalize(a_ref[...])
            av = (xn * g_ref[...]).astype(BF16)

        @pl.when(pl.program_id(1) == 0)
        def _():
            o_ref[...] = jnp.zeros_like(o_ref)

        o_ref[...] += lax.dot_general(av, b_ref[...].astype(BF16), (((0,), (0,)), ((), ())),
                                      preferred_element_type=F32)

    in_specs = [pl.BlockSpec((tt, k), lambda j, s: (s, 0))]
    args = [a]
    if gain is not None:
        in_specs.append(pl.BlockSpec((1, k), lambda j, s: (0, 0)))
        args.append(gain)
    in_specs.append(pl.BlockSpec((tt, tn), lambda j, s: (s, j)))
    args.append(b)
    if blocked_out:
        out_spec = pl.BlockSpec((None, k, tn), lambda j, s: (j, 0, 0))
        out_shape = jax.ShapeDtypeStruct((n // tn, k, tn), F32)
    else:
        out_spec = pl.BlockSpec((k, tn), lambda j, s: (0, j))
        out_shape = jax.ShapeDtypeStruct((k, n), F32)
    return pl.pallas_call(
        body, name=name, grid=(n // tn, t // tt),
        in_specs=in_specs, out_specs=out_spec, out_shape=out_shape,
        compiler_params=_params("parallel", "arbitrary"),
    )(*args)


HALO = 16


def _shift_down(v, halo, k):
    rows = lax.broadcasted_iota(jnp.int32, v.shape, 0)
    out = pltpu.roll(v, k, 0)
    for r in range(k):
        out = jnp.where(rows == r, halo[HALO - k + r:HALO - k + r + 1, :], out)
    return out


def _shift_up(v, halo, k):
    n = v.shape[0]
    rows = lax.broadcasted_iota(jnp.int32, v.shape, 0)
    out = pltpu.roll(v, n - k, 0)
    for r in range(k):
        out = jnp.where(rows == n - k + r, halo[r:r + 1, :], out)
    return out


def _mixer_fwd(u, conv, *, tm=256):
    t, d4 = u.shape
    d = d4 // 4
    tm = min(tm, t)
    hb = tm // HALO

    def body(b_ref, c_ref, x_ref, g_ref, ch_ref, xh_ref, w_ref, m_ref):
        i = pl.program_id(0)
        y1 = c_ref[...].astype(F32) * x_ref[...].astype(F32)
        prev = ch_ref[...].astype(F32) * xh_ref[...].astype(F32)
        prev = jnp.where(i == 0, 0.0, prev)
        w = w_ref[...]
        yc = w[2:3, :] * y1 + w[1:2, :] * _shift_down(y1, prev, 1) + w[0:1, :] * _shift_down(y1, prev, 2)
        g = g_ref[...].astype(F32)
        m_ref[...] = (b_ref[...].astype(F32) * yc * (g * _sigmoid(g))).astype(m_ref.dtype)

    def col(c):
        return pl.BlockSpec((tm, d), lambda i: (i, c))

    def prev_rows(c):
        return pl.BlockSpec((HALO, d), lambda i: (jnp.maximum(i * hb - 1, 0), c))

    return pl.pallas_call(
        body, name="mixer_fwd", grid=(t // tm,),
        in_specs=[col(0), col(1), col(2), col(3), prev_rows(1), prev_rows(2), pl.BlockSpec((3, d), lambda i: (0, 0))],
        out_specs=pl.BlockSpec((tm, d), lambda i: (i, 0)),
        out_shape=jax.ShapeDtypeStruct((t, d), BF16),
        compiler_params=_params("parallel"),
    )(u, u, u, u, u, u, conv)


def _mixer_bwd(u, dm, conv, *, tm=256):
    t, d4 = u.shape
    d = d4 // 4
    tm = min(tm, t)
    hb = tm // HALO
    nb = t // tm

    def body(b_ref, c_ref, x_ref, g_ref, dm_ref, ch_ref, xh_ref, bn_ref, gn_ref, dmn_ref, w_ref, du_ref, gw_ref):
        i = pl.program_id(0)
        w = w_ref[...]
        b = b_ref[...].astype(F32)
        c = c_ref[...].astype(F32)
        xin = x_ref[...].astype(F32)
        g = g_ref[...].astype(F32)
        dm_v = dm_ref[...]
        y1 = c * xin
        prev = jnp.where(i == 0, 0.0, ch_ref[...].astype(F32) * xh_ref[...].astype(F32))
        y1m1 = _shift_down(y1, prev, 1)
        y1m2 = _shift_down(y1, prev, 2)
        yc = w[2:3, :] * y1 + w[1:2, :] * y1m1 + w[0:1, :] * y1m2
        sg = _sigmoid(g)
        s = g * sg
        ds = sg * (1.0 + g * (1.0 - sg))
        dyc = dm_v * b * s
        gn = gn_ref[...].astype(F32)
        nxt = dmn_ref[...] * bn_ref[...].astype(F32) * (gn * _sigmoid(gn))
        nxt = jnp.where(i == nb - 1, 0.0, nxt)
        dy1 = w[2:3, :] * dyc + w[1:2, :] * _shift_up(dyc, nxt, 1) + w[0:1, :] * _shift_up(dyc, nxt, 2)
        du_ref[:, 0:d] = (dm_v * yc * s).astype(du_ref.dtype)
        du_ref[:, d:2 * d] = (dy1 * xin).astype(du_ref.dtype)
        du_ref[:, 2 * d:3 * d] = (dy1 * c).astype(du_ref.dtype)
        du_ref[:, 3 * d:4 * d] = (dm_v * b * yc * ds).astype(du_ref.dtype)

        @pl.when(i == 0)
        def _():
            gw_ref[...] = jnp.zeros_like(gw_ref)

        gw_ref[0:1, :] += jnp.sum(dyc * y1m2, axis=0, keepdims=True)
        gw_ref[1:2, :] += jnp.sum(dyc * y1m1, axis=0, keepdims=True)
        gw_ref[2:3, :] += jnp.sum(dyc * y1, axis=0, keepdims=True)

    def col(c):
        return pl.BlockSpec((tm, d), lambda i: (i, c))

    def prev_rows(c):
        return pl.BlockSpec((HALO, d), lambda i: (jnp.maximum(i * hb - 1, 0), c))

    def next_rows(c):
        return pl.BlockSpec((HALO, d), lambda i: (jnp.minimum((i + 1) * hb, nb * hb - 1), c))

    return pl.pallas_call(
        body, name="mixer_bwd", grid=(nb,),
        in_specs=[col(0), col(1), col(2), col(3), pl.BlockSpec((tm, d), lambda i: (i, 0)),
                  prev_rows(1), prev_rows(2), next_rows(0), next_rows(3),
                  pl.BlockSpec((HALO, d), lambda i: (jnp.minimum((i + 1) * hb, nb * hb - 1), 0)),
                  pl.BlockSpec((3, d), lambda i: (0, 0))],
        out_specs=[pl.BlockSpec((tm, d4), lambda i: (i, 0)), pl.BlockSpec((SMALL_ROWS, d), lambda i: (0, 0))],
        out_shape=[jax.ShapeDtypeStruct((t, d4), BF16), jax.ShapeDtypeStruct((SMALL_ROWS, d), F32)],
        compiler_params=_params("arbitrary"),
    )(u, u, u, u, dm, u, u, u, u, dm, conv)


def _gate_fwd(o, qg, *, tm=512):
    t, a = o.shape
    tm = min(tm, t)

    def body(o_ref, g_ref, out_ref):
        g = g_ref[...].astype(F32)
        out_ref[...] = (o_ref[...] * (g * _sigmoid(g))).astype(out_ref.dtype)

    return pl.pallas_call(
        body, name="gate_fwd", grid=(t // tm,),
        in_specs=[pl.BlockSpec((tm, a), lambda i: (i, 0)), pl.BlockSpec((tm, a), lambda i: (i, 1))],
        out_specs=pl.BlockSpec((tm, a), lambda i: (i, 0)),
        out_shape=jax.ShapeDtypeStruct((t, a), BF16),
        compiler_params=_params("parallel"),
    )(o, qg)


def _gate_bwd(da, o, qg, *, tm=512):
    t, a = o.shape
    tm = min(tm, t)

    def body(da_ref, o_ref, g_ref, do_ref, dg_ref):
        g = g_ref[...].astype(F32)
        sg = _sigmoid(g)
        da_v = da_ref[...]
        do_ref[...] = (da_v * (g * sg)).astype(do_ref.dtype)
        dg_ref[...] = (da_v * o_ref[...] * (sg * (1.0 + g * (1.0 - sg)))).astype(dg_ref.dtype)

    blk = pl.BlockSpec((tm, a), lambda i: (i, 0))
    return pl.pallas_call(
        body, name="gate_bwd", grid=(t // tm,),
        in_specs=[blk, blk, pl.BlockSpec((tm, a), lambda i: (i, 1))],
        out_specs=[blk, blk],
        out_shape=[jax.ShapeDtypeStruct((t, a), F32), jax.ShapeDtypeStruct((t, a), BF16)],
        compiler_params=_params("parallel"),
    )(da, o, qg)


def _loss_head(x2, gain, target, *, tm=512):
    t, d = x2.shape
    tm = min(tm, t)

    def body(x_ref, g_ref, y_ref, dx_ref, loss_ref, gg_ref):
        @pl.when(pl.program_id(0) == 0)
        def _():
            loss_ref[...] = jnp.zeros_like(loss_ref)
            gg_ref[...] = jnp.zeros_like(gg_ref)

        xn, rstd = _rms_normalize(x_ref[...])
        gv = g_ref[...]
        err = xn * gv - y_ref[...]
        loss_ref[...] += jnp.sum(err * err, axis=0, keepdims=True) * (0.5 / d)
        dy = err * (1.0 / d)
        gg_ref[...] += jnp.sum(dy * xn, axis=0, keepdims=True)
        dxn = dy * gv
        dx_ref[...] = rstd * (dxn - xn * jnp.mean(dxn * xn, axis=-1, keepdims=True))

    blk = pl.BlockSpec((tm, d), lambda i: (i, 0))
    row = pl.BlockSpec((1, d), lambda i: (0, 0))
    return pl.pallas_call(
        body, name="loss_head", grid=(t // tm,),
        in_specs=[blk, row, blk], out_specs=[blk, row, row],
        out_shape=[jax.ShapeDtypeStruct((t, d), F32), jax.ShapeDtypeStruct((1, d), F32),
                   jax.ShapeDtypeStruct((1, d), F32)],
        compiler_params=_params("arbitrary"),
    )(x2, gain, target)


def _rms_bwd(x, dhs, gains, dres, *, name, tm=512):
    t, d = x.shape
    tm = min(tm, t)
    nk = len(dhs)

    def body(*refs):
        x_ref = refs[0]
        dh_refs = refs[1:1 + nk]
        g_refs = refs[1 + nk:1 + 2 * nk]
        dres_ref = refs[1 + 2 * nk]
        dx_ref = refs[2 + 2 * nk]
        gg_refs = refs[3 + 2 * nk:]

        @pl.when(pl.program_id(0) == 0)
        def _():
            for gg in gg_refs:
                gg[...] = jnp.zeros_like(gg)

        xn, rstd = _rms_normalize(x_ref[...])
        dxn = None
        for dh_ref, g_ref, gg in zip(dh_refs, g_refs, gg_refs):
            dh = dh_ref[...]
            gg[...] += jnp.sum(dh * xn, axis=0, keepdims=True)
            term = dh * g_ref[...]
            dxn = term if dxn is None else dxn + term
        dx_ref[...] = dres_ref[...] + rstd * (dxn - xn * jnp.mean(dxn * xn, axis=-1, keepdims=True))

    blk = pl.BlockSpec((tm, d), lambda i: (i, 0))
    row = pl.BlockSpec((1, d), lambda i: (0, 0))
    return pl.pallas_call(
        body, name=name, grid=(t // tm,),
        in_specs=[blk] + [blk] * nk + [row] * nk + [blk],
        out_specs=[blk] + [row] * nk,
        out_shape=[jax.ShapeDtypeStruct((t, d), F32)] + [jax.ShapeDtypeStruct((1, d), F32)] * nk,
        compiler_params=_params("arbitrary"),
    )(x, *dhs, *gains, dres)


def _suffix_ones(n):
    r = lax.broadcasted_iota(jnp.int32, (n, n), 0)
    c = lax.broadcasted_iota(jnp.int32, (n, n), 1)
    return (r >= c).astype(BF16)


def _suffix_sum(l, ones):
    hi = l.astype(BF16)
    lo = (l - hi.astype(F32)).astype(BF16)
    return (jnp.dot(hi, ones, preferred_element_type=F32) + jnp.dot(lo, ones, preferred_element_type=F32))


def _log_one_minus_beta(z):
    return -(jnp.maximum(z, 0.0) + jnp.log(1.0 + jnp.exp(-jnp.abs(z))))


def _nt(a, b):
    return lax.dot_general(a, b, (((1,), (1,)), ((), ())), preferred_element_type=F32)


def _tn(a, b):
    return lax.dot_general(a, b, (((0,), (0,)), ((), ())), preferred_element_type=F32)


EXP_UNDERFLOW = 104.0
NORM_MARGIN = 1.01


def _key_norm_bounds(kv, *, tq):
    t, a2 = kv.shape
    a = a2 // 2
    heads = a // HEAD_DIM
    nq = t // tq

    def body(k_ref, o_ref):
        k = k_ref[...].astype(F32)
        k2 = k * k
        lane = lax.broadcasted_iota(jnp.int32, (1, heads), 1)
        out = jnp.zeros((1, heads), F32)
        for h in range(heads):
            n2 = jnp.sum(k2[:, h * HEAD_DIM:(h + 1) * HEAD_DIM], axis=1, keepdims=True)
            out = jnp.where(lane == h, jnp.max(n2, axis=0, keepdims=True), out)
        o_ref[...] = jnp.sqrt(out)

    norms = pl.pallas_call(
        body, name="key_norms", grid=(nq,),
        in_specs=[pl.BlockSpec((tq, a), lambda i: (i, 0))],
        out_specs=pl.BlockSpec((None, 1, heads), lambda i: (i, 0, 0)),
        out_shape=jax.ShapeDtypeStruct((nq, 1, heads), F32),
        compiler_params=_params("parallel"),
    )(kv)
    return lax.cummax(norms[:, 0, :], axis=0).T.reshape(-1)


def _attention_fwd(qg, kv, *, tq=256):
    t, a2 = qg.shape
    a = a2 // 2
    npair = a // HEAD_PAIR
    tq = min(tq, t)
    scale = HEAD_DIM ** -0.5
    nq = t // tq

    def body(kmax_ref, q_ref, k_ref, v_ref, ones_ref, o_ref, cin_ref, first_ref):
        p = pl.program_id(0)
        i = pl.program_id(1)
        ones = ones_ref[...]
        q_all = q_ref[...] * jnp.asarray(scale, BF16)
        qs = [q_all[:, h * HEAD_DIM:(h + 1) * HEAD_DIM] for h in range(2)]
        q2 = q_all.astype(F32) * q_all.astype(F32)
        q_norm = [jnp.sqrt(jnp.sum(q2[:, h * HEAD_DIM:(h + 1) * HEAD_DIM], axis=1, keepdims=True)) * NORM_MARGIN
                  for h in range(2)]
        rows = lax.broadcasted_iota(jnp.int32, (tq, tq), 0)
        cols = lax.broadcasted_iota(jnp.int32, (tq, tq), 1)
        causal = cols < rows
        block_of_lane = lax.broadcasted_iota(jnp.int32, (tq, nq), 1)
        cin_ref[...] = jnp.zeros_like(cin_ref)

        def any_weight_left_of(j, carry):
            jj = jnp.maximum(j - 1, 0)
            bound = None
            for h in range(2):
                top = jnp.max(carry[h][0] + q_norm[h] * kmax_ref[(2 * p + h) * nq + jj])
                bound = top if bound is None else jnp.maximum(bound, top)
            return bound > -EXP_UNDERFLOW

        def tile(j, carry, masked):
            start = pl.multiple_of(j * tq, tq)
            k_all = k_ref[pl.ds(start, tq), :]
            v_all = v_ref[pl.ds(start, tq), :]
            out = []
            for h in range(2):
                c, acc = carry[h]
                kh = k_all[:, h * HEAD_DIM:(h + 1) * HEAD_DIM]
                vh = v_all[:, h * HEAD_DIM:(h + 1) * HEAD_DIM]
                z = _nt(qs[h], kh)
                l = _log_one_minus_beta(z)
                if masked:
                    l = jnp.where(causal, l, 0.0)
                r_loc = _suffix_sum(l, ones)
                logw = z + (c + r_loc)
                if masked:
                    logw = jnp.where(causal, logw, -jnp.inf)
                w = jnp.exp(logw).astype(BF16)
                acc = acc + jnp.dot(w, vh, preferred_element_type=F32)
                c = c + r_loc[:, 0:1]
                cin_ref[h] = jnp.where(block_of_lane == j - 1, c, cin_ref[h])
                out.append((c, acc))
            return tuple(out)

        init = tuple((jnp.zeros((tq, 1), F32), jnp.zeros((tq, HEAD_DIM), F32)) for _ in range(2))
        carry = tile(i, init, True)

        def visit(state):
            j, _, cr = state
            cr = tile(j - 1, cr, False)
            return j - 1, any_weight_left_of(j - 1, cr), cr

        first, _, carry = lax.while_loop(lambda s: (s[0] > 0) & s[1], visit,
                                         (i, any_weight_left_of(i, carry), carry))
        first_ref[p * nq + i] = first
        o_ref[...] = jnp.concatenate([carry[0][1], carry[1][1]], axis=1)

    return pl.pallas_call(
        body, name="attention_fwd",
        grid_spec=pltpu.PrefetchScalarGridSpec(
            num_scalar_prefetch=1, grid=(npair, nq),
            in_specs=[pl.BlockSpec((tq, HEAD_PAIR), lambda p, i, km: (i, p)),
                      pl.BlockSpec((t, HEAD_PAIR), lambda p, i, km: (0, p)),
                      pl.BlockSpec((t, HEAD_PAIR), lambda p, i, km: (0, npair + p)),
                      pl.BlockSpec((tq, tq), lambda p, i, km: (0, 0))],
            out_specs=[pl.BlockSpec((tq, HEAD_PAIR), lambda p, i, km: (i, p)),
                       pl.BlockSpec((None, 2, tq, nq), lambda p, i, km: (p, 0, i, 0)),
                       pl.BlockSpec(memory_space=pltpu.SMEM)]),
        out_shape=[jax.ShapeDtypeStruct((t, a), F32), jax.ShapeDtypeStruct((npair, 2, t, nq), F32),
                   jax.ShapeDtypeStruct((npair * nq,), jnp.int32)],
        compiler_params=_params("arbitrary", "arbitrary"),
    )(_key_norm_bounds(kv, tq=tq), qg, kv, kv, _suffix_ones(tq))


def _attention_bwd(qg, kv, cin, first, do, *, tq=256):
    t, a2 = qg.shape
    a = a2 // 2
    npair = a // HEAD_PAIR
    tq = min(tq, t)
    nq = t // tq
    scale = HEAD_DIM ** -0.5

    def body(first_ref, q_ref, k_ref, v_ref, cin_ref, do_ref, ones_ref, ones_t_ref, dq_ref, dkv_ref,
             dk_acc, dv_acc, sem):
        p = pl.program_id(0)
        i = pl.program_id(1)
        first = jnp.clip(first_ref[p * nq + i], 0, i)

        @pl.when(i == 0)
        def _():
            dk_acc[...] = jnp.zeros_like(dk_acc)
            dv_acc[...] = jnp.zeros_like(dv_acc)

        ones = ones_ref[...]
        ones_t = ones_t_ref[...]
        q_all = q_ref[...] * jnp.asarray(scale, BF16)
        do_bf = do_ref[...].astype(BF16)
        qs = [q_all[:, h * HEAD_DIM:(h + 1) * HEAD_DIM] for h in range(2)]
        dos = [do_bf[:, h * HEAD_DIM:(h + 1) * HEAD_DIM] for h in range(2)]
        rows = lax.broadcasted_iota(jnp.int32, (tq, tq), 0)
        cols = lax.broadcasted_iota(jnp.int32, (tq, tq), 1)
        causal = cols < rows
        block_of_lane = lax.broadcasted_iota(jnp.int32, (tq, nq), 1)

        def tile(j, carry, masked):
            start = pl.multiple_of(j * tq, tq)
            k_all = k_ref[pl.ds(start, tq), :]
            v_all = v_ref[pl.ds(start, tq), :]
            out, dks, dvs = [], [], []
            for h in range(2):
                gp, dq = carry[h]
                kh = k_all[:, h * HEAD_DIM:(h + 1) * HEAD_DIM]
                vh = v_all[:, h * HEAD_DIM:(h + 1) * HEAD_DIM]
                z = _nt(qs[h], kh)
                l = _log_one_minus_beta(z)
                if masked:
                    l = jnp.where(causal, l, 0.0)
                beta = jnp.exp(z + l)
                r_loc = _suffix_sum(l, ones)
                if masked:
                    logw = jnp.where(causal, z + r_loc, -jnp.inf)
                else:
                    c = jnp.sum(jnp.where(block_of_lane == j, cin_ref[h], 0.0), axis=1, keepdims=True)
                    logw = z + (c + r_loc)
                w = jnp.exp(logw)
                g = w * _nt(dos[h], vh)
                dvs.append(_tn(w.astype(BF16), dos[h]))
                g_pre = jnp.dot(g.astype(BF16), ones_t, preferred_element_type=F32)
                dz = g - beta * (gp + g_pre)
                if masked:
                    dz = jnp.where(causal, dz, 0.0)
                dz = dz.astype(BF16)
                dq = dq + jnp.dot(dz, kh, preferred_element_type=F32)
                dks.append(_tn(dz, qs[h]))
                out.append((gp + g_pre[:, tq - 1:tq], dq))
            dk_acc[pl.ds(start, tq), :] += jnp.concatenate(dks, axis=1)
            dv_acc[pl.ds(start, tq), :] += jnp.concatenate(dvs, axis=1)
            return tuple(out)

        init = tuple((jnp.zeros((tq, 1), F32), jnp.zeros((tq, HEAD_DIM), F32)) for _ in range(2))
        carry = lax.fori_loop(first, i, lambda j, cr: tile(j, cr, False), init)
        carry = tile(i, carry, True)
        dq_ref[...] = (jnp.concatenate([carry[0][1], carry[1][1]], axis=1) * scale).astype(dq_ref.dtype)

        @pl.when(i == nq - 1)
        def _():
            k_cols = pl.multiple_of(p * HEAD_PAIR, HEAD_PAIR)
            v_cols = pl.multiple_of((npair + p) * HEAD_PAIR, HEAD_PAIR)
            ck = pltpu.make_async_copy(dk_acc, dkv_ref.at[:, pl.ds(k_cols, HEAD_PAIR)], sem.at[0])
            cv = pltpu.make_async_copy(dv_acc, dkv_ref.at[:, pl.ds(v_cols, HEAD_PAIR)], sem.at[1])
            ck.start()
            cv.start()
            ck.wait()
            cv.wait()

    blk = pl.BlockSpec((tq, HEAD_PAIR), lambda p, i, fr: (i, p))
    tri = pl.BlockSpec((tq, tq), lambda p, i, fr: (0, 0))
    ones = _suffix_ones(tq)
    return pl.pallas_call(
        body, name="attention_bwd",
        grid_spec=pltpu.PrefetchScalarGridSpec(
            num_scalar_prefetch=1, grid=(npair, nq),
            in_specs=[blk,
                      pl.BlockSpec((t, HEAD_PAIR), lambda p, i, fr: (0, p)),
                      pl.BlockSpec((t, HEAD_PAIR), lambda p, i, fr: (0, npair + p)),
                      pl.BlockSpec((None, 2, tq, nq), lambda p, i, fr: (p, 0, i, 0)),
                      blk, tri, tri],
            out_specs=[blk, pl.BlockSpec(memory_space=pl.ANY)],
            scratch_shapes=[pltpu.VMEM((t, HEAD_PAIR), F32), pltpu.VMEM((t, HEAD_PAIR), F32),
                            pltpu.SemaphoreType.DMA((2,))]),
        out_shape=[jax.ShapeDtypeStruct((t, a), BF16), jax.ShapeDtypeStruct((t, a2), F32)],
        compiler_params=_params("arbitrary", "arbitrary"),
    )(first, qg, kv, kv, cin, do, ones, ones.T)


def _other_chips(x, y):
    return [(1 - x, y), (x, 1 - y), (1 - x, 1 - y)]


def _gather_blocks(wide, small):
    def body(w_ref, s_ref, wo_ref, so_ref, send_sems, recv_sems, local_sems):
        x, y, c = lax.axis_index("x"), lax.axis_index("y"), lax.axis_index("c")
        mine = 2 * x + y
        local = [pltpu.make_async_copy(w_ref, wo_ref.at[mine], local_sems.at[0]),
                 pltpu.make_async_copy(s_ref, so_ref.at[mine], local_sems.at[1])]
        for cp in local:
            cp.start()
        sends = []
        for k, (px, py) in enumerate(_other_chips(x, y)):
            for n, (src, dst) in enumerate(((w_ref, wo_ref), (s_ref, so_ref))):
                sends.append(pltpu.make_async_remote_copy(
                    src_ref=src, dst_ref=dst.at[mine], send_sem=send_sems.at[2 * k + n],
                    recv_sem=recv_sems.at[2 * k + n], device_id=(px, py, c), device_id_type=MESH))
        for cp in sends:
            cp.start()
        for k, (px, py) in enumerate(_other_chips(x, y)):
            for n, (src, dst) in enumerate(((w_ref, wo_ref), (s_ref, so_ref))):
                pltpu.make_async_remote_copy(
                    src_ref=src, dst_ref=dst.at[2 * px + py], send_sem=send_sems.at[2 * k + n],
                    recv_sem=recv_sems.at[2 * k + n], device_id=(px, py, c), device_id_type=MESH).wait_recv()
        for cp in sends:
            cp.wait_send()
        for cp in local:
            cp.wait()

    any_spec = pl.BlockSpec(memory_space=pl.ANY)
    return pl.pallas_call(
        body, name="gather_weights",
        in_specs=[any_spec, any_spec], out_specs=[any_spec, any_spec],
        out_shape=[jax.ShapeDtypeStruct((N_BLOCKS,) + wide.shape, wide.dtype),
                   jax.ShapeDtypeStruct((N_BLOCKS,) + small.shape, small.dtype)],
        scratch_shapes=[pltpu.SemaphoreType.DMA((6,)), pltpu.SemaphoreType.DMA((6,)), pltpu.SemaphoreType.DMA((2,))],
    )(wide, small)


def _scatter_blocks(grads):
    def body(g_ref, out_ref, send_sems, recv_sems, local_sem):
        x, y, c = lax.axis_index("x"), lax.axis_index("y"), lax.axis_index("c")
        mine = 2 * x + y
        local = pltpu.make_async_copy(g_ref.at[mine], out_ref.at[mine], local_sem)
        local.start()
        sends = []
        for k, (px, py) in enumerate(_other_chips(x, y)):
            sends.append(pltpu.make_async_remote_copy(
                src_ref=g_ref.at[2 * px + py], dst_ref=out_ref.at[mine], send_sem=send_sems.at[k],
                recv_sem=recv_sems.at[k], device_id=(px, py, c), device_id_type=MESH))
        for cp in sends:
            cp.start()
        for k, (px, py) in enumerate(_other_chips(x, y)):
            pltpu.make_async_remote_copy(
                src_ref=g_ref.at[mine], dst_ref=out_ref.at[2 * px + py], send_sem=send_sems.at[k],
                recv_sem=recv_sems.at[k], device_id=(px, py, c), device_id_type=MESH).wait_recv()
        for cp in sends:
            cp.wait_send()
        local.wait()

    any_spec = pl.BlockSpec(memory_space=pl.ANY)
    return pl.pallas_call(
        body, name="scatter_grads",
        in_specs=[any_spec], out_specs=any_spec,
        out_shape=jax.ShapeDtypeStruct(grads.shape, grads.dtype),
        scratch_shapes=[pltpu.SemaphoreType.DMA((3,)), pltpu.SemaphoreType.DMA((3,)), pltpu.SemaphoreType.DMA],
    )(grads)


def _swap_with_sibling(part):
    def body(p_ref, out_ref, send_sem, recv_sem):
        x, y, c = lax.axis_index("x"), lax.axis_index("y"), lax.axis_index("c")
        cp = pltpu.make_async_remote_copy(src_ref=p_ref, dst_ref=out_ref, send_sem=send_sem, recv_sem=recv_sem,
                                          device_id=(x, y, 1 - c), device_id_type=MESH)
        cp.start()
        cp.wait()

    any_spec = pl.BlockSpec(memory_space=pl.ANY)
    return pl.pallas_call(
        body, name="swap_sibling",
        in_specs=[any_spec], out_specs=any_spec,
        out_shape=jax.ShapeDtypeStruct(part.shape, part.dtype),
        scratch_shapes=[pltpu.SemaphoreType.DMA, pltpu.SemaphoreType.DMA],
    )(part)


def _sum_slots(parts, *, tr=ROW_TILE):
    _, r, d = parts.shape

    def body(p_ref, o_ref):
        o_ref[...] = ((p_ref[0] + p_ref[1]) + p_ref[2]) + p_ref[3]

    return pl.pallas_call(
        body, name="sum_slots", grid=(r // tr,),
        in_specs=[pl.BlockSpec((N_BLOCKS, tr, d), lambda i: (0, i, 0))],
        out_specs=pl.BlockSpec((tr, d), lambda i: (i, 0)),
        out_shape=jax.ShapeDtypeStruct((r, d), F32),
        compiler_params=_params("parallel"),
    )(parts)


def _adamw(part_mine, part_sibling, w, m, v, *, tr=ROW_TILE):
    r, d = w.shape
    m_scale = 1.0 / (1.0 - ADAM_B1 ** ADAM_STEP)
    v_scale = 1.0 / (1.0 - ADAM_B2 ** ADAM_STEP)

    def body(a_ref, b_ref, w_ref, m_ref, v_ref, g_ref, d_ref, mo_ref, vo_ref):
        g = a_ref[...] + b_ref[...]
        m_new = ADAM_B1 * m_ref[...] + (1.0 - ADAM_B1) * g
        v_new = ADAM_B2 * v_ref[...] + (1.0 - ADAM_B2) * (g * g)
        g_ref[...] = g
        mo_ref[...] = m_new
        vo_ref[...] = v_new
        d_ref[...] = -ADAM_LR * ((m_new * m_scale) / (jnp.sqrt(v_new * v_scale) + ADAM_EPS) + ADAM_WD * w_ref[...])

    blk = pl.BlockSpec((tr, d), lambda i: (i, 0))
    return pl.pallas_call(
        body, name="adamw", grid=(r // tr,),
        in_specs=[blk] * 5, out_specs=[blk] * 4,
        out_shape=[jax.ShapeDtypeStruct((r, d), F32)] * 4,
        compiler_params=_params("parallel"),
    )(part_mine, part_sibling, w, m, v)


def _pack_wide(w_in_a, w_out_a, w_kv, w_in_b, w_out_b):
    d = w_in_a.shape[-1]
    return jnp.concatenate([w_in_a[0], w_out_a[0], w_kv, w_in_b[0], w_out_b[0].reshape(-1, d)], axis=0)


def _small_rows(n_wide):
    return ROW_TILE - n_wide % ROW_TILE if n_wide % ROW_TILE else ROW_TILE


def _pack_small(conv_a, norm_a, norm_kv, norm_b, norm_f, rows):
    d = norm_kv.shape[-1]
    parts = [jnp.concatenate([conv_a[0].reshape(-1), norm_a[0]])[None, :], norm_kv.reshape(1, d),
             norm_b.reshape(1, d), norm_f.reshape(1, d), jnp.zeros((rows - 4, d), F32)]
    return jnp.concatenate(parts, axis=0)


def _unpack(packed, shapes):
    d = packed.shape[1]
    dq = d // N_BLOCKS
    att = shapes["w_out_b"][1]
    o = 0
    out = {}
    for name, rows in (("w_in_a", d), ("w_out_a", dq), ("w_kv", dq), ("w_in_b", dq), ("w_out_b", att // N_BLOCKS)):
        out[name] = packed[o:o + rows].reshape(shapes[name])
        o += rows
    out["conv_a"] = packed[o, :3 * dq].reshape(shapes["conv_a"])
    out["norm_a"] = packed[o, 3 * dq:].reshape(shapes["norm_a"])
    out["norm_kv"] = packed[o + 1].reshape(shapes["norm_kv"])
    out["norm_b"] = packed[o + 2].reshape(shapes["norm_b"])
    out["norm_f"] = packed[o + 3].reshape(shapes["norm_f"])
    return out


WEIGHTS = ["norm_a", "w_in_a", "conv_a", "w_out_a", "norm_kv", "w_kv", "norm_b", "w_in_b", "w_out_b", "norm_f"]


def kernel(x, norm_a, w_in_a, conv_a, w_out_a, norm_kv, w_kv, norm_b, w_in_b, w_out_b, norm_f, loss_target, m_norm_a, m_w_in_a, m_conv_a, m_w_out_a, m_norm_kv, m_w_kv, m_norm_b, m_w_in_b, m_w_out_b, m_norm_f, v_norm_a, v_w_in_a, v_conv_a, v_w_out_a, v_norm_kv, v_w_kv, v_norm_b, v_w_in_b, v_w_out_b, v_norm_f):
    d = x.shape[-1]
    dq = d // N_BLOCKS
    att = w_out_b.shape[1]
    xs = x[0]
    target = loss_target[0]
    shapes = dict(norm_a=norm_a.shape, w_in_a=w_in_a.shape, conv_a=conv_a.shape, w_out_a=w_out_a.shape,
                  norm_kv=norm_kv.shape, w_kv=w_kv.shape, norm_b=norm_b.shape, w_in_b=w_in_b.shape,
                  w_out_b=w_out_b.shape, norm_f=norm_f.shape)

    w_wide = _pack_wide(w_in_a, w_out_a, w_kv, w_in_b, w_out_b)
    n_wide = w_wide.shape[0]
    n_small = _small_rows(n_wide)
    w_pack = jnp.concatenate([w_wide, _pack_small(conv_a, norm_a, norm_kv, norm_b, norm_f, n_small)], axis=0)
    m_pack = jnp.concatenate([_pack_wide(m_w_in_a, m_w_out_a, m_w_kv, m_w_in_b, m_w_out_b),
                              _pack_small(m_conv_a, m_norm_a, m_norm_kv, m_norm_b, m_norm_f, n_small)], axis=0)
    v_pack = jnp.concatenate([_pack_wide(v_w_in_a, v_w_out_a, v_w_kv, v_w_in_b, v_w_out_b),
                              _pack_small(v_conv_a, v_norm_a, v_norm_kv, v_norm_b, v_norm_f, n_small)], axis=0)
    wide_all, small_all = _gather_blocks(w_wide.astype(BF16), w_pack[n_wide:n_wide + SMALL_ROWS])

    o1, o2, o3, o4 = d, d + dq, d + 2 * dq, d + 3 * dq
    wf_in_a = jnp.concatenate([wide_all[b, 0:o1] for b in range(N_BLOCKS)], axis=1)
    wf_out_a = wide_all[:, o1:o2].reshape(d, d)
    wf_kv = wide_all[:, o2:o3].reshape(d, 2 * att)
    wf_in_b = wide_all[:, o3:o4].reshape(d, 2 * att)
    wf_out_b = jnp.concatenate([wide_all[b, o4:].reshape(att, dq) for b in range(N_BLOCKS)], axis=1)
    conv_full = jnp.concatenate([small_all[b, 0, :3 * dq].reshape(3, dq) for b in range(N_BLOCKS)], axis=1)
    gain_a = jnp.concatenate([small_all[b, 0, 3 * dq:] for b in range(N_BLOCKS)])[None, :]
    gain_kv = norm_kv.reshape(1, d)
    gain_b = norm_b.reshape(1, d)
    gain_f = norm_f.reshape(1, d)

    u = _norm_matmul(xs, gain_a, wf_in_a, out_dtype=BF16, name="in_proj_a")
    mix = _mixer_fwd(u, conv_full)
    x1 = _matmul(mix, wf_out_a, res=xs, name="out_proj_a")
    kv = _norm_matmul(x1, gain_kv, wf_kv, out_dtype=BF16, name="kv_proj")
    qg = _norm_matmul(x1, gain_b, wf_in_b, out_dtype=BF16, name="in_proj_b")
    o, cin, first = _attention_fwd(qg, kv)
    gated = _gate_fwd(o, qg)
    x2 = _matmul(gated, wf_out_b, res=x1, name="out_proj_b")
    dx2, loss_cols, g_norm_f = _loss_head(x2, gain_f, target)
    loss = lax.psum(jnp.sum(loss_cols), ("x", "y", "c"))

    g_w_out_b = _matmul_tn(gated, dx2, name="grad_w_out_b")
    d_gated = _matmul(dx2, wf_out_b, trans_b=True, name="d_gated")
    do, dg = _gate_bwd(d_gated, o, qg)
    dq_att, dkv = _attention_bwd(qg, kv, cin, first, do)
    dqg = jnp.concatenate([dq_att, dg], axis=1)
    g_w_in_b = _matmul_tn(x1, dqg, gain=gain_b, name="grad_w_in_b")
    g_w_kv = _matmul_tn(x1, dkv, gain=gain_kv, name="grad_w_kv")
    dh_b = _matmul(dqg, wf_in_b, trans_b=True, name="d_h_b")
    dh_kv = _matmul(dkv, wf_kv, trans_b=True, name="d_h_kv")
    dx1, g_norm_b, g_norm_kv = _rms_bwd(x1, [dh_b, dh_kv], [gain_b, gain_kv], dx2, name="rms_bwd_b")
    g_w_out_a = _matmul_tn(mix, dx1, name="grad_w_out_a")
    dmix = _matmul(dx1, wf_out_a, trans_b=True, name="d_mix")
    du, g_conv = _mixer_bwd(u, dmix, conv_full)
    g_w_in_a = _matmul_tn(xs, du, gain=gain_a, blocked_out=True, tn=d, name="grad_w_in_a")
    dh_a = _matmul(du, wf_in_a, trans_b=True, name="d_h_a")
    dx, g_norm_a = _rms_bwd(xs, [dh_a], [gain_a], dx1, name="rms_bwd_a")

    small_rows = jnp.concatenate([
        jnp.concatenate([g_conv[:3].reshape(3, N_BLOCKS, dq).transpose(1, 0, 2).reshape(N_BLOCKS, 3 * dq),
                         g_norm_a.reshape(N_BLOCKS, dq)], axis=1)[:, None, :],
        jnp.broadcast_to(g_norm_kv[None], (N_BLOCKS, 1, d)),
        jnp.broadcast_to(g_norm_b[None], (N_BLOCKS, 1, d)),
        jnp.broadcast_to(g_norm_f[None], (N_BLOCKS, 1, d)),
        jnp.zeros((N_BLOCKS, n_small - 4, d), F32)], axis=1)
    g_pack = jnp.concatenate([
        g_w_in_a,
        g_w_out_a.reshape(N_BLOCKS, dq, d),
        g_w_kv.reshape(N_BLOCKS, dq, d),
        g_w_in_b.reshape(N_BLOCKS, dq, d),
        g_w_out_b.reshape(att, N_BLOCKS, dq).transpose(1, 0, 2).reshape(N_BLOCKS, att // N_BLOCKS, d),
        small_rows], axis=1)

    mine = _sum_slots(_scatter_blocks(g_pack))
    theirs = _swap_with_sibling(mine)
    g_tot, delta, m_new, v_new = _adamw(mine, theirs, w_pack, m_pack, v_pack)

    grads = _unpack(g_tot, shapes)
    deltas = _unpack(delta, shapes)
    new_m = _unpack(m_new, shapes)
    new_v = _unpack(v_new, shapes)
    return (loss, dx[None], *[grads[n] for n in WEIGHTS], *[deltas[n] for n in WEIGHTS],
            *[new_m[n] for n in WEIGHTS], *[new_v[n] for n in WEIGHTS])
```

```python
import functools

import jax
import jax.numpy as jnp
from jax import lax
from jax.experimental import pallas as pl
from jax.experimental.pallas import tpu as pltpu

F32 = jnp.float32
BF16 = jnp.bfloat16
MESH = pl.DeviceIdType.MESH

HEAD_DIM = 64
HEAD_PAIR = 2 * HEAD_DIM
RMS_EPS = 1e-6
ADAM_LR = 0.001
ADAM_B1 = 0.9
ADAM_B2 = 0.999
ADAM_EPS = 1e-08
ADAM_WD = 0.01
ADAM_STEP = 10
N_BLOCKS = 4
SMALL_ROWS = 8
ROW_TILE = 256
VMEM_LIMIT_V7X = 56 * 1024 * 1024


def _params(*sem):
    return pltpu.CompilerParams(dimension_semantics=sem if sem else None, vmem_limit_bytes=VMEM_LIMIT_V7X)


def _sigmoid(x):
    return 1.0 / (1.0 + jnp.exp(-x))


def _rms_normalize(x):
    rstd = lax.rsqrt(jnp.mean(x * x, axis=-1, keepdims=True) + RMS_EPS)
    return x * rstd, rstd


def _norm_matmul(x, gain, w, *, out_dtype, name, tm=1024, tn=1024):
    t, d = x.shape
    n = w.shape[1]
    tm, tn = min(tm, t), min(tn, n)

    def body(x_ref, g_ref, w_ref, o_ref):
        xn, _ = _rms_normalize(x_ref[...])
        h = (xn * g_ref[...]).astype(BF16)
        o_ref[...] = jnp.dot(h, w_ref[...], preferred_element_type=F32).astype(o_ref.dtype)

    return pl.pallas_call(
        body, name=name, grid=(t // tm, n // tn),
        in_specs=[pl.BlockSpec((tm, d), lambda i, j: (i, 0)),
                  pl.BlockSpec((1, d), lambda i, j: (0, 0)),
                  pl.BlockSpec((d, tn), lambda i, j: (0, j))],
        out_specs=pl.BlockSpec((tm, tn), lambda i, j: (i, j)),
        out_shape=jax.ShapeDtypeStruct((t, n), out_dtype),
        compiler_params=_params("parallel", "arbitrary"),
    )(x, gain, w)


def _matmul(a, w, *, name, trans_b=False, res=None, out_dtype=F32, tm=1024, tn=1024, tk=1024):
    t, k = a.shape
    n = w.shape[0] if trans_b else w.shape[1]
    tm, tn, tk = min(tm, t), min(tn, n), min(tk, k)
    nk = k // tk

    def body(*refs):
        if res is None:
            a_ref, w_ref, o_ref, acc = refs
        else:
            a_ref, w_ref, r_ref, o_ref, acc = refs
        kk = pl.program_id(2)

        @pl.when(kk == 0)
        def _():
            acc[...] = jnp.zeros_like(acc)

        av = a_ref[...].astype(BF16)
        if trans_b:
            acc[...] += lax.dot_general(av, w_ref[...], (((1,), (1,)), ((), ())), preferred_element_type=F32)
        else:
            acc[...] += jnp.dot(av, w_ref[...], preferred_element_type=F32)

        @pl.when(kk == nk - 1)
        def _():
            r = acc[...]
            if res is not None:
                r = r + r_ref[...]
            o_ref[...] = r.astype(o_ref.dtype)

    in_specs = [pl.BlockSpec((tm, tk), lambda i, j, kk: (i, kk)),
                pl.BlockSpec((tn, tk), lambda i, j, kk: (j, kk)) if trans_b
                else pl.BlockSpec((tk, tn), lambda i, j, kk: (kk, j))]
    args = [a, w]
    if res is not None:
        in_specs.append(pl.BlockSpec((tm, tn), lambda i, j, kk: (i, j)))
        args.append(res)
    return pl.pallas_call(
        body, name=name, grid=(t // tm, n // tn, nk),
        in_specs=in_specs,
        out_specs=pl.BlockSpec((tm, tn), lambda i, j, kk: (i, j)),
        out_shape=jax.ShapeDtypeStruct((t, n), out_dtype),
        scratch_shapes=[pltpu.VMEM((tm, tn), F32)],
        compiler_params=_params("parallel", "parallel", "arbitrary"),
    )(*args)


def _matmul_tn(a, b, *, name, gain=None, blocked_out=False, tt=1024, tn=512):
    t, k = a.shape
    n = b.shape[1]
    tt, tn = min(tt, t), min(tn, n)
    nt = t // tt

    def body(*refs):
        if gain is None:
            a_ref, b_ref, o_ref, acc = refs
            av = a_ref[...].astype(BF16)
        else:
            a_ref, g_ref, b_ref, o_ref, acc = refs
            xn, _ = _rms_normalize(a_ref[...])
            av = (xn * g_ref[...]).astype(BF16)

        @pl.when(pl.program_id(1) == 0)
        def _():
            acc[...] = jnp.zeros_like(acc)

        acc[...] += lax.dot_general(av, b_ref[...].astype(BF16), (((0,), (0,)), ((), ())),
                                    preferred_element_type=F32)

        @pl.when(pl.program_id(1) == nt - 1)
        def _():
            o_ref[...] = acc[...].astype(o_ref.dtype)

    in_specs = [pl.BlockSpec((tt, k), lambda j, s: (s, 0))]
    args = [a]
    if gain is not None:
        in_specs.append(pl.BlockSpec((1, k), lambda j, s: (0, 0)))
        args.append(gain)
    in_specs.append(pl.BlockSpec((tt, tn), lambda j, s: (s, j)))
    args.append(b)
    if blocked_out:
        out_spec = pl.BlockSpec((None, k, tn), lambda j, s: (j, 0, 0))
        out_shape = jax.ShapeDtypeStruct((n // tn, k, tn), BF16)
    else:
        out_spec = pl.BlockSpec((k, tn), lambda j, s: (0, j))
        out_shape = jax.ShapeDtypeStruct((k, n), BF16)
    return pl.pallas_call(
        body, name=name, grid=(n // tn, nt),
        in_specs=in_specs, out_specs=out_spec, out_shape=out_shape,
        scratch_shapes=[pltpu.VMEM((k, tn), F32)],
        compiler_params=_params("parallel", "arbitrary"),
    )(*args)


HALO = 16


def _shift_down(v, halo, k):
    rows = lax.broadcasted_iota(jnp.int32, v.shape, 0)
    out = pltpu.roll(v, k, 0)
    for r in range(k):
        out = jnp.where(rows == r, halo[HALO - k + r:HALO - k + r + 1, :], out)
    return out


def _shift_up(v, halo, k):
    n = v.shape[0]
    rows = lax.broadcasted_iota(jnp.int32, v.shape, 0)
    out = pltpu.roll(v, n - k, 0)
    for r in range(k):
        out = jnp.where(rows == n - k + r, halo[r:r + 1, :], out)
    return out


def _mixer_fwd(u, conv, *, tm=256):
    t, d4 = u.shape
    d = d4 // 4
    tm = min(tm, t)
    hb = tm // HALO

    def body(b_ref, c_ref, x_ref, g_ref, ch_ref, xh_ref, w_ref, m_ref):
        i = pl.program_id(0)
        y1 = c_ref[...].astype(F32) * x_ref[...].astype(F32)
        prev = ch_ref[...].astype(F32) * xh_ref[...].astype(F32)
        prev = jnp.where(i == 0, 0.0, prev)
        w = w_ref[...]
        yc = w[2:3, :] * y1 + w[1:2, :] * _shift_down(y1, prev, 1) + w[0:1, :] * _shift_down(y1, prev, 2)
        g = g_ref[...].astype(F32)
        m_ref[...] = (b_ref[...].astype(F32) * yc * (g * _sigmoid(g))).astype(m_ref.dtype)

    def col(c):
        return pl.BlockSpec((tm, d), lambda i: (i, c))

    def prev_rows(c):
        return pl.BlockSpec((HALO, d), lambda i: (jnp.maximum(i * hb - 1, 0), c))

    return pl.pallas_call(
        body, name="mixer_fwd", grid=(t // tm,),
        in_specs=[col(0), col(1), col(2), col(3), prev_rows(1), prev_rows(2), pl.BlockSpec((3, d), lambda i: (0, 0))],
        out_specs=pl.BlockSpec((tm, d), lambda i: (i, 0)),
        out_shape=jax.ShapeDtypeStruct((t, d), BF16),
        compiler_params=_params("parallel"),
    )(u, u, u, u, u, u, conv)


def _mixer_bwd(u, dm, conv, *, tm=256):
    t, d4 = u.shape
    d = d4 // 4
    tm = min(tm, t)
    hb = tm // HALO
    nb = t // tm

    def body(b_ref, c_ref, x_ref, g_ref, dm_ref, ch_ref, xh_ref, bn_ref, gn_ref, dmn_ref, w_ref, du_ref, gw_ref):
        i = pl.program_id(0)
        w = w_ref[...]
        b = b_ref[...].astype(F32)
        c = c_ref[...].astype(F32)
        xin = x_ref[...].astype(F32)
        g = g_ref[...].astype(F32)
        dm_v = dm_ref[...]
        y1 = c * xin
        prev = jnp.where(i == 0, 0.0, ch_ref[...].astype(F32) * xh_ref[...].astype(F32))
        y1m1 = _shift_down(y1, prev, 1)
        y1m2 = _shift_down(y1, prev, 2)
        yc = w[2:3, :] * y1 + w[1:2, :] * y1m1 + w[0:1, :] * y1m2
        sg = _sigmoid(g)
        s = g * sg
        ds = sg * (1.0 + g * (1.0 - sg))
        dyc = dm_v * b * s
        gn = gn_ref[...].astype(F32)
        nxt = dmn_ref[...] * bn_ref[...].astype(F32) * (gn * _sigmoid(gn))
        nxt = jnp.where(i == nb - 1, 0.0, nxt)
        dy1 = w[2:3, :] * dyc + w[1:2, :] * _shift_up(dyc, nxt, 1) + w[0:1, :] * _shift_up(dyc, nxt, 2)
        du_ref[:, 0:d] = (dm_v * yc * s).astype(du_ref.dtype)
        du_ref[:, d:2 * d] = (dy1 * xin).astype(du_ref.dtype)
        du_ref[:, 2 * d:3 * d] = (dy1 * c).astype(du_ref.dtype)
        du_ref[:, 3 * d:4 * d] = (dm_v * b * yc * ds).astype(du_ref.dtype)

        @pl.when(i == 0)
        def _():
            gw_ref[...] = jnp.zeros_like(gw_ref)

        gw_ref[0:1, :] += jnp.sum(dyc * y1m2, axis=0, keepdims=True)
        gw_ref[1:2, :] += jnp.sum(dyc * y1m1, axis=0, keepdims=True)
        gw_ref[2:3, :] += jnp.sum(dyc * y1, axis=0, keepdims=True)

    def col(c):
        return pl.BlockSpec((tm, d), lambda i: (i, c))

    def prev_rows(c):
        return pl.BlockSpec((HALO, d), lambda i: (jnp.maximum(i * hb - 1, 0), c))

    def next_rows(c):
        return pl.BlockSpec((HALO, d), lambda i: (jnp.minimum((i + 1) * hb, nb * hb - 1), c))

    return pl.pallas_call(
        body, name="mixer_bwd", grid=(nb,),
        in_specs=[col(0), col(1), col(2), col(3), pl.BlockSpec((tm, d), lambda i: (i, 0)),
                  prev_rows(1), prev_rows(2), next_rows(0), next_rows(3),
                  pl.BlockSpec((HALO, d), lambda i: (jnp.minimum((i + 1) * hb, nb * hb - 1), 0)),
                  pl.BlockSpec((3, d), lambda i: (0, 0))],
        out_specs=[pl.BlockSpec((tm, d4), lambda i: (i, 0)), pl.BlockSpec((SMALL_ROWS, d), lambda i: (0, 0))],
        out_shape=[jax.ShapeDtypeStruct((t, d4), BF16), jax.ShapeDtypeStruct((SMALL_ROWS, d), F32)],
        compiler_params=_params("arbitrary"),
    )(u, u, u, u, dm, u, u, u, u, dm, conv)


def _gate_fwd(o, qg, *, tm=512):
    t, a = o.shape
    tm = min(tm, t)

    def body(o_ref, g_ref, out_ref):
        g = g_ref[...].astype(F32)
        out_ref[...] = (o_ref[...] * (g * _sigmoid(g))).astype(out_ref.dtype)

    return pl.pallas_call(
        body, name="gate_fwd", grid=(t // tm,),
        in_specs=[pl.BlockSpec((tm, a), lambda i: (i, 0)), pl.BlockSpec((tm, a), lambda i: (i, 1))],
        out_specs=pl.BlockSpec((tm, a), lambda i: (i, 0)),
        out_shape=jax.ShapeDtypeStruct((t, a), BF16),
        compiler_params=_params("parallel"),
    )(o, qg)


def _gate_bwd(da, o, qg, *, tm=512):
    t, a = o.shape
    tm = min(tm, t)

    def body(da_ref, o_ref, g_ref, do_ref, dg_ref):
        g = g_ref[...].astype(F32)
        sg = _sigmoid(g)
        da_v = da_ref[...]
        do_ref[...] = (da_v * (g * sg)).astype(do_ref.dtype)
        dg_ref[...] = (da_v * o_ref[...] * (sg * (1.0 + g * (1.0 - sg)))).astype(dg_ref.dtype)

    blk = pl.BlockSpec((tm, a), lambda i: (i, 0))
    return pl.pallas_call(
        body, name="gate_bwd", grid=(t // tm,),
        in_specs=[blk, blk, pl.BlockSpec((tm, a), lambda i: (i, 1))],
        out_specs=[blk, blk],
        out_shape=[jax.ShapeDtypeStruct((t, a), F32), jax.ShapeDtypeStruct((t, a), BF16)],
        compiler_params=_params("parallel"),
    )(da, o, qg)


def _loss_head(x2, gain, target, *, tm=512):
    t, d = x2.shape
    tm = min(tm, t)

    def body(x_ref, g_ref, y_ref, dx_ref, loss_ref, gg_ref):
        @pl.when(pl.program_id(0) == 0)
        def _():
            loss_ref[...] = jnp.zeros_like(loss_ref)
            gg_ref[...] = jnp.zeros_like(gg_ref)

        xn, rstd = _rms_normalize(x_ref[...])
        gv = g_ref[...]
        err = xn * gv - y_ref[...]
        loss_ref[...] += jnp.sum(err * err, axis=0, keepdims=True) * (0.5 / d)
        dy = err * (1.0 / d)
        gg_ref[...] += jnp.sum(dy * xn, axis=0, keepdims=True)
        dxn = dy * gv
        dx_ref[...] = rstd * (dxn - xn * jnp.mean(dxn * xn, axis=-1, keepdims=True))

    blk = pl.BlockSpec((tm, d), lambda i: (i, 0))
    row = pl.BlockSpec((1, d), lambda i: (0, 0))
    return pl.pallas_call(
        body, name="loss_head", grid=(t // tm,),
        in_specs=[blk, row, blk], out_specs=[blk, row, row],
        out_shape=[jax.ShapeDtypeStruct((t, d), F32), jax.ShapeDtypeStruct((1, d), F32),
                   jax.ShapeDtypeStruct((1, d), F32)],
        compiler_params=_params("arbitrary"),
    )(x2, gain, target)


def _rms_bwd(x, dhs, gains, dres, *, name, tm=512):
    t, d = x.shape
    tm = min(tm, t)
    nk = len(dhs)

    def body(*refs):
        x_ref = refs[0]
        dh_refs = refs[1:1 + nk]
        g_refs = refs[1 + nk:1 + 2 * nk]
        dres_ref = refs[1 + 2 * nk]
        dx_ref = refs[2 + 2 * nk]
        gg_refs = refs[3 + 2 * nk:]

        @pl.when(pl.program_id(0) == 0)
        def _():
            for gg in gg_refs:
                gg[...] = jnp.zeros_like(gg)

        xn, rstd = _rms_normalize(x_ref[...])
        dxn = None
        for dh_ref, g_ref, gg in zip(dh_refs, g_refs, gg_refs):
            dh = dh_ref[...]
            gg[...] += jnp.sum(dh * xn, axis=0, keepdims=True)
            term = dh * g_ref[...]
            dxn = term if dxn is None else dxn + term
        dx_ref[...] = dres_ref[...] + rstd * (dxn - xn * jnp.mean(dxn * xn, axis=-1, keepdims=True))

    blk = pl.BlockSpec((tm, d), lambda i: (i, 0))
    row = pl.BlockSpec((1, d), lambda i: (0, 0))
    return pl.pallas_call(
        body, name=name, grid=(t // tm,),
        in_specs=[blk] + [blk] * nk + [row] * nk + [blk],
        out_specs=[blk] + [row] * nk,
        out_shape=[jax.ShapeDtypeStruct((t, d), F32)] + [jax.ShapeDtypeStruct((1, d), F32)] * nk,
        compiler_params=_params("arbitrary"),
    )(x, *dhs, *gains, dres)


def _suffix_ones(n):
    r = lax.broadcasted_iota(jnp.int32, (n, n), 0)
    c = lax.broadcasted_iota(jnp.int32, (n, n), 1)
    return (r >= c).astype(BF16)


def _suffix_sum(l, ones):
    return jnp.dot(l.astype(BF16), ones, preferred_element_type=F32)


def _log_one_minus_beta(z):
    return -(jnp.maximum(z, 0.0) + jnp.log(1.0 + jnp.exp(-jnp.abs(z))))


def _nt(a, b):
    return lax.dot_general(a, b, (((1,), (1,)), ((), ())), preferred_element_type=F32)


def _tn(a, b):
    return lax.dot_general(a, b, (((0,), (0,)), ((), ())), preferred_element_type=F32)


EXP_UNDERFLOW = 104.0
NORM_MARGIN = 1.01


def _key_norm_bounds(kv, *, tq):
    t, a2 = kv.shape
    a = a2 // 2
    heads = a // HEAD_DIM
    nq = t // tq

    def body(k_ref, o_ref):
        k = k_ref[...].astype(F32)
        k2 = k * k
        lane = lax.broadcasted_iota(jnp.int32, (1, heads), 1)
        out = jnp.zeros((1, heads), F32)
        for h in range(heads):
            n2 = jnp.sum(k2[:, h * HEAD_DIM:(h + 1) * HEAD_DIM], axis=1, keepdims=True)
            out = jnp.where(lane == h, jnp.max(n2, axis=0, keepdims=True), out)
        o_ref[...] = jnp.sqrt(out)

    norms = pl.pallas_call(
        body, name="key_norms", grid=(nq,),
        in_specs=[pl.BlockSpec((tq, a), lambda i: (i, 0))],
        out_specs=pl.BlockSpec((None, 1, heads), lambda i: (i, 0, 0)),
        out_shape=jax.ShapeDtypeStruct((nq, 1, heads), F32),
        compiler_params=_params("parallel"),
    )(kv)
    return lax.cummax(norms[:, 0, :], axis=0).T.reshape(-1)


def _attention_fwd(qg, kv, *, tq=256, n_sub=1):
    t, a2 = qg.shape
    a = a2 // 2
    npair = a // HEAD_PAIR
    tq = min(tq, t)
    scale = HEAD_DIM ** -0.5
    nq = t // tq
    ts = tq // n_sub

    def body(kmax_ref, q_ref, k_ref, v_ref, ones_ref, o_ref, cin_ref, first_ref):
        p = pl.program_id(0)
        i = pl.program_id(1)
        ones = ones_ref[...]
        q_all = q_ref[...] * jnp.asarray(scale, BF16)
        q2 = q_all.astype(F32) * q_all.astype(F32)
        chains = [(h, r) for h in range(2) for r in range(n_sub)]
        qs = [q_all[r * ts:(r + 1) * ts, h * HEAD_DIM:(h + 1) * HEAD_DIM] for h, r in chains]
        q_norm = [jnp.sqrt(jnp.sum(q2[r * ts:(r + 1) * ts, h * HEAD_DIM:(h + 1) * HEAD_DIM], axis=1, keepdims=True))
                  * NORM_MARGIN for h, r in chains]
        rows = lax.broadcasted_iota(jnp.int32, (ts, tq), 0)
        cols = lax.broadcasted_iota(jnp.int32, (ts, tq), 1)
        causal = [cols < rows + r * ts for r in range(n_sub)]
        block_of_lane = lax.broadcasted_iota(jnp.int32, (ts, nq), 1)
        cin_ref[...] = jnp.zeros_like(cin_ref)

        def any_weight_left_of(j, carry):
            jj = jnp.maximum(j - 1, 0)
            bound = None
            for n, (h, r) in enumerate(chains):
                top = jnp.max(carry[n][0] + q_norm[n] * kmax_ref[(2 * p + h) * nq + jj])
                bound = top if bound is None else jnp.maximum(bound, top)
            return bound > -EXP_UNDERFLOW

        def tile(j, carry, masked):
            start = pl.multiple_of(j * tq, tq)
            k_all = k_ref[pl.ds(start, tq), :]
            v_all = v_ref[pl.ds(start, tq), :]
            ks = [k_all[:, h * HEAD_DIM:(h + 1) * HEAD_DIM] for h in range(2)]
            vs = [v_all[:, h * HEAD_DIM:(h + 1) * HEAD_DIM] for h in range(2)]
            zs = [_nt(qs[n], ks[h]) for n, (h, r) in enumerate(chains)]
            ls = [_log_one_minus_beta(z) for z in zs]
            if masked:
                ls = [jnp.where(causal[r], l, 0.0) for l, (h, r) in zip(ls, chains)]
            r_locs = [_suffix_sum(l, ones) for l in ls]
            logws = [z + (carry[n][0] + r_loc) for n, (z, r_loc) in enumerate(zip(zs, r_locs))]
            if masked:
                logws = [jnp.where(causal[r], lw, -jnp.inf) for lw, (h, r) in zip(logws, chains)]
            ws = [jnp.exp(lw).astype(BF16) for lw in logws]
            out = []
            for n, (h, r) in enumerate(chains):
                acc = carry[n][1] + jnp.dot(ws[n], vs[h], preferred_element_type=F32)
                c = carry[n][0] + r_locs[n][:, 0:1]
                cin_ref[h, r * ts:(r + 1) * ts, :] = jnp.where(block_of_lane == j - 1, c,
                                                               cin_ref[h, r * ts:(r + 1) * ts, :])
                out.append((c, acc))
            return tuple(out)

        init = tuple((jnp.zeros((ts, 1), F32), jnp.zeros((ts, HEAD_DIM), F32)) for _ in chains)
        carry = tile(i, init, True)

        def visit(state):
            j, _, cr = state
            cr = tile(j - 1, cr, False)
            return j - 1, any_weight_left_of(j - 1, cr), cr

        first, _, carry = lax.while_loop(lambda s: (s[0] > 0) & s[1], visit,
                                         (i, any_weight_left_of(i, carry), carry))
        first_ref[p * nq + i] = first
        heads = [jnp.concatenate([carry[h * n_sub + r][1] for r in range(n_sub)], axis=0) for h in range(2)]
        o_ref[...] = jnp.concatenate(heads, axis=1)

    return pl.pallas_call(
        body, name="attention_fwd",
        grid_spec=pltpu.PrefetchScalarGridSpec(
            num_scalar_prefetch=1, grid=(npair, nq),
            in_specs=[pl.BlockSpec((tq, HEAD_PAIR), lambda p, i, km: (i, p)),
                      pl.BlockSpec((t, HEAD_PAIR), lambda p, i, km: (0, p)),
                      pl.BlockSpec((t, HEAD_PAIR), lambda p, i, km: (0, npair + p)),
                      pl.BlockSpec((tq, tq), lambda p, i, km: (0, 0))],
            out_specs=[pl.BlockSpec((tq, HEAD_PAIR), lambda p, i, km: (i, p)),
                       pl.BlockSpec((None, 2, tq, nq), lambda p, i, km: (p, 0, i, 0)),
                       pl.BlockSpec(memory_space=pltpu.SMEM)]),
        out_shape=[jax.ShapeDtypeStruct((t, a), F32), jax.ShapeDtypeStruct((npair, 2, t, nq), F32),
                   jax.ShapeDtypeStruct((npair * nq,), jnp.int32)],
        compiler_params=_params("arbitrary", "arbitrary"),
    )(_key_norm_bounds(kv, tq=tq), qg, kv, kv, _suffix_ones(tq))


def _attention_bwd(qg, kv, cin, first, do, *, tq=256):
    t, a2 = qg.shape
    a = a2 // 2
    npair = a // HEAD_PAIR
    tq = min(tq, t)
    nq = t // tq
    scale = HEAD_DIM ** -0.5

    def body(first_ref, q_ref, k_ref, v_ref, cin_ref, do_ref, ones_ref, ones_t_ref, dq_ref, dkv_ref,
             dk_acc, dv_acc, sem):
        p = pl.program_id(0)
        i = pl.program_id(1)
        first = jnp.clip(first_ref[p * nq + i], 0, i)

        @pl.when(i == 0)
        def _():
            dk_acc[...] = jnp.zeros_like(dk_acc)
            dv_acc[...] = jnp.zeros_like(dv_acc)

        ones = ones_ref[...]
        ones_t = ones_t_ref[...]
        q_all = q_ref[...] * jnp.asarray(scale, BF16)
        do_bf = do_ref[...].astype(BF16)
        qs = [q_all[:, h * HEAD_DIM:(h + 1) * HEAD_DIM] for h in range(2)]
        dos = [do_bf[:, h * HEAD_DIM:(h + 1) * HEAD_DIM] for h in range(2)]
        rows = lax.broadcasted_iota(jnp.int32, (tq, tq), 0)
        cols = lax.broadcasted_iota(jnp.int32, (tq, tq), 1)
        causal = cols < rows
        block_of_lane = lax.broadcasted_iota(jnp.int32, (tq, nq), 1)

        def tile(j, carry, masked):
            start = pl.multiple_of(j * tq, tq)
            k_all = k_ref[pl.ds(start, tq), :]
            v_all = v_ref[pl.ds(start, tq), :]
            hs = range(2)
            ks = [k_all[:, h * HEAD_DIM:(h + 1) * HEAD_DIM] for h in hs]
            vs = [v_all[:, h * HEAD_DIM:(h + 1) * HEAD_DIM] for h in hs]
            zs = [_nt(qs[h], ks[h]) for h in hs]
            das = [_nt(dos[h], vs[h]) for h in hs]
            ls = [_log_one_minus_beta(z) for z in zs]
            if masked:
                ls = [jnp.where(causal, l, 0.0) for l in ls]
            betas = [jnp.exp(z + l) for z, l in zip(zs, ls)]
            r_locs = [_suffix_sum(l, ones) for l in ls]
            if masked:
                logws = [jnp.where(causal, z + r_loc, -jnp.inf) for z, r_loc in zip(zs, r_locs)]
            else:
                cs = [jnp.sum(jnp.where(block_of_lane == j, cin_ref[h], 0.0), axis=1, keepdims=True) for h in hs]
                logws = [z + (c + r_loc) for z, c, r_loc in zip(zs, cs, r_locs)]
            ws = [jnp.exp(lw) for lw in logws]
            gs = [w * da for w, da in zip(ws, das)]
            g_pres = [jnp.dot(g.astype(BF16), ones_t, preferred_element_type=F32) for g in gs]
            dvs = [_tn(ws[h].astype(BF16), dos[h]) for h in hs]
            dzs = [gs[h] - betas[h] * (carry[h][0] + g_pres[h]) for h in hs]
            if masked:
                dzs = [jnp.where(causal, dz, 0.0) for dz in dzs]
            dzs = [dz.astype(BF16) for dz in dzs]
            dqs = [carry[h][1] + jnp.dot(dzs[h], ks[h], preferred_element_type=F32) for h in hs]
            dks = [_tn(dzs[h], qs[h]) for h in hs]
            dk_acc[pl.ds(start, tq), :] += jnp.concatenate(dks, axis=1)
            dv_acc[pl.ds(start, tq), :] += jnp.concatenate(dvs, axis=1)
            return tuple((carry[h][0] + g_pres[h][:, tq - 1:tq], dqs[h]) for h in hs)

        init = tuple((jnp.zeros((tq, 1), F32), jnp.zeros((tq, HEAD_DIM), F32)) for _ in range(2))
        carry = lax.fori_loop(first, i, lambda j, cr: tile(j, cr, False), init)
        carry = tile(i, carry, True)
        dq_ref[...] = (jnp.concatenate([carry[0][1], carry[1][1]], axis=1) * scale).astype(dq_ref.dtype)

        @pl.when(i == nq - 1)
        def _():
            k_cols = pl.multiple_of(p * HEAD_PAIR, HEAD_PAIR)
            v_cols = pl.multiple_of((npair + p) * HEAD_PAIR, HEAD_PAIR)
            ck = pltpu.make_async_copy(dk_acc, dkv_ref.at[:, pl.ds(k_cols, HEAD_PAIR)], sem.at[0])
            cv = pltpu.make_async_copy(dv_acc, dkv_ref.at[:, pl.ds(v_cols, HEAD_PAIR)], sem.at[1])
            ck.start()
            cv.start()
            ck.wait()
            cv.wait()

    blk = pl.BlockSpec((tq, HEAD_PAIR), lambda p, i, fr: (i, p))
    tri = pl.BlockSpec((tq, tq), lambda p, i, fr: (0, 0))
    ones = _suffix_ones(tq)
    return pl.pallas_call(
        body, name="attention_bwd",
        grid_spec=pltpu.PrefetchScalarGridSpec(
            num_scalar_prefetch=1, grid=(npair, nq),
            in_specs=[blk,
                      pl.BlockSpec((t, HEAD_PAIR), lambda p, i, fr: (0, p)),
                      pl.BlockSpec((t, HEAD_PAIR), lambda p, i, fr: (0, npair + p)),
                      pl.BlockSpec((None, 2, tq, nq), lambda p, i, fr: (p, 0, i, 0)),
                      blk, tri, tri],
            out_specs=[blk, pl.BlockSpec(memory_space=pl.ANY)],
            scratch_shapes=[pltpu.VMEM((t, HEAD_PAIR), F32), pltpu.VMEM((t, HEAD_PAIR), F32),
                            pltpu.SemaphoreType.DMA((2,))]),
        out_shape=[jax.ShapeDtypeStruct((t, a), BF16), jax.ShapeDtypeStruct((t, a2), F32)],
        compiler_params=_params("arbitrary", "arbitrary"),
    )(first, qg, kv, kv, cin, do, ones, ones.T)


def _other_chips(x, y):
    return [(1 - x, y), (x, 1 - y), (1 - x, 1 - y)]


def _gather_blocks(wide, small):
    def body(w_ref, s_ref, wo_ref, so_ref, send_sems, recv_sems, local_sems):
        x, y, c = lax.axis_index("x"), lax.axis_index("y"), lax.axis_index("c")
        mine = 2 * x + y
        local = [pltpu.make_async_copy(w_ref, wo_ref.at[mine], local_sems.at[0]),
                 pltpu.make_async_copy(s_ref, so_ref.at[mine], local_sems.at[1])]
        for cp in local:
            cp.start()
        sends = []
        for k, (px, py) in enumerate(_other_chips(x, y)):
            for n, (src, dst) in enumerate(((w_ref, wo_ref), (s_ref, so_ref))):
                sends.append(pltpu.make_async_remote_copy(
                    src_ref=src, dst_ref=dst.at[mine], send_sem=send_sems.at[2 * k + n],
                    recv_sem=recv_sems.at[2 * k + n], device_id=(px, py, c), device_id_type=MESH))
        for cp in sends:
            cp.start()
        for k, (px, py) in enumerate(_other_chips(x, y)):
            for n, (src, dst) in enumerate(((w_ref, wo_ref), (s_ref, so_ref))):
                pltpu.make_async_remote_copy(
                    src_ref=src, dst_ref=dst.at[2 * px + py], send_sem=send_sems.at[2 * k + n],
                    recv_sem=recv_sems.at[2 * k + n], device_id=(px, py, c), device_id_type=MESH).wait_recv()
        for cp in sends:
            cp.wait_send()
        for cp in local:
            cp.wait()

    any_spec = pl.BlockSpec(memory_space=pl.ANY)
    return pl.pallas_call(
        body, name="gather_weights",
        in_specs=[any_spec, any_spec], out_specs=[any_spec, any_spec],
        out_shape=[jax.ShapeDtypeStruct((N_BLOCKS,) + wide.shape, wide.dtype),
                   jax.ShapeDtypeStruct((N_BLOCKS,) + small.shape, small.dtype)],
        scratch_shapes=[pltpu.SemaphoreType.DMA((6,)), pltpu.SemaphoreType.DMA((6,)), pltpu.SemaphoreType.DMA((2,))],
    )(wide, small)


def _scatter_blocks(wide, small):
    n_in = len(wide) + 1
    offsets = [sum(w.shape[1] for w in wide[:k]) for k in range(len(wide))]
    n_wide = sum(w.shape[1] for w in wide)

    def body(*refs):
        in_refs = refs[:n_in]
        wide_out, small_out, send_sems, recv_sems, local_sems = refs[n_in:]
        x, y, c = lax.axis_index("x"), lax.axis_index("y"), lax.axis_index("c")
        mine = 2 * x + y

        def landing(k, slot):
            if k == n_in - 1:
                return small_out.at[slot]
            return wide_out.at[slot, pl.ds(offsets[k], wide[k].shape[1])]

        local = [pltpu.make_async_copy(in_refs[k].at[mine], landing(k, mine), local_sems.at[k]) for k in range(n_in)]
        for cp in local:
            cp.start()
        sends = []
        for p, (px, py) in enumerate(_other_chips(x, y)):
            for k in range(n_in):
                sends.append(pltpu.make_async_remote_copy(
                    src_ref=in_refs[k].at[2 * px + py], dst_ref=landing(k, mine), send_sem=send_sems.at[p * n_in + k],
                    recv_sem=recv_sems.at[p * n_in + k], device_id=(px, py, c), device_id_type=MESH))
        for cp in sends:
            cp.start()
        for p, (px, py) in enumerate(_other_chips(x, y)):
            for k in range(n_in):
                pltpu.make_async_remote_copy(
                    src_ref=in_refs[k].at[mine], dst_ref=landing(k, 2 * px + py), send_sem=send_sems.at[p * n_in + k],
                    recv_sem=recv_sems.at[p * n_in + k], device_id=(px, py, c), device_id_type=MESH).wait_recv()
        for cp in sends:
            cp.wait_send()
        for cp in local:
            cp.wait()

    any_spec = pl.BlockSpec(memory_space=pl.ANY)
    d = small.shape[-1]
    return pl.pallas_call(
        body, name="scatter_grads",
        in_specs=[any_spec] * n_in, out_specs=[any_spec, any_spec],
        out_shape=[jax.ShapeDtypeStruct((N_BLOCKS, n_wide, d), BF16), jax.ShapeDtypeStruct(small.shape, F32)],
        scratch_shapes=[pltpu.SemaphoreType.DMA((3 * n_in,)), pltpu.SemaphoreType.DMA((3 * n_in,)),
                        pltpu.SemaphoreType.DMA((n_in,))],
    )(*wide, small)


def _swap_with_sibling(part):
    def body(p_ref, out_ref, send_sem, recv_sem):
        x, y, c = lax.axis_index("x"), lax.axis_index("y"), lax.axis_index("c")
        cp = pltpu.make_async_remote_copy(src_ref=p_ref, dst_ref=out_ref, send_sem=send_sem, recv_sem=recv_sem,
                                          device_id=(x, y, 1 - c), device_id_type=MESH)
        cp.start()
        cp.wait()

    any_spec = pl.BlockSpec(memory_space=pl.ANY)
    return pl.pallas_call(
        body, name="swap_sibling",
        in_specs=[any_spec], out_specs=any_spec,
        out_shape=jax.ShapeDtypeStruct(part.shape, part.dtype),
        scratch_shapes=[pltpu.SemaphoreType.DMA, pltpu.SemaphoreType.DMA],
    )(part)


def _sum_slots(wide, small):
    _, n_wide, d = wide.shape
    tr = small.shape[1]
    nw = n_wide // tr

    def body(w_ref, s_ref, o_ref):
        i = pl.program_id(0)

        @pl.when(i < nw)
        def _():
            w = w_ref[...].astype(F32)
            o_ref[...] = ((w[0] + w[1]) + w[2]) + w[3]

        @pl.when(i == nw)
        def _():
            o_ref[...] = ((s_ref[0] + s_ref[1]) + s_ref[2]) + s_ref[3]

    return pl.pallas_call(
        body, name="sum_slots", grid=(nw + 1,),
        in_specs=[pl.BlockSpec((N_BLOCKS, tr, d), lambda i: (0, jnp.minimum(i, nw - 1), 0)),
                  pl.BlockSpec((N_BLOCKS, tr, d), lambda i: (0, 0, 0))],
        out_specs=pl.BlockSpec((tr, d), lambda i: (i, 0)),
        out_shape=jax.ShapeDtypeStruct((n_wide + tr, d), F32),
        compiler_params=_params("arbitrary"),
    )(wide, small)


def _adamw(part_mine, part_sibling, w, m, v, *, tr=ROW_TILE):
    r, d = w.shape
    m_scale = 1.0 / (1.0 - ADAM_B1 ** ADAM_STEP)
    v_scale = 1.0 / (1.0 - ADAM_B2 ** ADAM_STEP)

    def body(a_ref, b_ref, w_ref, m_ref, v_ref, g_ref, d_ref, mo_ref, vo_ref):
        g = a_ref[...] + b_ref[...]
        m_new = ADAM_B1 * m_ref[...] + (1.0 - ADAM_B1) * g
        v_new = ADAM_B2 * v_ref[...] + (1.0 - ADAM_B2) * (g * g)
        g_ref[...] = g
        mo_ref[...] = m_new
        vo_ref[...] = v_new
        d_ref[...] = -ADAM_LR * ((m_new * m_scale) / (jnp.sqrt(v_new * v_scale) + ADAM_EPS) + ADAM_WD * w_ref[...])

    blk = pl.BlockSpec((tr, d), lambda i: (i, 0))
    return pl.pallas_call(
        body, name="adamw", grid=(r // tr,),
        in_specs=[blk] * 5, out_specs=[blk] * 4,
        out_shape=[jax.ShapeDtypeStruct((r, d), F32)] * 4,
        compiler_params=_params("parallel"),
    )(part_mine, part_sibling, w, m, v)


def _pack_wide(w_in_a, w_out_a, w_kv, w_in_b, w_out_b):
    d = w_in_a.shape[-1]
    return jnp.concatenate([w_in_a[0], w_out_a[0], w_kv, w_in_b[0], w_out_b[0].reshape(-1, d)], axis=0)


def _small_rows(n_wide):
    return ROW_TILE - n_wide % ROW_TILE if n_wide % ROW_TILE else ROW_TILE


def _pack_small(conv_a, norm_a, norm_kv, norm_b, norm_f, rows):
    d = norm_kv.shape[-1]
    parts = [jnp.concatenate([conv_a[0].reshape(-1), norm_a[0]])[None, :], norm_kv.reshape(1, d),
             norm_b.reshape(1, d), norm_f.reshape(1, d), jnp.zeros((rows - 4, d), F32)]
    return jnp.concatenate(parts, axis=0)


def _unpack(packed, shapes):
    d = packed.shape[1]
    dq = d // N_BLOCKS
    att = shapes["w_out_b"][1]
    o = 0
    out = {}
    for name, rows in (("w_in_a", d), ("w_out_a", dq), ("w_kv", dq), ("w_in_b", dq), ("w_out_b", att // N_BLOCKS)):
        out[name] = packed[o:o + rows].reshape(shapes[name])
        o += rows
    out["conv_a"] = packed[o, :3 * dq].reshape(shapes["conv_a"])
    out["norm_a"] = packed[o, 3 * dq:].reshape(shapes["norm_a"])
    out["norm_kv"] = packed[o + 1].reshape(shapes["norm_kv"])
    out["norm_b"] = packed[o + 2].reshape(shapes["norm_b"])
    out["norm_f"] = packed[o + 3].reshape(shapes["norm_f"])
    return out


WEIGHTS = ["norm_a", "w_in_a", "conv_a", "w_out_a", "norm_kv", "w_kv", "norm_b", "w_in_b", "w_out_b", "norm_f"]


def kernel(x, norm_a, w_in_a, conv_a, w_out_a, norm_kv, w_kv, norm_b, w_in_b, w_out_b, norm_f, loss_target, m_norm_a, m_w_in_a, m_conv_a, m_w_out_a, m_norm_kv, m_w_kv, m_norm_b, m_w_in_b, m_w_out_b, m_norm_f, v_norm_a, v_w_in_a, v_conv_a, v_w_out_a, v_norm_kv, v_w_kv, v_norm_b, v_w_in_b, v_w_out_b, v_norm_f):
    d = x.shape[-1]
    dq = d // N_BLOCKS
    att = w_out_b.shape[1]
    xs = x[0]
    target = loss_target[0]
    shapes = dict(norm_a=norm_a.shape, w_in_a=w_in_a.shape, conv_a=conv_a.shape, w_out_a=w_out_a.shape,
                  norm_kv=norm_kv.shape, w_kv=w_kv.shape, norm_b=norm_b.shape, w_in_b=w_in_b.shape,
                  w_out_b=w_out_b.shape, norm_f=norm_f.shape)

    w_wide = _pack_wide(w_in_a, w_out_a, w_kv, w_in_b, w_out_b)
    n_wide = w_wide.shape[0]
    n_small = _small_rows(n_wide)
    w_pack = jnp.concatenate([w_wide, _pack_small(conv_a, norm_a, norm_kv, norm_b, norm_f, n_small)], axis=0)
    m_pack = jnp.concatenate([_pack_wide(m_w_in_a, m_w_out_a, m_w_kv, m_w_in_b, m_w_out_b),
                              _pack_small(m_conv_a, m_norm_a, m_norm_kv, m_norm_b, m_norm_f, n_small)], axis=0)
    v_pack = jnp.concatenate([_pack_wide(v_w_in_a, v_w_out_a, v_w_kv, v_w_in_b, v_w_out_b),
                              _pack_small(v_conv_a, v_norm_a, v_norm_kv, v_norm_b, v_norm_f, n_small)], axis=0)
    wide_all, small_all = _gather_blocks(w_wide.astype(BF16), w_pack[n_wide:n_wide + SMALL_ROWS])

    o1, o2, o3, o4 = d, d + dq, d + 2 * dq, d + 3 * dq
    wf_in_a = jnp.concatenate([wide_all[b, 0:o1] for b in range(N_BLOCKS)], axis=1)
    wf_out_a = wide_all[:, o1:o2].reshape(d, d)
    wf_kv = wide_all[:, o2:o3].reshape(d, 2 * att)
    wf_in_b = wide_all[:, o3:o4].reshape(d, 2 * att)
    wf_out_b = jnp.concatenate([wide_all[b, o4:].reshape(att, dq) for b in range(N_BLOCKS)], axis=1)
    conv_full = jnp.concatenate([small_all[b, 0, :3 * dq].reshape(3, dq) for b in range(N_BLOCKS)], axis=1)
    gain_a = jnp.concatenate([small_all[b, 0, 3 * dq:] for b in range(N_BLOCKS)])[None, :]
    gain_kv = norm_kv.reshape(1, d)
    gain_b = norm_b.reshape(1, d)
    gain_f = norm_f.reshape(1, d)

    u = _norm_matmul(xs, gain_a, wf_in_a, out_dtype=BF16, name="in_proj_a")
    mix = _mixer_fwd(u, conv_full)
    x1 = _matmul(mix, wf_out_a, res=xs, name="out_proj_a")
    kv = _norm_matmul(x1, gain_kv, wf_kv, out_dtype=BF16, name="kv_proj")
    qg = _norm_matmul(x1, gain_b, wf_in_b, out_dtype=BF16, name="in_proj_b")
    o, cin, first = _attention_fwd(qg, kv)
    gated = _gate_fwd(o, qg)
    x2 = _matmul(gated, wf_out_b, res=x1, name="out_proj_b")
    dx2, loss_cols, g_norm_f = _loss_head(x2, gain_f, target)
    loss = lax.psum(jnp.sum(loss_cols), ("x", "y", "c"))

    g_w_out_b = _matmul_tn(gated, dx2, blocked_out=True, tn=dq, name="grad_w_out_b")
    d_gated = _matmul(dx2, wf_out_b, trans_b=True, name="d_gated")
    do, dg = _gate_bwd(d_gated, o, qg)
    dq_att, dkv = _attention_bwd(qg, kv, cin, first, do)
    dqg = jnp.concatenate([dq_att, dg], axis=1)
    g_w_in_b = _matmul_tn(x1, dqg, gain=gain_b, name="grad_w_in_b")
    g_w_kv = _matmul_tn(x1, dkv, gain=gain_kv, name="grad_w_kv")
    dh_b = _matmul(dqg, wf_in_b, trans_b=True, name="d_h_b")
    dh_kv = _matmul(dkv, wf_kv, trans_b=True, name="d_h_kv")
    dx1, g_norm_b, g_norm_kv = _rms_bwd(x1, [dh_b, dh_kv], [gain_b, gain_kv], dx2, name="rms_bwd_b")
    g_w_out_a = _matmul_tn(mix, dx1, name="grad_w_out_a")
    dmix = _matmul(dx1, wf_out_a, trans_b=True, name="d_mix")
    du, g_conv = _mixer_bwd(u, dmix, conv_full)
    g_w_in_a = _matmul_tn(xs, du, gain=gain_a, blocked_out=True, tn=d, name="grad_w_in_a")
    dh_a = _matmul(du, wf_in_a, trans_b=True, name="d_h_a")
    dx, g_norm_a = _rms_bwd(xs, [dh_a], [gain_a], dx1, name="rms_bwd_a")

    small_rows = jnp.concatenate([
        jnp.concatenate([g_conv[:3].reshape(3, N_BLOCKS, dq).transpose(1, 0, 2).reshape(N_BLOCKS, 3 * dq),
                         g_norm_a.reshape(N_BLOCKS, dq)], axis=1)[:, None, :],
        jnp.broadcast_to(g_norm_kv[None], (N_BLOCKS, 1, d)),
        jnp.broadcast_to(g_norm_b[None], (N_BLOCKS, 1, d)),
        jnp.broadcast_to(g_norm_f[None], (N_BLOCKS, 1, d)),
        jnp.zeros((N_BLOCKS, n_small - 4, d), F32)], axis=1)
    wide_blocks = [g_w_in_a, g_w_out_a.reshape(N_BLOCKS, dq, d), g_w_kv.reshape(N_BLOCKS, dq, d),
                   g_w_in_b.reshape(N_BLOCKS, dq, d), g_w_out_b.reshape(N_BLOCKS, att // N_BLOCKS, d)]

    mine = _sum_slots(*_scatter_blocks(wide_blocks, small_rows))
    theirs = _swap_with_sibling(mine)
    g_tot, delta, m_new, v_new = _adamw(mine, theirs, w_pack, m_pack, v_pack)

    grads = _unpack(g_tot, shapes)
    deltas = _unpack(delta, shapes)
    new_m = _unpack(m_new, shapes)
    new_v = _unpack(v_new, shapes)
    return (loss, dx[None], *[grads[n] for n in WEIGHTS], *[deltas[n] for n in WEIGHTS],
            *[new_m[n] for n in WEIGHTS], *[new_v[n] for n in WEIGHTS])
```

```python
import functools

import jax
import jax.numpy as jnp
from jax import lax
from jax.experimental import pallas as pl
from jax.experimental.pallas import tpu as pltpu

F32 = jnp.float32
BF16 = jnp.bfloat16
MESH = pl.DeviceIdType.MESH

HEAD_DIM = 64
HEAD_PAIR = 2 * HEAD_DIM
RMS_EPS = 1e-6
ADAM_LR = 0.001
ADAM_B1 = 0.9
ADAM_B2 = 0.999
ADAM_EPS = 1e-08
ADAM_WD = 0.01
ADAM_STEP = 10
N_BLOCKS = 4
SMALL_ROWS = 8
ROW_TILE = 256
VMEM_LIMIT_V7X = 56 * 1024 * 1024


def _params(*sem):
    return pltpu.CompilerParams(dimension_semantics=sem if sem else None, vmem_limit_bytes=VMEM_LIMIT_V7X)


def _sigmoid(x):
    return 1.0 / (1.0 + jnp.exp(-x))


def _rms_normalize(x):
    rstd = lax.rsqrt(jnp.mean(x * x, axis=-1, keepdims=True) + RMS_EPS)
    return x * rstd, rstd


def _norm_matmul(x, gain, w, *, out_dtype, name, tm=1024, tn=1024):
    t, d = x.shape
    n = w.shape[1]
    tm, tn = min(tm, t), min(tn, n)

    def body(x_ref, g_ref, w_ref, o_ref):
        xn, _ = _rms_normalize(x_ref[...])
        h = (xn * g_ref[...]).astype(BF16)
        o_ref[...] = jnp.dot(h, w_ref[...], preferred_element_type=F32).astype(o_ref.dtype)

    return pl.pallas_call(
        body, name=name, grid=(t // tm, n // tn),
        in_specs=[pl.BlockSpec((tm, d), lambda i, j: (i, 0)),
                  pl.BlockSpec((1, d), lambda i, j: (0, 0)),
                  pl.BlockSpec((d, tn), lambda i, j: (0, j))],
        out_specs=pl.BlockSpec((tm, tn), lambda i, j: (i, j)),
        out_shape=jax.ShapeDtypeStruct((t, n), out_dtype),
        compiler_params=_params("parallel", "arbitrary"),
    )(x, gain, w)


def _matmul(a, w, *, name, trans_b=False, res=None, out_dtype=F32, tm=1024, tn=1024, tk=1024):
    t, k = a.shape
    n = w.shape[0] if trans_b else w.shape[1]
    tm, tn, tk = min(tm, t), min(tn, n), min(tk, k)
    nk = k // tk

    def body(*refs):
        if res is None:
            a_ref, w_ref, o_ref, acc = refs
        else:
            a_ref, w_ref, r_ref, o_ref, acc = refs
        kk = pl.program_id(2)

        @pl.when(kk == 0)
        def _():
            acc[...] = jnp.zeros_like(acc)

        av = a_ref[...].astype(BF16)
        if trans_b:
            acc[...] += lax.dot_general(av, w_ref[...], (((1,), (1,)), ((), ())), preferred_element_type=F32)
        else:
            acc[...] += jnp.dot(av, w_ref[...], preferred_element_type=F32)

        @pl.when(kk == nk - 1)
        def _():
            r = acc[...]
            if res is not None:
                r = r + r_ref[...]
            o_ref[...] = r.astype(o_ref.dtype)

    in_specs = [pl.BlockSpec((tm, tk), lambda i, j, kk: (i, kk)),
                pl.BlockSpec((tn, tk), lambda i, j, kk: (j, kk)) if trans_b
                else pl.BlockSpec((tk, tn), lambda i, j, kk: (kk, j))]
    args = [a, w]
    if res is not None:
        in_specs.append(pl.BlockSpec((tm, tn), lambda i, j, kk: (i, j)))
        args.append(res)
    return pl.pallas_call(
        body, name=name, grid=(t // tm, n // tn, nk),
        in_specs=in_specs,
        out_specs=pl.BlockSpec((tm, tn), lambda i, j, kk: (i, j)),
        out_shape=jax.ShapeDtypeStruct((t, n), out_dtype),
        scratch_shapes=[pltpu.VMEM((tm, tn), F32)],
        compiler_params=_params("parallel", "parallel", "arbitrary"),
    )(*args)


def _matmul_tn(a, b, *, name, gain=None, blocked_out=False, tt=1024, tn=512):
    t, k = a.shape
    n = b.shape[1]
    tt, tn = min(tt, t), min(tn, n)
    nt = t // tt

    def body(*refs):
        if gain is None:
            a_ref, b_ref, o_ref, acc = refs
            av = a_ref[...].astype(BF16)
        else:
            a_ref, g_ref, b_ref, o_ref, acc = refs
            xn, _ = _rms_normalize(a_ref[...])
            av = (xn * g_ref[...]).astype(BF16)

        @pl.when(pl.program_id(1) == 0)
        def _():
            acc[...] = jnp.zeros_like(acc)

        acc[...] += lax.dot_general(av, b_ref[...].astype(BF16), (((0,), (0,)), ((), ())),
                                    preferred_element_type=F32)

        @pl.when(pl.program_id(1) == nt - 1)
        def _():
            o_ref[...] = acc[...].astype(o_ref.dtype)

    in_specs = [pl.BlockSpec((tt, k), lambda j, s: (s, 0))]
    args = [a]
    if gain is not None:
        in_specs.append(pl.BlockSpec((1, k), lambda j, s: (0, 0)))
        args.append(gain)
    in_specs.append(pl.BlockSpec((tt, tn), lambda j, s: (s, j)))
    args.append(b)
    if blocked_out:
        out_spec = pl.BlockSpec((None, k, tn), lambda j, s: (j, 0, 0))
        out_shape = jax.ShapeDtypeStruct((n // tn, k, tn), BF16)
    else:
        out_spec = pl.BlockSpec((k, tn), lambda j, s: (0, j))
        out_shape = jax.ShapeDtypeStruct((k, n), BF16)
    return pl.pallas_call(
        body, name=name, grid=(n // tn, nt),
        in_specs=in_specs, out_specs=out_spec, out_shape=out_shape,
        scratch_shapes=[pltpu.VMEM((k, tn), F32)],
        compiler_params=_params("parallel", "arbitrary"),
    )(*args)


HALO = 16


def _shift_down(v, halo, k):
    rows = lax.broadcasted_iota(jnp.int32, v.shape, 0)
    out = pltpu.roll(v, k, 0)
    for r in range(k):
        out = jnp.where(rows == r, halo[HALO - k + r:HALO - k + r + 1, :], out)
    return out


def _shift_up(v, halo, k):
    n = v.shape[0]
    rows = lax.broadcasted_iota(jnp.int32, v.shape, 0)
    out = pltpu.roll(v, n - k, 0)
    for r in range(k):
        out = jnp.where(rows == n - k + r, halo[r:r + 1, :], out)
    return out


def _mixer_fwd(u, conv, *, tm=256):
    t, d4 = u.shape
    d = d4 // 4
    tm = min(tm, t)
    hb = tm // HALO

    def body(b_ref, c_ref, x_ref, g_ref, ch_ref, xh_ref, w_ref, m_ref):
        i = pl.program_id(0)
        y1 = c_ref[...].astype(F32) * x_ref[...].astype(F32)
        prev = ch_ref[...].astype(F32) * xh_ref[...].astype(F32)
        prev = jnp.where(i == 0, 0.0, prev)
        w = w_ref[...]
        yc = w[2:3, :] * y1 + w[1:2, :] * _shift_down(y1, prev, 1) + w[0:1, :] * _shift_down(y1, prev, 2)
        g = g_ref[...].astype(F32)
        m_ref[...] = (b_ref[...].astype(F32) * yc * (g * _sigmoid(g))).astype(m_ref.dtype)

    def col(c):
        return pl.BlockSpec((tm, d), lambda i: (i, c))

    def prev_rows(c):
        return pl.BlockSpec((HALO, d), lambda i: (jnp.maximum(i * hb - 1, 0), c))

    return pl.pallas_call(
        body, name="mixer_fwd", grid=(t // tm,),
        in_specs=[col(0), col(1), col(2), col(3), prev_rows(1), prev_rows(2), pl.BlockSpec((3, d), lambda i: (0, 0))],
        out_specs=pl.BlockSpec((tm, d), lambda i: (i, 0)),
        out_shape=jax.ShapeDtypeStruct((t, d), BF16),
        compiler_params=_params("parallel"),
    )(u, u, u, u, u, u, conv)


def _mixer_bwd(u, dm, conv, *, tm=256):
    t, d4 = u.shape
    d = d4 // 4
    tm = min(tm, t)
    hb = tm // HALO
    nb = t // tm

    def body(b_ref, c_ref, x_ref, g_ref, dm_ref, ch_ref, xh_ref, bn_ref, gn_ref, dmn_ref, w_ref, du_ref, gw_ref):
        i = pl.program_id(0)
        w = w_ref[...]
        b = b_ref[...].astype(F32)
        c = c_ref[...].astype(F32)
        xin = x_ref[...].astype(F32)
        g = g_ref[...].astype(F32)
        dm_v = dm_ref[...]
        y1 = c * xin
        prev = jnp.where(i == 0, 0.0, ch_ref[...].astype(F32) * xh_ref[...].astype(F32))
        y1m1 = _shift_down(y1, prev, 1)
        y1m2 = _shift_down(y1, prev, 2)
        yc = w[2:3, :] * y1 + w[1:2, :] * y1m1 + w[0:1, :] * y1m2
        sg = _sigmoid(g)
        s = g * sg
        ds = sg * (1.0 + g * (1.0 - sg))
        dyc = dm_v * b * s
        gn = gn_ref[...].astype(F32)
        nxt = dmn_ref[...] * bn_ref[...].astype(F32) * (gn * _sigmoid(gn))
        nxt = jnp.where(i == nb - 1, 0.0, nxt)
        dy1 = w[2:3, :] * dyc + w[1:2, :] * _shift_up(dyc, nxt, 1) + w[0:1, :] * _shift_up(dyc, nxt, 2)
        du_ref[:, 0:d] = (dm_v * yc * s).astype(du_ref.dtype)
        du_ref[:, d:2 * d] = (dy1 * xin).astype(du_ref.dtype)
        du_ref[:, 2 * d:3 * d] = (dy1 * c).astype(du_ref.dtype)
        du_ref[:, 3 * d:4 * d] = (dm_v * b * yc * ds).astype(du_ref.dtype)

        @pl.when(i == 0)
        def _():
            gw_ref[...] = jnp.zeros_like(gw_ref)

        gw_ref[0:1, :] += jnp.sum(dyc * y1m2, axis=0, keepdims=True)
        gw_ref[1:2, :] += jnp.sum(dyc * y1m1, axis=0, keepdims=True)
        gw_ref[2:3, :] += jnp.sum(dyc * y1, axis=0, keepdims=True)

    def col(c):
        return pl.BlockSpec((tm, d), lambda i: (i, c))

    def prev_rows(c):
        return pl.BlockSpec((HALO, d), lambda i: (jnp.maximum(i * hb - 1, 0), c))

    def next_rows(c):
        return pl.BlockSpec((HALO, d), lambda i: (jnp.minimum((i + 1) * hb, nb * hb - 1), c))

    return pl.pallas_call(
        body, name="mixer_bwd", grid=(nb,),
        in_specs=[col(0), col(1), col(2), col(3), pl.BlockSpec((tm, d), lambda i: (i, 0)),
                  prev_rows(1), prev_rows(2), next_rows(0), next_rows(3),
                  pl.BlockSpec((HALO, d), lambda i: (jnp.minimum((i + 1) * hb, nb * hb - 1), 0)),
                  pl.BlockSpec((3, d), lambda i: (0, 0))],
        out_specs=[pl.BlockSpec((tm, d4), lambda i: (i, 0)), pl.BlockSpec((SMALL_ROWS, d), lambda i: (0, 0))],
        out_shape=[jax.ShapeDtypeStruct((t, d4), BF16), jax.ShapeDtypeStruct((SMALL_ROWS, d), F32)],
        compiler_params=_params("arbitrary"),
    )(u, u, u, u, dm, u, u, u, u, dm, conv)


def _out_proj_b_loss(o, qg, w_out, x1, gain, target, *, tm=512):
    t, a = o.shape
    d = x1.shape[1]
    tm = min(tm, t)

    def body(o_ref, g_ref, w_ref, x_ref, gain_ref, y_ref, gated_ref, dx_ref, loss_ref, gg_ref):
        @pl.when(pl.program_id(0) == 0)
        def _():
            loss_ref[...] = jnp.zeros_like(loss_ref)
            gg_ref[...] = jnp.zeros_like(gg_ref)

        g = g_ref[...].astype(F32)
        gated = (o_ref[...] * (g * _sigmoid(g))).astype(BF16)
        gated_ref[...] = gated
        x2 = x_ref[...] + jnp.dot(gated, w_ref[...], preferred_element_type=F32)
        xn, rstd = _rms_normalize(x2)
        gv = gain_ref[...]
        err = xn * gv - y_ref[...]
        loss_ref[...] += jnp.sum(err * err, axis=0, keepdims=True) * (0.5 / d)
        dy = err * (1.0 / d)
        gg_ref[...] += jnp.sum(dy * xn, axis=0, keepdims=True)
        dxn = dy * gv
        dx_ref[...] = rstd * (dxn - xn * jnp.mean(dxn * xn, axis=-1, keepdims=True))

    blk = pl.BlockSpec((tm, d), lambda i: (i, 0))
    blk_a = pl.BlockSpec((tm, a), lambda i: (i, 0))
    row = pl.BlockSpec((1, d), lambda i: (0, 0))
    return pl.pallas_call(
        body, name="out_proj_b_loss", grid=(t // tm,),
        in_specs=[blk_a, pl.BlockSpec((tm, a), lambda i: (i, 1)), pl.BlockSpec(w_out.shape, lambda i: (0, 0)),
                  blk, row, blk],
        out_specs=[blk_a, blk, row, row],
        out_shape=[jax.ShapeDtypeStruct((t, a), BF16), jax.ShapeDtypeStruct((t, d), F32),
                   jax.ShapeDtypeStruct((1, d), F32), jax.ShapeDtypeStruct((1, d), F32)],
        compiler_params=_params("arbitrary"),
    )(o, qg, w_out, x1, gain, target)


def _out_proj_b_bwd(dx2, w_out, o, qg, *, tm=512):
    t, a = o.shape
    d = dx2.shape[1]
    tm = min(tm, t)

    def body(dx_ref, w_ref, o_ref, g_ref, do_ref, dg_ref):
        da = _nt(dx_ref[...].astype(BF16), w_ref[...])
        g = g_ref[...].astype(F32)
        sg = _sigmoid(g)
        do_ref[...] = da * (g * sg)
        dg_ref[...] = (da * o_ref[...] * (sg * (1.0 + g * (1.0 - sg)))).astype(dg_ref.dtype)

    blk_a = pl.BlockSpec((tm, a), lambda i: (i, 0))
    return pl.pallas_call(
        body, name="out_proj_b_bwd", grid=(t // tm,),
        in_specs=[pl.BlockSpec((tm, d), lambda i: (i, 0)), pl.BlockSpec(w_out.shape, lambda i: (0, 0)), blk_a,
                  pl.BlockSpec((tm, a), lambda i: (i, 1))],
        out_specs=[blk_a, blk_a],
        out_shape=[jax.ShapeDtypeStruct((t, a), F32), jax.ShapeDtypeStruct((t, a), BF16)],
        compiler_params=_params("parallel"),
    )(dx2, w_out, o, qg)


def _proj_rms_bwd(x, cots, ws, gains, dres, *, name, tm=512):
    t, d = x.shape
    tm = min(tm, t)
    nk = len(cots)

    def body(*refs):
        x_ref = refs[0]
        cot_refs = refs[1:1 + nk]
        w_refs = refs[1 + nk:1 + 2 * nk]
        g_refs = refs[1 + 2 * nk:1 + 3 * nk]
        dres_ref = refs[1 + 3 * nk]
        dx_ref = refs[2 + 3 * nk]
        gg_refs = refs[3 + 3 * nk:]

        @pl.when(pl.program_id(0) == 0)
        def _():
            for gg in gg_refs:
                gg[...] = jnp.zeros_like(gg)

        xn, rstd = _rms_normalize(x_ref[...])
        dxn = None
        for cot_ref, w_ref, g_ref, gg in zip(cot_refs, w_refs, g_refs, gg_refs):
            dh = _nt(cot_ref[...].astype(BF16), w_ref[...])
            gg[...] += jnp.sum(dh * xn, axis=0, keepdims=True)
            term = dh * g_ref[...]
            dxn = term if dxn is None else dxn + term
        dx_ref[...] = dres_ref[...] + rstd * (dxn - xn * jnp.mean(dxn * xn, axis=-1, keepdims=True))

    blk = pl.BlockSpec((tm, d), lambda i: (i, 0))
    row = pl.BlockSpec((1, d), lambda i: (0, 0))
    return pl.pallas_call(
        body, name=name, grid=(t // tm,),
        in_specs=([blk] + [pl.BlockSpec((tm, c.shape[1]), lambda i: (i, 0)) for c in cots]
                  + [pl.BlockSpec(w.shape, lambda i: (0, 0)) for w in ws] + [row] * nk + [blk]),
        out_specs=[blk] + [row] * nk,
        out_shape=[jax.ShapeDtypeStruct((t, d), F32)] + [jax.ShapeDtypeStruct((1, d), F32)] * nk,
        compiler_params=_params("arbitrary"),
    )(x, *cots, *ws, *gains, dres)


def _suffix_ones(n):
    r = lax.broadcasted_iota(jnp.int32, (n, n), 0)
    c = lax.broadcasted_iota(jnp.int32, (n, n), 1)
    return (r >= c).astype(BF16)


def _suffix_sum(l, ones):
    return jnp.dot(l.astype(BF16), ones, preferred_element_type=F32)


def _log_one_minus_beta(z):
    return -(jnp.maximum(z, 0.0) + jnp.log(1.0 + jnp.exp(-jnp.abs(z))))


def _nt(a, b):
    return lax.dot_general(a, b, (((1,), (1,)), ((), ())), preferred_element_type=F32)


def _tn(a, b):
    return lax.dot_general(a, b, (((0,), (0,)), ((), ())), preferred_element_type=F32)


EXP_UNDERFLOW = 104.0
NORM_MARGIN = 1.01


def _key_norm_bounds(kv, *, tq):
    t, a2 = kv.shape
    a = a2 // 2
    heads = a // HEAD_DIM
    nq = t // tq

    per_step = min(4, nq)
    lanes = 128

    def body(k_ref, sel_ref, o_ref):
        k = k_ref[...].astype(F32)
        n2 = jnp.dot((k * k).astype(BF16), sel_ref[...], preferred_element_type=F32)
        o_ref[...] = jnp.sqrt(jnp.max(n2.reshape(per_step, tq, lanes), axis=1))

    head_of_col = lax.broadcasted_iota(jnp.int32, (a, lanes), 0) // HEAD_DIM
    sel = (head_of_col == lax.broadcasted_iota(jnp.int32, (a, lanes), 1)).astype(BF16)
    norms = pl.pallas_call(
        body, name="key_norms", grid=(nq // per_step,),
        in_specs=[pl.BlockSpec((per_step * tq, a), lambda i: (i, 0)), pl.BlockSpec((a, lanes), lambda i: (0, 0))],
        out_specs=pl.BlockSpec((None, per_step, lanes), lambda i: (i, 0, 0)),
        out_shape=jax.ShapeDtypeStruct((nq // per_step, per_step, lanes), F32),
        compiler_params=_params("parallel"),
    )(kv, sel)
    return lax.cummax(norms.reshape(nq, lanes)[:, :heads], axis=0).T.reshape(-1)


def _attention_fwd(qg, kv, *, tq=256, n_sub=1):
    t, a2 = qg.shape
    a = a2 // 2
    npair = a // HEAD_PAIR
    tq = min(tq, t)
    scale = HEAD_DIM ** -0.5
    nq = t // tq
    ts = tq // n_sub

    def body(kmax_ref, q_ref, k_ref, v_ref, ones_ref, o_ref, cin_ref, first_ref):
        p = pl.program_id(0)
        i = pl.program_id(1)
        ones = ones_ref[...]
        q_all = q_ref[...] * jnp.asarray(scale, BF16)
        q2 = q_all.astype(F32) * q_all.astype(F32)
        chains = [(h, r) for h in range(2) for r in range(n_sub)]
        qs = [q_all[r * ts:(r + 1) * ts, h * HEAD_DIM:(h + 1) * HEAD_DIM] for h, r in chains]
        q_norm = [jnp.sqrt(jnp.sum(q2[r * ts:(r + 1) * ts, h * HEAD_DIM:(h + 1) * HEAD_DIM], axis=1, keepdims=True))
                  * NORM_MARGIN for h, r in chains]
        rows = lax.broadcasted_iota(jnp.int32, (ts, tq), 0)
        cols = lax.broadcasted_iota(jnp.int32, (ts, tq), 1)
        causal = [cols < rows + r * ts for r in range(n_sub)]
        block_of_lane = lax.broadcasted_iota(jnp.int32, (ts, nq), 1)
        cin_ref[...] = jnp.zeros_like(cin_ref)

        def any_weight_left_of(j, carry):
            jj = jnp.maximum(j - 1, 0)
            bound = None
            for n, (h, r) in enumerate(chains):
                top = jnp.max(carry[n][0] + q_norm[n] * kmax_ref[(2 * p + h) * nq + jj])
                bound = top if bound is None else jnp.maximum(bound, top)
            return bound > -EXP_UNDERFLOW

        def tile(j, carry, masked):
            start = pl.multiple_of(j * tq, tq)
            k_all = k_ref[pl.ds(start, tq), :]
            v_all = v_ref[pl.ds(start, tq), :]
            ks = [k_all[:, h * HEAD_DIM:(h + 1) * HEAD_DIM] for h in range(2)]
            vs = [v_all[:, h * HEAD_DIM:(h + 1) * HEAD_DIM] for h in range(2)]
            zs = [_nt(qs[n], ks[h]) for n, (h, r) in enumerate(chains)]
            ls = [_log_one_minus_beta(z) for z in zs]
            if masked:
                ls = [jnp.where(causal[r], l, 0.0) for l, (h, r) in zip(ls, chains)]
            r_locs = [_suffix_sum(l, ones) for l in ls]
            logws = [z + (carry[n][0] + r_loc) for n, (z, r_loc) in enumerate(zip(zs, r_locs))]
            if masked:
                logws = [jnp.where(causal[r], lw, -jnp.inf) for lw, (h, r) in zip(logws, chains)]
            ws = [jnp.exp(lw).astype(BF16) for lw in logws]
            out = []
            for n, (h, r) in enumerate(chains):
                acc = carry[n][1] + jnp.dot(ws[n], vs[h], preferred_element_type=F32)
                c = carry[n][0] + r_locs[n][:, 0:1]
                cin_ref[h, r * ts:(r + 1) * ts, :] = jnp.where(block_of_lane == j - 1, c,
                                                               cin_ref[h, r * ts:(r + 1) * ts, :])
                out.append((c, acc))
            return tuple(out)

        init = tuple((jnp.zeros((ts, 1), F32), jnp.zeros((ts, HEAD_DIM), F32)) for _ in chains)
        carry = tile(i, init, True)

        def visit(state):
            j, _, cr = state
            cr = tile(j - 1, cr, False)
            return j - 1, any_weight_left_of(j - 1, cr), cr

        first, _, carry = lax.while_loop(lambda s: (s[0] > 0) & s[1], visit,
                                         (i, any_weight_left_of(i, carry), carry))
        first_ref[p * nq + i] = first
        heads = [jnp.concatenate([carry[h * n_sub + r][1] for r in range(n_sub)], axis=0) for h in range(2)]
        o_ref[...] = jnp.concatenate(heads, axis=1)

    return pl.pallas_call(
        body, name="attention_fwd",
        grid_spec=pltpu.PrefetchScalarGridSpec(
            num_scalar_prefetch=1, grid=(npair, nq),
            in_specs=[pl.BlockSpec((tq, HEAD_PAIR), lambda p, i, km: (i, p)),
                      pl.BlockSpec((t, HEAD_PAIR), lambda p, i, km: (0, p)),
                      pl.BlockSpec((t, HEAD_PAIR), lambda p, i, km: (0, npair + p)),
                      pl.BlockSpec((tq, tq), lambda p, i, km: (0, 0))],
            out_specs=[pl.BlockSpec((tq, HEAD_PAIR), lambda p, i, km: (i, p)),
                       pl.BlockSpec((None, 2, tq, nq), lambda p, i, km: (p, 0, i, 0)),
                       pl.BlockSpec(memory_space=pltpu.SMEM)]),
        out_shape=[jax.ShapeDtypeStruct((t, a), F32), jax.ShapeDtypeStruct((npair, 2, t, nq), F32),
                   jax.ShapeDtypeStruct((npair * nq,), jnp.int32)],
        compiler_params=_params("arbitrary", "arbitrary"),
    )(_key_norm_bounds(kv, tq=tq), qg, kv, kv, _suffix_ones(tq))


def _attention_bwd(qg, kv, cin, first, do, *, tq=256):
    t, a2 = qg.shape
    a = a2 // 2
    npair = a // HEAD_PAIR
    tq = min(tq, t)
    nq = t // tq
    scale = HEAD_DIM ** -0.5

    def body(first_ref, q_ref, k_ref, v_ref, cin_ref, do_ref, ones_ref, ones_t_ref, dq_ref, dkv_ref,
             dk_acc, dv_acc, sem):
        p = pl.program_id(0)
        i = pl.program_id(1)
        first = jnp.clip(first_ref[p * nq + i], 0, i)

        @pl.when(i == 0)
        def _():
            dk_acc[...] = jnp.zeros_like(dk_acc)
            dv_acc[...] = jnp.zeros_like(dv_acc)

        ones = ones_ref[...]
        ones_t = ones_t_ref[...]
        q_all = q_ref[...] * jnp.asarray(scale, BF16)
        do_bf = do_ref[...].astype(BF16)
        qs = [q_all[:, h * HEAD_DIM:(h + 1) * HEAD_DIM] for h in range(2)]
        dos = [do_bf[:, h * HEAD_DIM:(h + 1) * HEAD_DIM] for h in range(2)]
        rows = lax.broadcasted_iota(jnp.int32, (tq, tq), 0)
        cols = lax.broadcasted_iota(jnp.int32, (tq, tq), 1)
        causal = cols < rows
        block_of_lane = lax.broadcasted_iota(jnp.int32, (tq, nq), 1)

        def tile(j, carry, masked):
            start = pl.multiple_of(j * tq, tq)
            k_all = k_ref[pl.ds(start, tq), :]
            v_all = v_ref[pl.ds(start, tq), :]
            hs = range(2)
            ks = [k_all[:, h * HEAD_DIM:(h + 1) * HEAD_DIM] for h in hs]
            vs = [v_all[:, h * HEAD_DIM:(h + 1) * HEAD_DIM] for h in hs]
            zs = [_nt(qs[h], ks[h]) for h in hs]
            das = [_nt(dos[h], vs[h]) for h in hs]
            ls = [_log_one_minus_beta(z) for z in zs]
            if masked:
                ls = [jnp.where(causal, l, 0.0) for l in ls]
            betas = [jnp.exp(z + l) for z, l in zip(zs, ls)]
            r_locs = [_suffix_sum(l, ones) for l in ls]
            if masked:
                logws = [jnp.where(causal, z + r_loc, -jnp.inf) for z, r_loc in zip(zs, r_locs)]
            else:
                cs = [jnp.sum(jnp.where(block_of_lane == j, cin_ref[h], 0.0), axis=1, keepdims=True) for h in hs]
                logws = [z + (c + r_loc) for z, c, r_loc in zip(zs, cs, r_locs)]
            ws = [jnp.exp(lw) for lw in logws]
            gs = [w * da for w, da in zip(ws, das)]
            g_pres = [jnp.dot(g.astype(BF16), ones_t, preferred_element_type=F32) for g in gs]
            dvs = [_tn(ws[h].astype(BF16), dos[h]) for h in hs]
            dzs = [gs[h] - betas[h] * (carry[h][0] + g_pres[h]) for h in hs]
            if masked:
                dzs = [jnp.where(causal, dz, 0.0) for dz in dzs]
            dzs = [dz.astype(BF16) for dz in dzs]
            dqs = [carry[h][1] + jnp.dot(dzs[h], ks[h], preferred_element_type=F32) for h in hs]
            dks = [_tn(dzs[h], qs[h]) for h in hs]
            dk_acc[pl.ds(start, tq), :] += jnp.concatenate(dks, axis=1)
            dv_acc[pl.ds(start, tq), :] += jnp.concatenate(dvs, axis=1)
            return tuple((carry[h][0] + g_pres[h][:, tq - 1:tq], dqs[h]) for h in hs)

        init = tuple((jnp.zeros((tq, 1), F32), jnp.zeros((tq, HEAD_DIM), F32)) for _ in range(2))
        carry = lax.fori_loop(first, i, lambda j, cr: tile(j, cr, False), init)
        carry = tile(i, carry, True)
        dq_ref[...] = (jnp.concatenate([carry[0][1], carry[1][1]], axis=1) * scale).astype(dq_ref.dtype)

        @pl.when(i == nq - 1)
        def _():
            k_cols = pl.multiple_of(p * HEAD_PAIR, HEAD_PAIR)
            v_cols = pl.multiple_of((npair + p) * HEAD_PAIR, HEAD_PAIR)
            ck = pltpu.make_async_copy(dk_acc, dkv_ref.at[:, pl.ds(k_cols, HEAD_PAIR)], sem.at[0])
            cv = pltpu.make_async_copy(dv_acc, dkv_ref.at[:, pl.ds(v_cols, HEAD_PAIR)], sem.at[1])
            ck.start()
            cv.start()
            ck.wait()
            cv.wait()

    blk = pl.BlockSpec((tq, HEAD_PAIR), lambda p, i, fr: (i, p))
    tri = pl.BlockSpec((tq, tq), lambda p, i, fr: (0, 0))
    ones = _suffix_ones(tq)
    return pl.pallas_call(
        body, name="attention_bwd",
        grid_spec=pltpu.PrefetchScalarGridSpec(
            num_scalar_prefetch=1, grid=(npair, nq),
            in_specs=[blk,
                      pl.BlockSpec((t, HEAD_PAIR), lambda p, i, fr: (0, p)),
                      pl.BlockSpec((t, HEAD_PAIR), lambda p, i, fr: (0, npair + p)),
                      pl.BlockSpec((None, 2, tq, nq), lambda p, i, fr: (p, 0, i, 0)),
                      blk, tri, tri],
            out_specs=[blk, pl.BlockSpec(memory_space=pl.ANY)],
            scratch_shapes=[pltpu.VMEM((t, HEAD_PAIR), F32), pltpu.VMEM((t, HEAD_PAIR), F32),
                            pltpu.SemaphoreType.DMA((2,))]),
        out_shape=[jax.ShapeDtypeStruct((t, a), BF16), jax.ShapeDtypeStruct((t, a2), F32)],
        compiler_params=_params("arbitrary", "arbitrary"),
    )(first, qg, kv, kv, cin, do, ones, ones.T)


def _other_chips(x, y):
    return [(1 - x, y), (x, 1 - y), (1 - x, 1 - y)]


def _gather_blocks(wide, small):
    def body(w_ref, s_ref, wo_ref, so_ref, send_sems, recv_sems, local_sems):
        srcs, outs = (w_ref, s_ref), (wo_ref, so_ref)
        _run_exchange(*_block_exchange(2, lambda k, slot: srcs[k], lambda k, slot: outs[k].at[slot],
                                       send_sems, recv_sems, local_sems))

    any_spec = pl.BlockSpec(memory_space=pl.ANY)
    return pl.pallas_call(
        body, name="gather_weights",
        in_specs=[any_spec, any_spec], out_specs=[any_spec, any_spec],
        out_shape=[jax.ShapeDtypeStruct((N_BLOCKS,) + wide.shape, wide.dtype),
                   jax.ShapeDtypeStruct((N_BLOCKS,) + small.shape, small.dtype)],
        scratch_shapes=_exchange_sems(2),
    )(wide, small)


def _norm_matmul_gather(x, gain, w, rest, *, tm=1024, tn=1024):
    t, d = x.shape
    n = w.shape[1]
    tm, tn = min(tm, t), min(tn, n)
    ni, nj = t // tm, n // tn

    def body(x_ref, g_ref, w_ref, rest_ref, o_ref, all_ref, send_sems, recv_sems, local_sems):
        i, j = pl.program_id(0), pl.program_id(1)
        copies = _block_exchange(1, lambda k, slot: rest_ref, lambda k, slot: all_ref.at[slot],
                                 send_sems, recv_sems, local_sems)

        @pl.when((i == 0) & (j == 0))
        def _():
            for cp in copies[0] + copies[1]:
                cp.start()

        xn, _ = _rms_normalize(x_ref[...])
        h = (xn * g_ref[...]).astype(BF16)
        o_ref[...] = jnp.dot(h, w_ref[...], preferred_element_type=F32).astype(o_ref.dtype)

        @pl.when((i == ni - 1) & (j == nj - 1))
        def _():
            _finish_exchange(*copies)

    any_spec = pl.BlockSpec(memory_space=pl.ANY)
    return pl.pallas_call(
        body, name="in_proj_a_gather", grid=(ni, nj),
        in_specs=[pl.BlockSpec((tm, d), lambda i, j: (i, 0)),
                  pl.BlockSpec((1, d), lambda i, j: (0, 0)),
                  pl.BlockSpec((d, tn), lambda i, j: (0, j)), any_spec],
        out_specs=[pl.BlockSpec((tm, tn), lambda i, j: (i, j)), any_spec],
        out_shape=[jax.ShapeDtypeStruct((t, n), BF16), jax.ShapeDtypeStruct((N_BLOCKS,) + rest.shape, rest.dtype)],
        scratch_shapes=_exchange_sems(1),
        compiler_params=_params("arbitrary", "arbitrary"),
    )(x, gain, w, rest)


def _block_exchange(n, source, landing, send_sems, recv_sems, local_sems):
    x, y, c = lax.axis_index("x"), lax.axis_index("y"), lax.axis_index("c")
    mine = 2 * x + y
    local = [pltpu.make_async_copy(source(k, mine), landing(k, mine), local_sems.at[k]) for k in range(n)]
    sends, recvs = [], []
    for p, (px, py) in enumerate(_other_chips(x, y)):
        for k in range(n):
            sems = dict(send_sem=send_sems.at[p * n + k], recv_sem=recv_sems.at[p * n + k],
                        device_id=(px, py, c), device_id_type=MESH)
            sends.append(pltpu.make_async_remote_copy(src_ref=source(k, 2 * px + py), dst_ref=landing(k, mine),
                                                      **sems))
            recvs.append(pltpu.make_async_remote_copy(src_ref=source(k, mine), dst_ref=landing(k, 2 * px + py),
                                                      **sems))
    return local, sends, recvs


def _run_exchange(local, sends, recvs):
    for cp in local + sends:
        cp.start()
    _finish_exchange(local, sends, recvs)


def _finish_exchange(local, sends, recvs):
    for cp in recvs:
        cp.wait_recv()
    for cp in sends:
        cp.wait_send()
    for cp in local:
        cp.wait()


def _exchange_sems(n):
    return [pltpu.SemaphoreType.DMA((3 * n,)), pltpu.SemaphoreType.DMA((3 * n,)), pltpu.SemaphoreType.DMA((n,))]


def _scatter_small(small):
    def body(s_ref, out_ref, send_sems, recv_sems, local_sems):
        _run_exchange(*_block_exchange(1, lambda k, slot: s_ref.at[slot], lambda k, slot: out_ref.at[slot],
                                       send_sems, recv_sems, local_sems))

    any_spec = pl.BlockSpec(memory_space=pl.ANY)
    return pl.pallas_call(
        body, name="scatter_small",
        in_specs=[any_spec], out_specs=any_spec,
        out_shape=jax.ShapeDtypeStruct(small.shape, F32),
        scratch_shapes=_exchange_sems(1),
    )(small)


def _rms_bwd_a_scatter(du, w_in, x, dx1, gain, wide, *, tm=512):
    t, d = x.shape
    tm = min(tm, t)
    nb = t // tm
    n = len(wide)
    offsets = [sum(w.shape[1] for w in wide[:k]) for k in range(n)]
    n_wide = sum(w.shape[1] for w in wide)

    def body(*refs):
        du_ref, w_ref, x_ref, dx1_ref, g_ref = refs[:5]
        wide_refs = refs[5:5 + n]
        dx_ref, gg_ref, land_ref, send_sems, recv_sems, local_sems = refs[5 + n:]
        i = pl.program_id(0)
        copies = _block_exchange(
            n, lambda k, slot: wide_refs[k].at[slot],
            lambda k, slot: land_ref.at[slot, pl.ds(offsets[k], wide[k].shape[1])],
            send_sems, recv_sems, local_sems)

        @pl.when(i == 0)
        def _():
            gg_ref[...] = jnp.zeros_like(gg_ref)
            for cp in copies[0] + copies[1]:
                cp.start()

        dh = _nt(du_ref[...], w_ref[...])
        xn, rstd = _rms_normalize(x_ref[...])
        gg_ref[...] += jnp.sum(dh * xn, axis=0, keepdims=True)
        dxn = dh * g_ref[...]
        dx_ref[...] = dx1_ref[...] + rstd * (dxn - xn * jnp.mean(dxn * xn, axis=-1, keepdims=True))

        @pl.when(i == nb - 1)
        def _():
            _finish_exchange(*copies)

    blk = pl.BlockSpec((tm, d), lambda i: (i, 0))
    row = pl.BlockSpec((1, d), lambda i: (0, 0))
    any_spec = pl.BlockSpec(memory_space=pl.ANY)
    return pl.pallas_call(
        body, name="rms_bwd_a_scatter", grid=(nb,),
        in_specs=[pl.BlockSpec((tm, du.shape[1]), lambda i: (i, 0)),
                  pl.BlockSpec(w_in.shape, lambda i: (0, 0)), blk, blk, row] + [any_spec] * n,
        out_specs=[blk, row, any_spec],
        out_shape=[jax.ShapeDtypeStruct((t, d), F32), jax.ShapeDtypeStruct((1, d), F32),
                   jax.ShapeDtypeStruct((N_BLOCKS, n_wide, d), BF16)],
        scratch_shapes=_exchange_sems(n),
        compiler_params=_params("arbitrary"),
    )(du, w_in, x, dx1, gain, *wide)


def _swap_with_sibling(part):
    def body(p_ref, out_ref, send_sem, recv_sem):
        x, y, c = lax.axis_index("x"), lax.axis_index("y"), lax.axis_index("c")
        cp = pltpu.make_async_remote_copy(src_ref=p_ref, dst_ref=out_ref, send_sem=send_sem, recv_sem=recv_sem,
                                          device_id=(x, y, 1 - c), device_id_type=MESH)
        cp.start()
        cp.wait()

    any_spec = pl.BlockSpec(memory_space=pl.ANY)
    return pl.pallas_call(
        body, name="swap_sibling",
        in_specs=[any_spec], out_specs=any_spec,
        out_shape=jax.ShapeDtypeStruct(part.shape, part.dtype),
        scratch_shapes=[pltpu.SemaphoreType.DMA, pltpu.SemaphoreType.DMA],
    )(part)


def _sum_slots(wide, small):
    _, n_wide, d = wide.shape
    tr = small.shape[1]
    nw = n_wide // tr

    def body(w_ref, s_ref, o_ref):
        i = pl.program_id(0)

        @pl.when(i < nw)
        def _():
            w = w_ref[...].astype(F32)
            o_ref[...] = ((w[0] + w[1]) + w[2]) + w[3]

        @pl.when(i == nw)
        def _():
            o_ref[...] = ((s_ref[0] + s_ref[1]) + s_ref[2]) + s_ref[3]

    return pl.pallas_call(
        body, name="sum_slots", grid=(nw + 1,),
        in_specs=[pl.BlockSpec((N_BLOCKS, tr, d), lambda i: (0, jnp.minimum(i, nw - 1), 0)),
                  pl.BlockSpec((N_BLOCKS, tr, d), lambda i: (0, 0, 0))],
        out_specs=pl.BlockSpec((tr, d), lambda i: (i, 0)),
        out_shape=jax.ShapeDtypeStruct((n_wide + tr, d), F32),
        compiler_params=_params("arbitrary"),
    )(wide, small)


def _adamw(part_mine, part_sibling, w, m, v, *, tr=ROW_TILE):
    r, d = w.shape
    m_scale = 1.0 / (1.0 - ADAM_B1 ** ADAM_STEP)
    v_scale = 1.0 / (1.0 - ADAM_B2 ** ADAM_STEP)

    def body(a_ref, b_ref, w_ref, m_ref, v_ref, g_ref, d_ref, mo_ref, vo_ref):
        g = a_ref[...] + b_ref[...]
        m_new = ADAM_B1 * m_ref[...] + (1.0 - ADAM_B1) * g
        v_new = ADAM_B2 * v_ref[...] + (1.0 - ADAM_B2) * (g * g)
        g_ref[...] = g
        mo_ref[...] = m_new
        vo_ref[...] = v_new
        d_ref[...] = -ADAM_LR * ((m_new * m_scale) / (jnp.sqrt(v_new * v_scale) + ADAM_EPS) + ADAM_WD * w_ref[...])

    blk = pl.BlockSpec((tr, d), lambda i: (i, 0))
    return pl.pallas_call(
        body, name="adamw", grid=(r // tr,),
        in_specs=[blk] * 5, out_specs=[blk] * 4,
        out_shape=[jax.ShapeDtypeStruct((r, d), F32)] * 4,
        compiler_params=_params("parallel"),
    )(part_mine, part_sibling, w, m, v)


def _pack_wide(w_in_a, w_out_a, w_kv, w_in_b, w_out_b):
    d = w_in_a.shape[-1]
    return jnp.concatenate([w_in_a[0], w_out_a[0], w_kv, w_in_b[0], w_out_b[0].reshape(-1, d)], axis=0)


def _small_rows(n_wide):
    return ROW_TILE - n_wide % ROW_TILE if n_wide % ROW_TILE else ROW_TILE


def _pack_small(conv_a, norm_a, norm_kv, norm_b, norm_f, rows):
    d = norm_kv.shape[-1]
    parts = [jnp.concatenate([conv_a[0].reshape(-1), norm_a[0]])[None, :], norm_kv.reshape(1, d),
             norm_b.reshape(1, d), norm_f.reshape(1, d), jnp.zeros((rows - 4, d), F32)]
    return jnp.concatenate(parts, axis=0)


def _unpack(packed, shapes):
    d = packed.shape[1]
    dq = d // N_BLOCKS
    att = shapes["w_out_b"][1]
    o = 0
    out = {}
    for name, rows in (("w_in_a", d), ("w_out_a", dq), ("w_kv", dq), ("w_in_b", dq), ("w_out_b", att // N_BLOCKS)):
        out[name] = packed[o:o + rows].reshape(shapes[name])
        o += rows
    out["conv_a"] = packed[o, :3 * dq].reshape(shapes["conv_a"])
    out["norm_a"] = packed[o, 3 * dq:].reshape(shapes["norm_a"])
    out["norm_kv"] = packed[o + 1].reshape(shapes["norm_kv"])
    out["norm_b"] = packed[o + 2].reshape(shapes["norm_b"])
    out["norm_f"] = packed[o + 3].reshape(shapes["norm_f"])
    return out


WEIGHTS = ["norm_a", "w_in_a", "conv_a", "w_out_a", "norm_kv", "w_kv", "norm_b", "w_in_b", "w_out_b", "norm_f"]


def kernel(x, norm_a, w_in_a, conv_a, w_out_a, norm_kv, w_kv, norm_b, w_in_b, w_out_b, norm_f, loss_target, m_norm_a, m_w_in_a, m_conv_a, m_w_out_a, m_norm_kv, m_w_kv, m_norm_b, m_w_in_b, m_w_out_b, m_norm_f, v_norm_a, v_w_in_a, v_conv_a, v_w_out_a, v_norm_kv, v_w_kv, v_norm_b, v_w_in_b, v_w_out_b, v_norm_f):
    d = x.shape[-1]
    dq = d // N_BLOCKS
    att = w_out_b.shape[1]
    xs = x[0]
    target = loss_target[0]
    shapes = dict(norm_a=norm_a.shape, w_in_a=w_in_a.shape, conv_a=conv_a.shape, w_out_a=w_out_a.shape,
                  norm_kv=norm_kv.shape, w_kv=w_kv.shape, norm_b=norm_b.shape, w_in_b=w_in_b.shape,
                  w_out_b=w_out_b.shape, norm_f=norm_f.shape)

    w_wide = _pack_wide(w_in_a, w_out_a, w_kv, w_in_b, w_out_b)
    n_wide = w_wide.shape[0]
    n_small = _small_rows(n_wide)
    w_pack = jnp.concatenate([w_wide, _pack_small(conv_a, norm_a, norm_kv, norm_b, norm_f, n_small)], axis=0)
    m_pack = jnp.concatenate([_pack_wide(m_w_in_a, m_w_out_a, m_w_kv, m_w_in_b, m_w_out_b),
                              _pack_small(m_conv_a, m_norm_a, m_norm_kv, m_norm_b, m_norm_f, n_small)], axis=0)
    v_pack = jnp.concatenate([_pack_wide(v_w_in_a, v_w_out_a, v_w_kv, v_w_in_b, v_w_out_b),
                              _pack_small(v_conv_a, v_norm_a, v_norm_kv, v_norm_b, v_norm_f, n_small)], axis=0)
    w_bf = w_wide.astype(BF16)
    in_a_all, small_all = _gather_blocks(w_bf[:d], w_pack[n_wide:n_wide + SMALL_ROWS])
    wf_in_a = jnp.concatenate([in_a_all[b] for b in range(N_BLOCKS)], axis=1)
    conv_full = jnp.concatenate([small_all[b, 0, :3 * dq].reshape(3, dq) for b in range(N_BLOCKS)], axis=1)
    gain_a = jnp.concatenate([small_all[b, 0, 3 * dq:] for b in range(N_BLOCKS)])[None, :]
    gain_kv = norm_kv.reshape(1, d)
    gain_b = norm_b.reshape(1, d)
    gain_f = norm_f.reshape(1, d)

    u, rest_all = _norm_matmul_gather(xs, gain_a, wf_in_a, w_bf[d:])
    o2, o3 = dq, 2 * dq
    wf_out_a = rest_all[:, :o2].reshape(d, d)
    wf_kv = rest_all[:, o2:o3].reshape(d, 2 * att)
    wf_in_b = rest_all[:, o3:o3 + dq].reshape(d, 2 * att)
    wf_out_b = jnp.concatenate([rest_all[b, o3 + dq:].reshape(att, dq) for b in range(N_BLOCKS)], axis=1)
    mix = _mixer_fwd(u, conv_full)
    x1 = _matmul(mix, wf_out_a, res=xs, name="out_proj_a")
    kv = _norm_matmul(x1, gain_kv, wf_kv, out_dtype=BF16, name="kv_proj")
    qg = _norm_matmul(x1, gain_b, wf_in_b, out_dtype=BF16, name="in_proj_b")
    o, cin, first = _attention_fwd(qg, kv)
    gated, dx2, loss_cols, g_norm_f = _out_proj_b_loss(o, qg, wf_out_b, x1, gain_f, target)
    loss = lax.psum(jnp.sum(loss_cols), ("x", "y", "c"))

    g_w_out_b = _matmul_tn(gated, dx2, name="grad_w_out_b")
    do, dg = _out_proj_b_bwd(dx2, wf_out_b, o, qg)
    dq_att, dkv = _attention_bwd(qg, kv, cin, first, do)
    dqg = jnp.concatenate([dq_att, dg], axis=1)
    g_w_in_b = _matmul_tn(x1, dqg, gain=gain_b, name="grad_w_in_b")
    g_w_kv = _matmul_tn(x1, dkv, gain=gain_kv, name="grad_w_kv")
    dx1, g_norm_b, g_norm_kv = _proj_rms_bwd(x1, [dqg, dkv], [wf_in_b, wf_kv], [gain_b, gain_kv], dx2,
                                             name="rms_bwd_b")
    g_w_out_a = _matmul_tn(mix, dx1, name="grad_w_out_a")
    dmix = _matmul(dx1, wf_out_a, trans_b=True, name="d_mix")
    du, g_conv = _mixer_bwd(u, dmix, conv_full)
    g_w_in_a = _matmul_tn(xs, du, gain=gain_a, blocked_out=True, tn=d, name="grad_w_in_a")
    wide_blocks = [g_w_in_a, g_w_out_a.reshape(N_BLOCKS, dq, d), g_w_kv.reshape(N_BLOCKS, dq, d),
                   g_w_in_b.reshape(N_BLOCKS, dq, d),
                   g_w_out_b.reshape(att, N_BLOCKS, dq).transpose(1, 0, 2).reshape(N_BLOCKS, att // N_BLOCKS, d)]
    dx, g_norm_a, wide_landed = _rms_bwd_a_scatter(du, wf_in_a, xs, dx1, gain_a, wide_blocks)
    small_rows = jnp.concatenate([
        jnp.concatenate([g_conv[:3].reshape(3, N_BLOCKS, dq).transpose(1, 0, 2).reshape(N_BLOCKS, 3 * dq),
                         g_norm_a.reshape(N_BLOCKS, dq)], axis=1)[:, None, :],
        jnp.broadcast_to(g_norm_kv[None], (N_BLOCKS, 1, d)),
        jnp.broadcast_to(g_norm_b[None], (N_BLOCKS, 1, d)),
        jnp.broadcast_to(g_norm_f[None], (N_BLOCKS, 1, d)),
        jnp.zeros((N_BLOCKS, n_small - 4, d), F32)], axis=1)

    mine = _sum_slots(wide_landed, _scatter_small(small_rows))
    theirs = _swap_with_sibling(mine)
    g_tot, delta, m_new, v_new = _adamw(mine, theirs, w_pack, m_pack, v_pack)

    grads = _unpack(g_tot, shapes)
    deltas = _unpack(delta, shapes)
    new_m = _unpack(m_new, shapes)
    new_v = _unpack(v_new, shapes)
    return (loss, dx[None], *[grads[n] for n in WEIGHTS], *[deltas[n] for n in WEIGHTS],
            *[new_m[n] for n in WEIGHTS], *[new_v[n] for n in WEIGHTS])
```

```python
import functools

import jax
import jax.numpy as jnp
from jax import lax
from jax.experimental import pallas as pl
from jax.experimental.pallas import tpu as pltpu

F32 = jnp.float32
BF16 = jnp.bfloat16
MESH = pl.DeviceIdType.MESH

HEAD_DIM = 64
HEAD_PAIR = 2 * HEAD_DIM
RMS_EPS = 1e-6
ADAM_LR = 0.001
ADAM_B1 = 0.9
ADAM_B2 = 0.999
ADAM_EPS = 1e-08
ADAM_WD = 0.01
ADAM_STEP = 10
N_BLOCKS = 4
SMALL_ROWS = 8
ROW_TILE = 256
VMEM_LIMIT_V7X = 56 * 1024 * 1024


def _params(*sem):
    return pltpu.CompilerParams(dimension_semantics=sem if sem else None, vmem_limit_bytes=VMEM_LIMIT_V7X)


def _sigmoid(x):
    return 1.0 / (1.0 + jnp.exp(-x))


def _rms_normalize(x):
    rstd = lax.rsqrt(jnp.mean(x * x, axis=-1, keepdims=True) + RMS_EPS)
    return x * rstd, rstd


def _norm_matmul(x, gain, w, *, out_dtype, name, tm=1024, tn=1024):
    t, d = x.shape
    n = w.shape[1]
    tm, tn = min(tm, t), min(tn, n)

    def body(x_ref, g_ref, w_ref, o_ref):
        xn, _ = _rms_normalize(x_ref[...])
        h = (xn * g_ref[...]).astype(BF16)
        o_ref[...] = jnp.dot(h, w_ref[...], preferred_element_type=F32).astype(o_ref.dtype)

    return pl.pallas_call(
        body, name=name, grid=(t // tm, n // tn),
        in_specs=[pl.BlockSpec((tm, d), lambda i, j: (i, 0)),
                  pl.BlockSpec((1, d), lambda i, j: (0, 0)),
                  pl.BlockSpec((d, tn), lambda i, j: (0, j))],
        out_specs=pl.BlockSpec((tm, tn), lambda i, j: (i, j)),
        out_shape=jax.ShapeDtypeStruct((t, n), out_dtype),
        compiler_params=_params("parallel", "arbitrary"),
    )(x, gain, w)


def _matmul(a, w, *, name, trans_b=False, res=None, out_dtype=F32, tm=1024, tn=1024, tk=1024):
    t, k = a.shape
    n = w.shape[0] if trans_b else w.shape[1]
    tm, tn, tk = min(tm, t), min(tn, n), min(tk, k)
    nk = k // tk

    def body(*refs):
        if res is None:
            a_ref, w_ref, o_ref, acc = refs
        else:
            a_ref, w_ref, r_ref, o_ref, acc = refs
        kk = pl.program_id(2)

        @pl.when(kk == 0)
        def _():
            acc[...] = jnp.zeros_like(acc)

        av = a_ref[...].astype(BF16)
        if trans_b:
            acc[...] += lax.dot_general(av, w_ref[...], (((1,), (1,)), ((), ())), preferred_element_type=F32)
        else:
            acc[...] += jnp.dot(av, w_ref[...], preferred_element_type=F32)

        @pl.when(kk == nk - 1)
        def _():
            r = acc[...]
            if res is not None:
                r = r + r_ref[...]
            o_ref[...] = r.astype(o_ref.dtype)

    in_specs = [pl.BlockSpec((tm, tk), lambda i, j, kk: (i, kk)),
                pl.BlockSpec((tn, tk), lambda i, j, kk: (j, kk)) if trans_b
                else pl.BlockSpec((tk, tn), lambda i, j, kk: (kk, j))]
    args = [a, w]
    if res is not None:
        in_specs.append(pl.BlockSpec((tm, tn), lambda i, j, kk: (i, j)))
        args.append(res)
    return pl.pallas_call(
        body, name=name, grid=(t // tm, n // tn, nk),
        in_specs=in_specs,
        out_specs=pl.BlockSpec((tm, tn), lambda i, j, kk: (i, j)),
        out_shape=jax.ShapeDtypeStruct((t, n), out_dtype),
        scratch_shapes=[pltpu.VMEM((tm, tn), F32)],
        compiler_params=_params("parallel", "parallel", "arbitrary"),
    )(*args)


def _matmul_tn(a, b, *, name, gain=None, block_cols=None, tt=1024, tn=1024):
    t, k = a.shape
    n = b.shape[1]
    tt, tn = min(tt, t), min(tn, n)
    nt = t // tt
    per_step = tn // block_cols if block_cols else 1

    def body(*refs):
        if gain is None:
            a_ref, b_ref, o_ref, acc = refs
            av = a_ref[...].astype(BF16)
        else:
            a_ref, g_ref, b_ref, o_ref, acc = refs
            xn, _ = _rms_normalize(a_ref[...])
            av = (xn * g_ref[...]).astype(BF16)

        @pl.when(pl.program_id(1) == 0)
        def _():
            acc[...] = jnp.zeros_like(acc)

        acc[...] += lax.dot_general(av, b_ref[...].astype(BF16), (((0,), (0,)), ((), ())),
                                    preferred_element_type=F32)

        @pl.when(pl.program_id(1) == nt - 1)
        def _():
            if block_cols:
                for c in range(per_step):
                    o_ref[c] = acc[:, c * block_cols:(c + 1) * block_cols].astype(o_ref.dtype)
            else:
                o_ref[...] = acc[...].astype(o_ref.dtype)

    in_specs = [pl.BlockSpec((tt, k), lambda j, s: (s, 0))]
    args = [a]
    if gain is not None:
        in_specs.append(pl.BlockSpec((1, k), lambda j, s: (0, 0)))
        args.append(gain)
    in_specs.append(pl.BlockSpec((tt, tn), lambda j, s: (s, j)))
    args.append(b)
    if block_cols:
        out_spec = pl.BlockSpec((per_step, k, block_cols), lambda j, s: (j, 0, 0))
        out_shape = jax.ShapeDtypeStruct((n // block_cols, k, block_cols), BF16)
    else:
        out_spec = pl.BlockSpec((k, tn), lambda j, s: (0, j))
        out_shape = jax.ShapeDtypeStruct((k, n), BF16)
    return pl.pallas_call(
        body, name=name, grid=(n // tn, nt),
        in_specs=in_specs, out_specs=out_spec, out_shape=out_shape,
        scratch_shapes=[pltpu.VMEM((k, tn), F32)],
        compiler_params=_params("parallel", "arbitrary"),
    )(*args)


HALO = 16


def _shift_down(v, halo, k):
    rows = lax.broadcasted_iota(jnp.int32, v.shape, 0)
    out = pltpu.roll(v, k, 0)
    for r in range(k):
        out = jnp.where(rows == r, halo[HALO - k + r:HALO - k + r + 1, :], out)
    return out


def _shift_up(v, halo, k):
    n = v.shape[0]
    rows = lax.broadcasted_iota(jnp.int32, v.shape, 0)
    out = pltpu.roll(v, n - k, 0)
    for r in range(k):
        out = jnp.where(rows == n - k + r, halo[r:r + 1, :], out)
    return out


def _mixer_fwd(u, conv, *, tm=256):
    t, d4 = u.shape
    d = d4 // 4
    tm = min(tm, t)
    hb = tm // HALO

    def body(b_ref, c_ref, x_ref, g_ref, ch_ref, xh_ref, w_ref, m_ref):
        i = pl.program_id(0)
        y1 = c_ref[...].astype(F32) * x_ref[...].astype(F32)
        prev = ch_ref[...].astype(F32) * xh_ref[...].astype(F32)
        prev = jnp.where(i == 0, 0.0, prev)
        w = w_ref[...]
        yc = w[2:3, :] * y1 + w[1:2, :] * _shift_down(y1, prev, 1) + w[0:1, :] * _shift_down(y1, prev, 2)
        g = g_ref[...].astype(F32)
        m_ref[...] = (b_ref[...].astype(F32) * yc * (g * _sigmoid(g))).astype(m_ref.dtype)

    def col(c):
        return pl.BlockSpec((tm, d), lambda i: (i, c))

    def prev_rows(c):
        return pl.BlockSpec((HALO, d), lambda i: (jnp.maximum(i * hb - 1, 0), c))

    return pl.pallas_call(
        body, name="mixer_fwd", grid=(t // tm,),
        in_specs=[col(0), col(1), col(2), col(3), prev_rows(1), prev_rows(2), pl.BlockSpec((3, d), lambda i: (0, 0))],
        out_specs=pl.BlockSpec((tm, d), lambda i: (i, 0)),
        out_shape=jax.ShapeDtypeStruct((t, d), BF16),
        compiler_params=_params("parallel"),
    )(u, u, u, u, u, u, conv)


def _mixer_bwd(u, dm, conv, *, tm=256):
    t, d4 = u.shape
    d = d4 // 4
    tm = min(tm, t)
    hb = tm // HALO
    nb = t // tm

    def body(b_ref, c_ref, x_ref, g_ref, dm_ref, ch_ref, xh_ref, bn_ref, gn_ref, dmn_ref, w_ref, du_ref, gw_ref):
        i = pl.program_id(0)
        w = w_ref[...]
        b = b_ref[...].astype(F32)
        c = c_ref[...].astype(F32)
        xin = x_ref[...].astype(F32)
        g = g_ref[...].astype(F32)
        dm_v = dm_ref[...]
        y1 = c * xin
        prev = jnp.where(i == 0, 0.0, ch_ref[...].astype(F32) * xh_ref[...].astype(F32))
        y1m1 = _shift_down(y1, prev, 1)
        y1m2 = _shift_down(y1, prev, 2)
        yc = w[2:3, :] * y1 + w[1:2, :] * y1m1 + w[0:1, :] * y1m2
        sg = _sigmoid(g)
        s = g * sg
        ds = sg * (1.0 + g * (1.0 - sg))
        dyc = dm_v * b * s
        gn = gn_ref[...].astype(F32)
        nxt = dmn_ref[...] * bn_ref[...].astype(F32) * (gn * _sigmoid(gn))
        nxt = jnp.where(i == nb - 1, 0.0, nxt)
        dy1 = w[2:3, :] * dyc + w[1:2, :] * _shift_up(dyc, nxt, 1) + w[0:1, :] * _shift_up(dyc, nxt, 2)
        du_ref[:, 0:d] = (dm_v * yc * s).astype(du_ref.dtype)
        du_ref[:, d:2 * d] = (dy1 * xin).astype(du_ref.dtype)
        du_ref[:, 2 * d:3 * d] = (dy1 * c).astype(du_ref.dtype)
        du_ref[:, 3 * d:4 * d] = (dm_v * b * yc * ds).astype(du_ref.dtype)

        @pl.when(i == 0)
        def _():
            gw_ref[...] = jnp.zeros_like(gw_ref)

        gw_ref[0:1, :] += jnp.sum(dyc * y1m2, axis=0, keepdims=True)
        gw_ref[1:2, :] += jnp.sum(dyc * y1m1, axis=0, keepdims=True)
        gw_ref[2:3, :] += jnp.sum(dyc * y1, axis=0, keepdims=True)

    def col(c):
        return pl.BlockSpec((tm, d), lambda i: (i, c))

    def prev_rows(c):
        return pl.BlockSpec((HALO, d), lambda i: (jnp.maximum(i * hb - 1, 0), c))

    def next_rows(c):
        return pl.BlockSpec((HALO, d), lambda i: (jnp.minimum((i + 1) * hb, nb * hb - 1), c))

    return pl.pallas_call(
        body, name="mixer_bwd", grid=(nb,),
        in_specs=[col(0), col(1), col(2), col(3), pl.BlockSpec((tm, d), lambda i: (i, 0)),
                  prev_rows(1), prev_rows(2), next_rows(0), next_rows(3),
                  pl.BlockSpec((HALO, d), lambda i: (jnp.minimum((i + 1) * hb, nb * hb - 1), 0)),
                  pl.BlockSpec((3, d), lambda i: (0, 0))],
        out_specs=[pl.BlockSpec((tm, d4), lambda i: (i, 0)), pl.BlockSpec((SMALL_ROWS, d), lambda i: (0, 0))],
        out_shape=[jax.ShapeDtypeStruct((t, d4), BF16), jax.ShapeDtypeStruct((SMALL_ROWS, d), F32)],
        compiler_params=_params("arbitrary"),
    )(u, u, u, u, dm, u, u, u, u, dm, conv)


def _out_proj_b_loss(o, qg, w_out, x1, gain, target, *, tm=512):
    t, a = o.shape
    d = x1.shape[1]
    tm = min(tm, t)

    def body(o_ref, g_ref, w_ref, x_ref, gain_ref, y_ref, gated_ref, dx_ref, loss_ref, gg_ref):
        @pl.when(pl.program_id(0) == 0)
        def _():
            loss_ref[...] = jnp.zeros_like(loss_ref)
            gg_ref[...] = jnp.zeros_like(gg_ref)

        g = g_ref[...].astype(F32)
        gated = (o_ref[...] * (g * _sigmoid(g))).astype(BF16)
        gated_ref[...] = gated
        x2 = x_ref[...] + jnp.dot(gated, w_ref[...], preferred_element_type=F32)
        xn, rstd = _rms_normalize(x2)
        gv = gain_ref[...]
        err = xn * gv - y_ref[...]
        loss_ref[...] += jnp.sum(err * err, axis=0, keepdims=True) * (0.5 / d)
        dy = err * (1.0 / d)
        gg_ref[...] += jnp.sum(dy * xn, axis=0, keepdims=True)
        dxn = dy * gv
        dx_ref[...] = rstd * (dxn - xn * jnp.mean(dxn * xn, axis=-1, keepdims=True))

    blk = pl.BlockSpec((tm, d), lambda i: (i, 0))
    blk_a = pl.BlockSpec((tm, a), lambda i: (i, 0))
    row = pl.BlockSpec((1, d), lambda i: (0, 0))
    return pl.pallas_call(
        body, name="out_proj_b_loss", grid=(t // tm,),
        in_specs=[blk_a, pl.BlockSpec((tm, a), lambda i: (i, 1)), pl.BlockSpec(w_out.shape, lambda i: (0, 0)),
                  blk, row, blk],
        out_specs=[blk_a, blk, row, row],
        out_shape=[jax.ShapeDtypeStruct((t, a), BF16), jax.ShapeDtypeStruct((t, d), F32),
                   jax.ShapeDtypeStruct((1, d), F32), jax.ShapeDtypeStruct((1, d), F32)],
        compiler_params=_params("arbitrary"),
    )(o, qg, w_out, x1, gain, target)


def _out_proj_b_bwd(dx2, w_out, o, qg, *, tm=512):
    t, a = o.shape
    d = dx2.shape[1]
    tm = min(tm, t)

    def body(dx_ref, w_ref, o_ref, g_ref, do_ref, dg_ref):
        da = _nt(dx_ref[...].astype(BF16), w_ref[...])
        g = g_ref[...].astype(F32)
        sg = _sigmoid(g)
        do_ref[...] = da * (g * sg)
        dg_ref[...] = (da * o_ref[...] * (sg * (1.0 + g * (1.0 - sg)))).astype(dg_ref.dtype)

    blk_a = pl.BlockSpec((tm, a), lambda i: (i, 0))
    return pl.pallas_call(
        body, name="out_proj_b_bwd", grid=(t // tm,),
        in_specs=[pl.BlockSpec((tm, d), lambda i: (i, 0)), pl.BlockSpec(w_out.shape, lambda i: (0, 0)), blk_a,
                  pl.BlockSpec((tm, a), lambda i: (i, 1))],
        out_specs=[blk_a, blk_a],
        out_shape=[jax.ShapeDtypeStruct((t, a), F32), jax.ShapeDtypeStruct((t, a), BF16)],
        compiler_params=_params("parallel"),
    )(dx2, w_out, o, qg)


def _proj_rms_bwd(x, cots, ws, gains, dres, *, name, tm=512):
    t, d = x.shape
    tm = min(tm, t)
    nk = len(cots)

    def body(*refs):
        x_ref = refs[0]
        cot_refs = refs[1:1 + nk]
        w_refs = refs[1 + nk:1 + 2 * nk]
        g_refs = refs[1 + 2 * nk:1 + 3 * nk]
        dres_ref = refs[1 + 3 * nk]
        dx_ref = refs[2 + 3 * nk]
        gg_refs = refs[3 + 3 * nk:]

        @pl.when(pl.program_id(0) == 0)
        def _():
            for gg in gg_refs:
                gg[...] = jnp.zeros_like(gg)

        xn, rstd = _rms_normalize(x_ref[...])
        dxn = None
        for cot_ref, w_ref, g_ref, gg in zip(cot_refs, w_refs, g_refs, gg_refs):
            dh = _nt(cot_ref[...].astype(BF16), w_ref[...])
            gg[...] += jnp.sum(dh * xn, axis=0, keepdims=True)
            term = dh * g_ref[...]
            dxn = term if dxn is None else dxn + term
        dx_ref[...] = dres_ref[...] + rstd * (dxn - xn * jnp.mean(dxn * xn, axis=-1, keepdims=True))

    blk = pl.BlockSpec((tm, d), lambda i: (i, 0))
    row = pl.BlockSpec((1, d), lambda i: (0, 0))
    return pl.pallas_call(
        body, name=name, grid=(t // tm,),
        in_specs=([blk] + [pl.BlockSpec((tm, c.shape[1]), lambda i: (i, 0)) for c in cots]
                  + [pl.BlockSpec(w.shape, lambda i: (0, 0)) for w in ws] + [row] * nk + [blk]),
        out_specs=[blk] + [row] * nk,
        out_shape=[jax.ShapeDtypeStruct((t, d), F32)] + [jax.ShapeDtypeStruct((1, d), F32)] * nk,
        compiler_params=_params("arbitrary"),
    )(x, *cots, *ws, *gains, dres)


def _suffix_ones(n):
    r = lax.broadcasted_iota(jnp.int32, (n, n), 0)
    c = lax.broadcasted_iota(jnp.int32, (n, n), 1)
    return (r >= c).astype(BF16)


def _suffix_sum(l, ones):
    return jnp.dot(l.astype(BF16), ones, preferred_element_type=F32)


def _log_one_minus_beta(z):
    return -(jnp.maximum(z, 0.0) + jnp.log(1.0 + jnp.exp(-jnp.abs(z))))


def _nt(a, b):
    return lax.dot_general(a, b, (((1,), (1,)), ((), ())), preferred_element_type=F32)


def _tn(a, b):
    return lax.dot_general(a, b, (((0,), (0,)), ((), ())), preferred_element_type=F32)


EXP_UNDERFLOW = 104.0
NORM_MARGIN = 1.01


def _key_norm_bounds(kv, *, tq):
    t, a2 = kv.shape
    a = a2 // 2
    heads = a // HEAD_DIM
    nq = t // tq

    per_step = min(4, nq)
    lanes = 128

    def body(k_ref, sel_ref, o_ref):
        k = k_ref[...].astype(F32)
        n2 = jnp.dot((k * k).astype(BF16), sel_ref[...], preferred_element_type=F32)
        o_ref[...] = jnp.sqrt(jnp.max(n2.reshape(per_step, tq, lanes), axis=1))

    head_of_col = lax.broadcasted_iota(jnp.int32, (a, lanes), 0) // HEAD_DIM
    sel = (head_of_col == lax.broadcasted_iota(jnp.int32, (a, lanes), 1)).astype(BF16)
    norms = pl.pallas_call(
        body, name="key_norms", grid=(nq // per_step,),
        in_specs=[pl.BlockSpec((per_step * tq, a), lambda i: (i, 0)), pl.BlockSpec((a, lanes), lambda i: (0, 0))],
        out_specs=pl.BlockSpec((None, per_step, lanes), lambda i: (i, 0, 0)),
        out_shape=jax.ShapeDtypeStruct((nq // per_step, per_step, lanes), F32),
        compiler_params=_params("parallel"),
    )(kv, sel)
    return lax.cummax(norms.reshape(nq, lanes)[:, :heads], axis=0).T.reshape(-1)


def _attention_fwd(qg, kv, *, tq=256, n_sub=1):
    t, a2 = qg.shape
    a = a2 // 2
    npair = a // HEAD_PAIR
    tq = min(tq, t)
    scale = HEAD_DIM ** -0.5
    nq = t // tq
    ts = tq // n_sub

    def body(kmax_ref, q_ref, k_ref, v_ref, ones_ref, o_ref, cin_ref, first_ref):
        p = pl.program_id(0)
        i = pl.program_id(1)
        ones = ones_ref[...]
        q_all = q_ref[...] * jnp.asarray(scale, BF16)
        q2 = q_all.astype(F32) * q_all.astype(F32)
        chains = [(h, r) for h in range(2) for r in range(n_sub)]
        qs = [q_all[r * ts:(r + 1) * ts, h * HEAD_DIM:(h + 1) * HEAD_DIM] for h, r in chains]
        q_norm = [jnp.sqrt(jnp.sum(q2[r * ts:(r + 1) * ts, h * HEAD_DIM:(h + 1) * HEAD_DIM], axis=1, keepdims=True))
                  * NORM_MARGIN for h, r in chains]
        rows = lax.broadcasted_iota(jnp.int32, (ts, tq), 0)
        cols = lax.broadcasted_iota(jnp.int32, (ts, tq), 1)
        causal = [cols < rows + r * ts for r in range(n_sub)]
        block_of_lane = lax.broadcasted_iota(jnp.int32, (ts, nq), 1)
        cin_ref[...] = jnp.zeros_like(cin_ref)

        def any_weight_left_of(j, carry):
            jj = jnp.maximum(j - 1, 0)
            bound = None
            for n, (h, r) in enumerate(chains):
                top = jnp.max(carry[n][0] + q_norm[n] * kmax_ref[(2 * p + h) * nq + jj])
                bound = top if bound is None else jnp.maximum(bound, top)
            return bound > -EXP_UNDERFLOW

        def tile(j, carry, masked):
            start = pl.multiple_of(j * tq, tq)
            k_all = k_ref[pl.ds(start, tq), :]
            v_all = v_ref[pl.ds(start, tq), :]
            ks = [k_all[:, h * HEAD_DIM:(h + 1) * HEAD_DIM] for h in range(2)]
            vs = [v_all[:, h * HEAD_DIM:(h + 1) * HEAD_DIM] for h in range(2)]
            zs = [_nt(qs[n], ks[h]) for n, (h, r) in enumerate(chains)]
            ls = [_log_one_minus_beta(z) for z in zs]
            if masked:
                ls = [jnp.where(causal[r], l, 0.0) for l, (h, r) in zip(ls, chains)]
            r_locs = [_suffix_sum(l, ones) for l in ls]
            logws = [z + (carry[n][0] + r_loc) for n, (z, r_loc) in enumerate(zip(zs, r_locs))]
            if masked:
                logws = [jnp.where(causal[r], lw, -jnp.inf) for lw, (h, r) in zip(logws, chains)]
            ws = [jnp.exp(lw).astype(BF16) for lw in logws]
            out = []
            for n, (h, r) in enumerate(chains):
                acc = carry[n][1] + jnp.dot(ws[n], vs[h], preferred_element_type=F32)
                c = carry[n][0] + r_locs[n][:, 0:1]
                cin_ref[h, r * ts:(r + 1) * ts, :] = jnp.where(block_of_lane == j - 1, c,
                                                               cin_ref[h, r * ts:(r + 1) * ts, :])
                out.append((c, acc))
            return tuple(out)

        init = tuple((jnp.zeros((ts, 1), F32), jnp.zeros((ts, HEAD_DIM), F32)) for _ in chains)
        carry = tile(i, init, True)

        def visit(state):
            j, _, cr = state
            cr = tile(j - 1, cr, False)
            return j - 1, any_weight_left_of(j - 1, cr), cr

        first, _, carry = lax.while_loop(lambda s: (s[0] > 0) & s[1], visit,
                                         (i, any_weight_left_of(i, carry), carry))
        first_ref[p * nq + i] = first
        heads = [jnp.concatenate([carry[h * n_sub + r][1] for r in range(n_sub)], axis=0) for h in range(2)]
        o_ref[...] = jnp.concatenate(heads, axis=1)

    return pl.pallas_call(
        body, name="attention_fwd",
        grid_spec=pltpu.PrefetchScalarGridSpec(
            num_scalar_prefetch=1, grid=(npair, nq),
            in_specs=[pl.BlockSpec((tq, HEAD_PAIR), lambda p, i, km: (i, p)),
                      pl.BlockSpec((t, HEAD_PAIR), lambda p, i, km: (0, p)),
                      pl.BlockSpec((t, HEAD_PAIR), lambda p, i, km: (0, npair + p)),
                      pl.BlockSpec((tq, tq), lambda p, i, km: (0, 0))],
            out_specs=[pl.BlockSpec((tq, HEAD_PAIR), lambda p, i, km: (i, p)),
                       pl.BlockSpec((None, 2, tq, nq), lambda p, i, km: (p, 0, i, 0)),
                       pl.BlockSpec(memory_space=pltpu.SMEM)]),
        out_shape=[jax.ShapeDtypeStruct((t, a), F32), jax.ShapeDtypeStruct((npair, 2, t, nq), F32),
                   jax.ShapeDtypeStruct((npair * nq,), jnp.int32)],
        compiler_params=_params("arbitrary", "arbitrary"),
    )(_key_norm_bounds(kv, tq=tq), qg, kv, kv, _suffix_ones(tq))


def _attention_bwd(qg, kv, cin, first, do, *, tq=256):
    t, a2 = qg.shape
    a = a2 // 2
    npair = a // HEAD_PAIR
    tq = min(tq, t)
    nq = t // tq
    scale = HEAD_DIM ** -0.5

    def body(first_ref, q_ref, k_ref, v_ref, cin_ref, do_ref, ones_ref, ones_t_ref, dq_ref, dkv_ref,
             dk_acc, dv_acc, sem):
        p = pl.program_id(0)
        i = pl.program_id(1)
        first = jnp.clip(first_ref[p * nq + i], 0, i)

        @pl.when(i == 0)
        def _():
            dk_acc[...] = jnp.zeros_like(dk_acc)
            dv_acc[...] = jnp.zeros_like(dv_acc)

        ones = ones_ref[...]
        ones_t = ones_t_ref[...]
        q_all = q_ref[...] * jnp.asarray(scale, BF16)
        do_bf = do_ref[...].astype(BF16)
        qs = [q_all[:, h * HEAD_DIM:(h + 1) * HEAD_DIM] for h in range(2)]
        dos = [do_bf[:, h * HEAD_DIM:(h + 1) * HEAD_DIM] for h in range(2)]
        rows = lax.broadcasted_iota(jnp.int32, (tq, tq), 0)
        cols = lax.broadcasted_iota(jnp.int32, (tq, tq), 1)
        causal = cols < rows
        block_of_lane = lax.broadcasted_iota(jnp.int32, (tq, nq), 1)

        def tile(j, carry, masked):
            start = pl.multiple_of(j * tq, tq)
            k_all = k_ref[pl.ds(start, tq), :]
            v_all = v_ref[pl.ds(start, tq), :]
            hs = range(2)
            ks = [k_all[:, h * HEAD_DIM:(h + 1) * HEAD_DIM] for h in hs]
            vs = [v_all[:, h * HEAD_DIM:(h + 1) * HEAD_DIM] for h in hs]
            zs = [_nt(qs[h], ks[h]) for h in hs]
            das = [_nt(dos[h], vs[h]) for h in hs]
            ls = [_log_one_minus_beta(z) for z in zs]
            if masked:
                ls = [jnp.where(causal, l, 0.0) for l in ls]
            betas = [jnp.exp(z + l) for z, l in zip(zs, ls)]
            r_locs = [_suffix_sum(l, ones) for l in ls]
            if masked:
                logws = [jnp.where(causal, z + r_loc, -jnp.inf) for z, r_loc in zip(zs, r_locs)]
            else:
                cs = [jnp.sum(jnp.where(block_of_lane == j, cin_ref[h], 0.0), axis=1, keepdims=True) for h in hs]
                logws = [z + (c + r_loc) for z, c, r_loc in zip(zs, cs, r_locs)]
            ws = [jnp.exp(lw) for lw in logws]
            gs = [w * da for w, da in zip(ws, das)]
            g_pres = [jnp.dot(g.astype(BF16), ones_t, preferred_element_type=F32) for g in gs]
            dvs = [_tn(ws[h].astype(BF16), dos[h]) for h in hs]
            dzs = [gs[h] - betas[h] * (carry[h][0] + g_pres[h]) for h in hs]
            if masked:
                dzs = [jnp.where(causal, dz, 0.0) for dz in dzs]
            dzs = [dz.astype(BF16) for dz in dzs]
            dqs = [carry[h][1] + jnp.dot(dzs[h], ks[h], preferred_element_type=F32) for h in hs]
            dks = [_tn(dzs[h], qs[h]) for h in hs]
            dk_acc[pl.ds(start, tq), :] += jnp.concatenate(dks, axis=1)
            dv_acc[pl.ds(start, tq), :] += jnp.concatenate(dvs, axis=1)
            return tuple((carry[h][0] + g_pres[h][:, tq - 1:tq], dqs[h]) for h in hs)

        init = tuple((jnp.zeros((tq, 1), F32), jnp.zeros((tq, HEAD_DIM), F32)) for _ in range(2))
        carry = lax.fori_loop(first, i, lambda j, cr: tile(j, cr, False), init)
        carry = tile(i, carry, True)
        dq_ref[...] = (jnp.concatenate([carry[0][1], carry[1][1]], axis=1) * scale).astype(dq_ref.dtype)

        @pl.when(i == nq - 1)
        def _():
            k_cols = pl.multiple_of(p * HEAD_PAIR, HEAD_PAIR)
            v_cols = pl.multiple_of((npair + p) * HEAD_PAIR, HEAD_PAIR)
            ck = pltpu.make_async_copy(dk_acc, dkv_ref.at[:, pl.ds(k_cols, HEAD_PAIR)], sem.at[0])
            cv = pltpu.make_async_copy(dv_acc, dkv_ref.at[:, pl.ds(v_cols, HEAD_PAIR)], sem.at[1])
            ck.start()
            cv.start()
            ck.wait()
            cv.wait()

    blk = pl.BlockSpec((tq, HEAD_PAIR), lambda p, i, fr: (i, p))
    tri = pl.BlockSpec((tq, tq), lambda p, i, fr: (0, 0))
    ones = _suffix_ones(tq)
    return pl.pallas_call(
        body, name="attention_bwd",
        grid_spec=pltpu.PrefetchScalarGridSpec(
            num_scalar_prefetch=1, grid=(npair, nq),
            in_specs=[blk,
                      pl.BlockSpec((t, HEAD_PAIR), lambda p, i, fr: (0, p)),
                      pl.BlockSpec((t, HEAD_PAIR), lambda p, i, fr: (0, npair + p)),
                      pl.BlockSpec((None, 2, tq, nq), lambda p, i, fr: (p, 0, i, 0)),
                      blk, tri, tri],
            out_specs=[blk, pl.BlockSpec(memory_space=pl.ANY)],
            scratch_shapes=[pltpu.VMEM((t, HEAD_PAIR), F32), pltpu.VMEM((t, HEAD_PAIR), F32),
                            pltpu.SemaphoreType.DMA((2,))]),
        out_shape=[jax.ShapeDtypeStruct((t, a), BF16), jax.ShapeDtypeStruct((t, a2), F32)],
        compiler_params=_params("arbitrary", "arbitrary"),
    )(first, qg, kv, kv, cin, do, ones, ones.T)


def _other_chips(x, y):
    return [(1 - x, y), (x, 1 - y), (1 - x, 1 - y)]


def _gather_blocks(wide, small):
    def body(w_ref, s_ref, wo_ref, so_ref, send_sems, recv_sems, local_sems):
        srcs, outs = (w_ref, s_ref), (wo_ref, so_ref)
        _run_exchange(*_block_exchange(2, lambda k, slot: srcs[k], lambda k, slot: outs[k].at[slot],
                                       send_sems, recv_sems, local_sems))

    any_spec = pl.BlockSpec(memory_space=pl.ANY)
    return pl.pallas_call(
        body, name="gather_weights",
        in_specs=[any_spec, any_spec], out_specs=[any_spec, any_spec],
        out_shape=[jax.ShapeDtypeStruct((N_BLOCKS,) + wide.shape, wide.dtype),
                   jax.ShapeDtypeStruct((N_BLOCKS,) + small.shape, small.dtype)],
        scratch_shapes=_exchange_sems(2),
    )(wide, small)


def _norm_matmul_gather(x, gain, w, rest, *, tm=1024, tn=1024):
    t, d = x.shape
    n = w.shape[1]
    tm, tn = min(tm, t), min(tn, n)
    ni, nj = t // tm, n // tn

    def body(x_ref, g_ref, w_ref, rest_ref, o_ref, all_ref, send_sems, recv_sems, local_sems):
        i, j = pl.program_id(0), pl.program_id(1)
        copies = _block_exchange(1, lambda k, slot: rest_ref, lambda k, slot: all_ref.at[slot],
                                 send_sems, recv_sems, local_sems)

        @pl.when((i == 0) & (j == 0))
        def _():
            for cp in copies[0] + copies[1]:
                cp.start()

        xn, _ = _rms_normalize(x_ref[...])
        h = (xn * g_ref[...]).astype(BF16)
        o_ref[...] = jnp.dot(h, w_ref[...], preferred_element_type=F32).astype(o_ref.dtype)

        @pl.when((i == ni - 1) & (j == nj - 1))
        def _():
            _finish_exchange(*copies)

    any_spec = pl.BlockSpec(memory_space=pl.ANY)
    return pl.pallas_call(
        body, name="in_proj_a_gather", grid=(ni, nj),
        in_specs=[pl.BlockSpec((tm, d), lambda i, j: (i, 0)),
                  pl.BlockSpec((1, d), lambda i, j: (0, 0)),
                  pl.BlockSpec((d, tn), lambda i, j: (0, j)), any_spec],
        out_specs=[pl.BlockSpec((tm, tn), lambda i, j: (i, j)), any_spec],
        out_shape=[jax.ShapeDtypeStruct((t, n), BF16), jax.ShapeDtypeStruct((N_BLOCKS,) + rest.shape, rest.dtype)],
        scratch_shapes=_exchange_sems(1),
        compiler_params=_params("arbitrary", "arbitrary"),
    )(x, gain, w, rest)


def _block_exchange(n, source, landing, send_sems, recv_sems, local_sems):
    x, y, c = lax.axis_index("x"), lax.axis_index("y"), lax.axis_index("c")
    mine = 2 * x + y
    local = [pltpu.make_async_copy(source(k, mine), landing(k, mine), local_sems.at[k]) for k in range(n)]
    sends, recvs = [], []
    for p, (px, py) in enumerate(_other_chips(x, y)):
        for k in range(n):
            sems = dict(send_sem=send_sems.at[p * n + k], recv_sem=recv_sems.at[p * n + k],
                        device_id=(px, py, c), device_id_type=MESH)
            sends.append(pltpu.make_async_remote_copy(src_ref=source(k, 2 * px + py), dst_ref=landing(k, mine),
                                                      **sems))
            recvs.append(pltpu.make_async_remote_copy(src_ref=source(k, mine), dst_ref=landing(k, 2 * px + py),
                                                      **sems))
    return local, sends, recvs


def _run_exchange(local, sends, recvs):
    for cp in local + sends:
        cp.start()
    _finish_exchange(local, sends, recvs)


def _finish_exchange(local, sends, recvs):
    for cp in recvs:
        cp.wait_recv()
    for cp in sends:
        cp.wait_send()
    for cp in local:
        cp.wait()


def _exchange_sems(n):
    return [pltpu.SemaphoreType.DMA((3 * n,)), pltpu.SemaphoreType.DMA((3 * n,)), pltpu.SemaphoreType.DMA((n,))]


def _scatter_small(small):
    def body(s_ref, out_ref, send_sems, recv_sems, local_sems):
        _run_exchange(*_block_exchange(1, lambda k, slot: s_ref.at[slot], lambda k, slot: out_ref.at[slot],
                                       send_sems, recv_sems, local_sems))

    any_spec = pl.BlockSpec(memory_space=pl.ANY)
    return pl.pallas_call(
        body, name="scatter_small",
        in_specs=[any_spec], out_specs=any_spec,
        out_shape=jax.ShapeDtypeStruct(small.shape, F32),
        scratch_shapes=_exchange_sems(1),
    )(small)


def _rms_bwd_a_scatter(du, w_in, x, dx1, gain, wide, *, tm=512):
    t, d = x.shape
    tm = min(tm, t)
    nb = t // tm
    n = len(wide)
    offsets = [sum(w.shape[1] for w in wide[:k]) for k in range(n)]
    n_wide = sum(w.shape[1] for w in wide)

    def body(*refs):
        du_ref, w_ref, x_ref, dx1_ref, g_ref = refs[:5]
        wide_refs = refs[5:5 + n]
        dx_ref, gg_ref, land_ref, send_sems, recv_sems, local_sems = refs[5 + n:]
        i = pl.program_id(0)
        copies = _block_exchange(
            n, lambda k, slot: wide_refs[k].at[slot],
            lambda k, slot: land_ref.at[slot, pl.ds(offsets[k], wide[k].shape[1])],
            send_sems, recv_sems, local_sems)

        @pl.when(i == 0)
        def _():
            gg_ref[...] = jnp.zeros_like(gg_ref)
            for cp in copies[0] + copies[1]:
                cp.start()

        dh = _nt(du_ref[...], w_ref[...])
        xn, rstd = _rms_normalize(x_ref[...])
        gg_ref[...] += jnp.sum(dh * xn, axis=0, keepdims=True)
        dxn = dh * g_ref[...]
        dx_ref[...] = dx1_ref[...] + rstd * (dxn - xn * jnp.mean(dxn * xn, axis=-1, keepdims=True))

        @pl.when(i == nb - 1)
        def _():
            _finish_exchange(*copies)

    blk = pl.BlockSpec((tm, d), lambda i: (i, 0))
    row = pl.BlockSpec((1, d), lambda i: (0, 0))
    any_spec = pl.BlockSpec(memory_space=pl.ANY)
    return pl.pallas_call(
        body, name="rms_bwd_a_scatter", grid=(nb,),
        in_specs=[pl.BlockSpec((tm, du.shape[1]), lambda i: (i, 0)),
                  pl.BlockSpec(w_in.shape, lambda i: (0, 0)), blk, blk, row] + [any_spec] * n,
        out_specs=[blk, row, any_spec],
        out_shape=[jax.ShapeDtypeStruct((t, d), F32), jax.ShapeDtypeStruct((1, d), F32),
                   jax.ShapeDtypeStruct((N_BLOCKS, n_wide, d), BF16)],
        scratch_shapes=_exchange_sems(n),
        compiler_params=_params("arbitrary"),
    )(du, w_in, x, dx1, gain, *wide)


def _swap_with_sibling(part):
    def body(p_ref, out_ref, send_sem, recv_sem):
        x, y, c = lax.axis_index("x"), lax.axis_index("y"), lax.axis_index("c")
        cp = pltpu.make_async_remote_copy(src_ref=p_ref, dst_ref=out_ref, send_sem=send_sem, recv_sem=recv_sem,
                                          device_id=(x, y, 1 - c), device_id_type=MESH)
        cp.start()
        cp.wait()

    any_spec = pl.BlockSpec(memory_space=pl.ANY)
    return pl.pallas_call(
        body, name="swap_sibling",
        in_specs=[any_spec], out_specs=any_spec,
        out_shape=jax.ShapeDtypeStruct(part.shape, part.dtype),
        scratch_shapes=[pltpu.SemaphoreType.DMA, pltpu.SemaphoreType.DMA],
    )(part)


def _sum_slots(wide, small):
    _, n_wide, d = wide.shape
    tr = small.shape[1]
    nw = n_wide // tr

    def body(w_ref, s_ref, o_ref):
        i = pl.program_id(0)

        @pl.when(i < nw)
        def _():
            w = w_ref[...].astype(F32)
            o_ref[...] = ((w[0] + w[1]) + w[2]) + w[3]

        @pl.when(i == nw)
        def _():
            o_ref[...] = ((s_ref[0] + s_ref[1]) + s_ref[2]) + s_ref[3]

    return pl.pallas_call(
        body, name="sum_slots", grid=(nw + 1,),
        in_specs=[pl.BlockSpec((N_BLOCKS, tr, d), lambda i: (0, jnp.minimum(i, nw - 1), 0)),
                  pl.BlockSpec((N_BLOCKS, tr, d), lambda i: (0, 0, 0))],
        out_specs=pl.BlockSpec((tr, d), lambda i: (i, 0)),
        out_shape=jax.ShapeDtypeStruct((n_wide + tr, d), F32),
        compiler_params=_params("arbitrary"),
    )(wide, small)


def _adamw(part_mine, part_sibling, row0, w, m, v, *, name):
    r, d = w.shape
    tr = next(c for c in (ROW_TILE, 128, 64, 32, 16, 8) if r % c == 0 and row0 % c == 0)
    first_block = row0 // tr
    m_scale = 1.0 / (1.0 - ADAM_B1 ** ADAM_STEP)
    v_scale = 1.0 / (1.0 - ADAM_B2 ** ADAM_STEP)

    def body(a_ref, b_ref, w_ref, m_ref, v_ref, g_ref, d_ref, mo_ref, vo_ref):
        g = a_ref[...] + b_ref[...]
        m_new = ADAM_B1 * m_ref[...] + (1.0 - ADAM_B1) * g
        v_new = ADAM_B2 * v_ref[...] + (1.0 - ADAM_B2) * (g * g)
        g_ref[...] = g
        mo_ref[...] = m_new
        vo_ref[...] = v_new
        d_ref[...] = -ADAM_LR * ((m_new * m_scale) / (jnp.sqrt(v_new * v_scale) + ADAM_EPS) + ADAM_WD * w_ref[...])

    blk = pl.BlockSpec((tr, d), lambda i: (i, 0))
    part = pl.BlockSpec((tr, d), lambda i: (first_block + i, 0))
    return pl.pallas_call(
        body, name=name, grid=(r // tr,),
        in_specs=[part, part, blk, blk, blk], out_specs=[blk] * 4,
        out_shape=[jax.ShapeDtypeStruct((r, d), F32)] * 4,
        compiler_params=_params("parallel"),
    )(part_mine, part_sibling, w, m, v)


def _wide_views(w_in_a, w_out_a, w_kv, w_in_b, w_out_b):
    d = w_in_a.shape[-1]
    return [w_in_a[0], w_out_a[0], w_kv, w_in_b[0], w_out_b[0].reshape(-1, d)]


WIDE = ["w_in_a", "w_out_a", "w_kv", "w_in_b", "w_out_b"]


def _small_rows(n_wide):
    return ROW_TILE - n_wide % ROW_TILE if n_wide % ROW_TILE else ROW_TILE


def _pack_small(conv_a, norm_a, norm_kv, norm_b, norm_f, rows):
    d = norm_kv.shape[-1]
    parts = [jnp.concatenate([conv_a[0].reshape(-1), norm_a[0]])[None, :], norm_kv.reshape(1, d),
             norm_b.reshape(1, d), norm_f.reshape(1, d), jnp.zeros((rows - 4, d), F32)]
    return jnp.concatenate(parts, axis=0)


def _unpack_small(packed, shapes):
    dq = packed.shape[1] // N_BLOCKS
    return dict(conv_a=packed[0, :3 * dq].reshape(shapes["conv_a"]), norm_a=packed[0, 3 * dq:].reshape(shapes["norm_a"]),
                norm_kv=packed[1].reshape(shapes["norm_kv"]), norm_b=packed[2].reshape(shapes["norm_b"]),
                norm_f=packed[3].reshape(shapes["norm_f"]))


WEIGHTS = ["norm_a", "w_in_a", "conv_a", "w_out_a", "norm_kv", "w_kv", "norm_b", "w_in_b", "w_out_b", "norm_f"]


def kernel(x, norm_a, w_in_a, conv_a, w_out_a, norm_kv, w_kv, norm_b, w_in_b, w_out_b, norm_f, loss_target, m_norm_a, m_w_in_a, m_conv_a, m_w_out_a, m_norm_kv, m_w_kv, m_norm_b, m_w_in_b, m_w_out_b, m_norm_f, v_norm_a, v_w_in_a, v_conv_a, v_w_out_a, v_norm_kv, v_w_kv, v_norm_b, v_w_in_b, v_w_out_b, v_norm_f):
    d = x.shape[-1]
    dq = d // N_BLOCKS
    att = w_out_b.shape[1]
    xs = x[0]
    target = loss_target[0]
    shapes = dict(norm_a=norm_a.shape, w_in_a=w_in_a.shape, conv_a=conv_a.shape, w_out_a=w_out_a.shape,
                  norm_kv=norm_kv.shape, w_kv=w_kv.shape, norm_b=norm_b.shape, w_in_b=w_in_b.shape,
                  w_out_b=w_out_b.shape, norm_f=norm_f.shape)

    w_wide = _wide_views(w_in_a, w_out_a, w_kv, w_in_b, w_out_b)
    m_wide = _wide_views(m_w_in_a, m_w_out_a, m_w_kv, m_w_in_b, m_w_out_b)
    v_wide = _wide_views(v_w_in_a, v_w_out_a, v_w_kv, v_w_in_b, v_w_out_b)
    n_wide = sum(w.shape[0] for w in w_wide)
    n_small = _small_rows(n_wide)
    w_small = _pack_small(conv_a, norm_a, norm_kv, norm_b, norm_f, n_small)
    m_small = _pack_small(m_conv_a, m_norm_a, m_norm_kv, m_norm_b, m_norm_f, n_small)
    v_small = _pack_small(v_conv_a, v_norm_a, v_norm_kv, v_norm_b, v_norm_f, n_small)
    rest_bf = jnp.concatenate(w_wide[1:], axis=0).astype(BF16)
    in_a_all, small_all = _gather_blocks(w_wide[0].astype(BF16), w_small[:SMALL_ROWS])
    wf_in_a = jnp.concatenate([in_a_all[b] for b in range(N_BLOCKS)], axis=1)
    conv_full = jnp.concatenate([small_all[b, 0, :3 * dq].reshape(3, dq) for b in range(N_BLOCKS)], axis=1)
    gain_a = jnp.concatenate([small_all[b, 0, 3 * dq:] for b in range(N_BLOCKS)])[None, :]
    gain_kv = norm_kv.reshape(1, d)
    gain_b = norm_b.reshape(1, d)
    gain_f = norm_f.reshape(1, d)

    u, rest_all = _norm_matmul_gather(xs, gain_a, wf_in_a, rest_bf, tm=512, tn=4 * d)
    o2, o3 = dq, 2 * dq
    wf_out_a = rest_all[:, :o2].reshape(d, d)
    wf_kv = rest_all[:, o2:o3].reshape(d, 2 * att)
    wf_in_b = rest_all[:, o3:o3 + dq].reshape(d, 2 * att)
    wf_out_b = jnp.concatenate([rest_all[b, o3 + dq:].reshape(att, dq) for b in range(N_BLOCKS)], axis=1)
    mix = _mixer_fwd(u, conv_full)
    x1 = _matmul(mix, wf_out_a, res=xs, name="out_proj_a")
    kv = _norm_matmul(x1, gain_kv, wf_kv, out_dtype=BF16, name="kv_proj")
    qg = _norm_matmul(x1, gain_b, wf_in_b, out_dtype=BF16, name="in_proj_b")
    o, cin, first = _attention_fwd(qg, kv)
    gated, dx2, loss_cols, g_norm_f = _out_proj_b_loss(o, qg, wf_out_b, x1, gain_f, target)
    loss = lax.psum(jnp.sum(loss_cols), ("x", "y", "c"))

    g_w_out_b = _matmul_tn(gated, dx2, name="grad_w_out_b")
    do, dg = _out_proj_b_bwd(dx2, wf_out_b, o, qg)
    dq_att, dkv = _attention_bwd(qg, kv, cin, first, do)
    dqg = jnp.concatenate([dq_att, dg], axis=1)
    g_w_in_b = _matmul_tn(x1, dqg, gain=gain_b, name="grad_w_in_b")
    g_w_kv = _matmul_tn(x1, dkv, gain=gain_kv, name="grad_w_kv")
    dx1, g_norm_b, g_norm_kv = _proj_rms_bwd(x1, [dqg, dkv], [wf_in_b, wf_kv], [gain_b, gain_kv], dx2,
                                             name="rms_bwd_b")
    g_w_out_a = _matmul_tn(mix, dx1, name="grad_w_out_a")
    dmix = _matmul(dx1, wf_out_a, trans_b=True, name="d_mix")
    du, g_conv = _mixer_bwd(u, dmix, conv_full)
    g_w_in_a = _matmul_tn(xs, du, gain=gain_a, block_cols=d, tn=2 * d, tt=512, name="grad_w_in_a")
    wide_blocks = [g_w_in_a, g_w_out_a.reshape(N_BLOCKS, dq, d), g_w_kv.reshape(N_BLOCKS, dq, d),
                   g_w_in_b.reshape(N_BLOCKS, dq, d),
                   g_w_out_b.reshape(att, N_BLOCKS, dq).transpose(1, 0, 2).reshape(N_BLOCKS, att // N_BLOCKS, d)]
    dx, g_norm_a, wide_landed = _rms_bwd_a_scatter(du, wf_in_a, xs, dx1, gain_a, wide_blocks)
    small_rows = jnp.concatenate([
        jnp.concatenate([g_conv[:3].reshape(3, N_BLOCKS, dq).transpose(1, 0, 2).reshape(N_BLOCKS, 3 * dq),
                         g_norm_a.reshape(N_BLOCKS, dq)], axis=1)[:, None, :],
        jnp.broadcast_to(g_norm_kv[None], (N_BLOCKS, 1, d)),
        jnp.broadcast_to(g_norm_b[None], (N_BLOCKS, 1, d)),
        jnp.broadcast_to(g_norm_f[None], (N_BLOCKS, 1, d)),
        jnp.zeros((N_BLOCKS, n_small - 4, d), F32)], axis=1)

    mine = _sum_slots(wide_landed, _scatter_small(small_rows))
    theirs = _swap_with_sibling(mine)
    results = [{}, {}, {}, {}]
    row0 = 0
    for name, w, m, v in zip(WIDE, w_wide, m_wide, v_wide):
        for res, val in zip(results, _adamw(mine, theirs, row0, w, m, v, name="adamw_" + name)):
            res[name] = val.reshape(shapes[name])
        row0 += w.shape[0]
    for res, val in zip(results, _adamw(mine, theirs, n_wide, w_small, m_small, v_small, name="adamw_small")):
        res.update(_unpack_small(val, shapes))
    return (loss, dx[None], *[res[n] for res in results for n in WEIGHTS])
```

```python
import jax
import jax.numpy as jnp
from jax import lax
from jax.experimental import pallas as pl
from jax.experimental.pallas import tpu as pltpu

F32 = jnp.float32
BF16 = jnp.bfloat16
MESH = pl.DeviceIdType.MESH

HEAD_DIM = 64
RMS_EPS = 1e-6
ADAM_LR = 0.001
ADAM_B1 = 0.9
ADAM_B2 = 0.999
ADAM_EPS = 1e-08
ADAM_WD = 0.01
ADAM_STEP = 10
N_BLOCKS = 4
SMALL_ROWS = 8
ROW_TILE = 256
VMEM_LIMIT_V7X = 56 * 1024 * 1024


def _params(*sem):
    return pltpu.CompilerParams(dimension_semantics=sem if sem else None, vmem_limit_bytes=VMEM_LIMIT_V7X)


def _sigmoid(x):
    return 1.0 / (1.0 + jnp.exp(-x))


def _rms_normalize(x):
    rstd = lax.rsqrt(jnp.mean(x * x, axis=-1, keepdims=True) + RMS_EPS)
    return x * rstd, rstd


def _norm_matmuls(x, gains, ws, *, name, tm=512):
    t, d = x.shape
    tm = min(tm, t)
    nk = len(ws)

    def body(*refs):
        x_ref = refs[0]
        xn, _ = _rms_normalize(x_ref[...])
        for g_ref, w_ref, o_ref in zip(refs[1:1 + nk], refs[1 + nk:1 + 2 * nk], refs[1 + 2 * nk:]):
            h = (xn * g_ref[...]).astype(BF16)
            o_ref[...] = jnp.dot(h, w_ref[...], preferred_element_type=F32).astype(o_ref.dtype)

    return pl.pallas_call(
        body, name=name, grid=(t // tm,),
        in_specs=([pl.BlockSpec((tm, d), lambda i: (i, 0))] + [pl.BlockSpec((1, d), lambda i: (0, 0))] * nk
                  + [pl.BlockSpec(w.shape, lambda i: (0, 0)) for w in ws]),
        out_specs=[pl.BlockSpec((tm, w.shape[1]), lambda i: (i, 0)) for w in ws],
        out_shape=[jax.ShapeDtypeStruct((t, w.shape[1]), BF16) for w in ws],
        compiler_params=_params("parallel"),
    )(x, *gains, *ws)


def _matmul(a, w, *, name, trans_b=False, res=None, out_dtype=F32, tm=1024, tn=1024, tk=1024):
    t, k = a.shape
    n = w.shape[0] if trans_b else w.shape[1]
    tm, tn, tk = min(tm, t), min(tn, n), min(tk, k)
    nk = k // tk

    def body(*refs):
        if res is None:
            a_ref, w_ref, o_ref, acc = refs
        else:
            a_ref, w_ref, r_ref, o_ref, acc = refs
        kk = pl.program_id(2)

        @pl.when(kk == 0)
        def _():
            acc[...] = jnp.zeros_like(acc)

        av = a_ref[...].astype(BF16)
        if trans_b:
            acc[...] += lax.dot_general(av, w_ref[...], (((1,), (1,)), ((), ())), preferred_element_type=F32)
        else:
            acc[...] += jnp.dot(av, w_ref[...], preferred_element_type=F32)

        @pl.when(kk == nk - 1)
        def _():
            r = acc[...]
            if res is not None:
                r = r + r_ref[...]
            o_ref[...] = r.astype(o_ref.dtype)

    in_specs = [pl.BlockSpec((tm, tk), lambda i, j, kk: (i, kk)),
                pl.BlockSpec((tn, tk), lambda i, j, kk: (j, kk)) if trans_b
                else pl.BlockSpec((tk, tn), lambda i, j, kk: (kk, j))]
    args = [a, w]
    if res is not None:
        in_specs.append(pl.BlockSpec((tm, tn), lambda i, j, kk: (i, j)))
        args.append(res)
    return pl.pallas_call(
        body, name=name, grid=(t // tm, n // tn, nk),
        in_specs=in_specs,
        out_specs=pl.BlockSpec((tm, tn), lambda i, j, kk: (i, j)),
        out_shape=jax.ShapeDtypeStruct((t, n), out_dtype),
        scratch_shapes=[pltpu.VMEM((tm, tn), F32)],
        compiler_params=_params("parallel", "parallel", "arbitrary"),
    )(*args)


def _matmul_tn(a, b, *, name, gain=None, block_cols=None, tt=1024, tn=1024):
    t, k = a.shape
    n = b.shape[1]
    tt, tn = min(tt, t), min(tn, n)
    nt = t // tt
    per_step = tn // block_cols if block_cols else 1

    def body(*refs):
        if gain is None:
            a_ref, b_ref, o_ref, acc = refs
            av = a_ref[...].astype(BF16)
        else:
            a_ref, g_ref, b_ref, o_ref, acc = refs
            xn, _ = _rms_normalize(a_ref[...])
            av = (xn * g_ref[...]).astype(BF16)

        @pl.when(pl.program_id(1) == 0)
        def _():
            acc[...] = jnp.zeros_like(acc)

        acc[...] += lax.dot_general(av, b_ref[...].astype(BF16), (((0,), (0,)), ((), ())),
                                    preferred_element_type=F32)

        @pl.when(pl.program_id(1) == nt - 1)
        def _():
            if block_cols:
                for c in range(per_step):
                    o_ref[c] = acc[:, c * block_cols:(c + 1) * block_cols].astype(o_ref.dtype)
            else:
                o_ref[...] = acc[...].astype(o_ref.dtype)

    in_specs = [pl.BlockSpec((tt, k), lambda j, s: (s, 0))]
    args = [a]
    if gain is not None:
        in_specs.append(pl.BlockSpec((1, k), lambda j, s: (0, 0)))
        args.append(gain)
    in_specs.append(pl.BlockSpec((tt, tn), lambda j, s: (s, j)))
    args.append(b)
    if block_cols:
        out_spec = pl.BlockSpec((per_step, k, block_cols), lambda j, s: (j, 0, 0))
        out_shape = jax.ShapeDtypeStruct((n // block_cols, k, block_cols), BF16)
    else:
        out_spec = pl.BlockSpec((k, tn), lambda j, s: (0, j))
        out_shape = jax.ShapeDtypeStruct((k, n), BF16)
    return pl.pallas_call(
        body, name=name, grid=(n // tn, nt),
        in_specs=in_specs, out_specs=out_spec, out_shape=out_shape,
        scratch_shapes=[pltpu.VMEM((k, tn), F32)],
        compiler_params=_params("parallel", "arbitrary"),
    )(*args)


HALO = 16


def _shift_down(v, halo, k):
    rows = lax.broadcasted_iota(jnp.int32, v.shape, 0)
    out = pltpu.roll(v, k, 0)
    for r in range(k):
        out = jnp.where(rows == r, halo[HALO - k + r:HALO - k + r + 1, :], out)
    return out


def _shift_up(v, halo, k):
    n = v.shape[0]
    rows = lax.broadcasted_iota(jnp.int32, v.shape, 0)
    out = pltpu.roll(v, n - k, 0)
    for r in range(k):
        out = jnp.where(rows == n - k + r, halo[r:r + 1, :], out)
    return out


def _mixer_fwd(u, conv, *, tm=256):
    t, d4 = u.shape
    d = d4 // 4
    tm = min(tm, t)
    hb = tm // HALO

    def body(b_ref, c_ref, x_ref, g_ref, ch_ref, xh_ref, w_ref, m_ref):
        i = pl.program_id(0)
        y1 = c_ref[...].astype(F32) * x_ref[...].astype(F32)
        prev = ch_ref[...].astype(F32) * xh_ref[...].astype(F32)
        prev = jnp.where(i == 0, 0.0, prev)
        w = w_ref[...]
        yc = w[2:3, :] * y1 + w[1:2, :] * _shift_down(y1, prev, 1) + w[0:1, :] * _shift_down(y1, prev, 2)
        g = g_ref[...].astype(F32)
        m_ref[...] = (b_ref[...].astype(F32) * yc * (g * _sigmoid(g))).astype(m_ref.dtype)

    def col(c):
        return pl.BlockSpec((tm, d), lambda i: (i, c))

    def prev_rows(c):
        return pl.BlockSpec((HALO, d), lambda i: (jnp.maximum(i * hb - 1, 0), c))

    return pl.pallas_call(
        body, name="mixer_fwd", grid=(t // tm,),
        in_specs=[col(0), col(1), col(2), col(3), prev_rows(1), prev_rows(2), pl.BlockSpec((3, d), lambda i: (0, 0))],
        out_specs=pl.BlockSpec((tm, d), lambda i: (i, 0)),
        out_shape=jax.ShapeDtypeStruct((t, d), BF16),
        compiler_params=_params("parallel"),
    )(u, u, u, u, u, u, conv)


def _mixer_bwd(u, dm, conv, *, tm=256):
    t, d4 = u.shape
    d = d4 // 4
    tm = min(tm, t)
    hb = tm // HALO
    nb = t // tm

    def body(b_ref, c_ref, x_ref, g_ref, dm_ref, ch_ref, xh_ref, bn_ref, gn_ref, dmn_ref, w_ref, du_ref, gw_ref):
        i = pl.program_id(0)
        w = w_ref[...]
        b = b_ref[...].astype(F32)
        c = c_ref[...].astype(F32)
        xin = x_ref[...].astype(F32)
        g = g_ref[...].astype(F32)
        dm_v = dm_ref[...]
        y1 = c * xin
        prev = jnp.where(i == 0, 0.0, ch_ref[...].astype(F32) * xh_ref[...].astype(F32))
        y1m1 = _shift_down(y1, prev, 1)
        y1m2 = _shift_down(y1, prev, 2)
        yc = w[2:3, :] * y1 + w[1:2, :] * y1m1 + w[0:1, :] * y1m2
        sg = _sigmoid(g)
        s = g * sg
        ds = sg * (1.0 + g * (1.0 - sg))
        dyc = dm_v * b * s
        gn = gn_ref[...].astype(F32)
        nxt = dmn_ref[...] * bn_ref[...].astype(F32) * (gn * _sigmoid(gn))
        nxt = jnp.where(i == nb - 1, 0.0, nxt)
        dy1 = w[2:3, :] * dyc + w[1:2, :] * _shift_up(dyc, nxt, 1) + w[0:1, :] * _shift_up(dyc, nxt, 2)
        du_ref[:, 0:d] = (dm_v * yc * s).astype(du_ref.dtype)
        du_ref[:, d:2 * d] = (dy1 * xin).astype(du_ref.dtype)
        du_ref[:, 2 * d:3 * d] = (dy1 * c).astype(du_ref.dtype)
        du_ref[:, 3 * d:4 * d] = (dm_v * b * yc * ds).astype(du_ref.dtype)

        @pl.when(i == 0)
        def _():
            gw_ref[...] = jnp.zeros_like(gw_ref)

        gw_ref[0:1, :] += jnp.sum(dyc * y1m2, axis=0, keepdims=True)
        gw_ref[1:2, :] += jnp.sum(dyc * y1m1, axis=0, keepdims=True)
        gw_ref[2:3, :] += jnp.sum(dyc * y1, axis=0, keepdims=True)

    def col(c):
        return pl.BlockSpec((tm, d), lambda i: (i, c))

    def prev_rows(c):
        return pl.BlockSpec((HALO, d), lambda i: (jnp.maximum(i * hb - 1, 0), c))

    def next_rows(c):
        return pl.BlockSpec((HALO, d), lambda i: (jnp.minimum((i + 1) * hb, nb * hb - 1), c))

    return pl.pallas_call(
        body, name="mixer_bwd", grid=(nb,),
        in_specs=[col(0), col(1), col(2), col(3), pl.BlockSpec((tm, d), lambda i: (i, 0)),
                  prev_rows(1), prev_rows(2), next_rows(0), next_rows(3),
                  pl.BlockSpec((HALO, d), lambda i: (jnp.minimum((i + 1) * hb, nb * hb - 1), 0)),
                  pl.BlockSpec((3, d), lambda i: (0, 0))],
        out_specs=[pl.BlockSpec((tm, d4), lambda i: (i, 0)), pl.BlockSpec((SMALL_ROWS, d), lambda i: (0, 0))],
        out_shape=[jax.ShapeDtypeStruct((t, d4), BF16), jax.ShapeDtypeStruct((SMALL_ROWS, d), F32)],
        compiler_params=_params("arbitrary"),
    )(u, u, u, u, dm, u, u, u, u, dm, conv)


def _out_proj_b_loss(o, qg, w_out, x1, gain, target, *, tm=512):
    t, a = o.shape
    d = x1.shape[1]
    tm = min(tm, t)

    def body(o_ref, g_ref, w_ref, x_ref, gain_ref, y_ref, gated_ref, dx_ref, loss_ref, gg_ref):
        @pl.when(pl.program_id(0) == 0)
        def _():
            loss_ref[...] = jnp.zeros_like(loss_ref)
            gg_ref[...] = jnp.zeros_like(gg_ref)

        g = g_ref[...].astype(F32)
        gated = (o_ref[...] * (g * _sigmoid(g))).astype(BF16)
        gated_ref[...] = gated
        x2 = x_ref[...] + jnp.dot(gated, w_ref[...], preferred_element_type=F32)
        xn, rstd = _rms_normalize(x2)
        gv = gain_ref[...]
        err = xn * gv - y_ref[...]
        loss_ref[...] += jnp.sum(err * err, axis=0, keepdims=True) * (0.5 / d)
        dy = err * (1.0 / d)
        gg_ref[...] += jnp.sum(dy * xn, axis=0, keepdims=True)
        dxn = dy * gv
        dx_ref[...] = rstd * (dxn - xn * jnp.mean(dxn * xn, axis=-1, keepdims=True))

    blk = pl.BlockSpec((tm, d), lambda i: (i, 0))
    blk_a = pl.BlockSpec((tm, a), lambda i: (i, 0))
    row = pl.BlockSpec((1, d), lambda i: (0, 0))
    return pl.pallas_call(
        body, name="out_proj_b_loss", grid=(t // tm,),
        in_specs=[blk_a, pl.BlockSpec((tm, a), lambda i: (i, 1)), pl.BlockSpec(w_out.shape, lambda i: (0, 0)),
                  blk, row, blk],
        out_specs=[blk_a, blk, row, row],
        out_shape=[jax.ShapeDtypeStruct((t, a), BF16), jax.ShapeDtypeStruct((t, d), F32),
                   jax.ShapeDtypeStruct((1, d), F32), jax.ShapeDtypeStruct((1, d), F32)],
        compiler_params=_params("arbitrary"),
    )(o, qg, w_out, x1, gain, target)


def _out_proj_b_bwd(dx2, w_out, o, qg, *, tm=512):
    t, a = o.shape
    d = dx2.shape[1]
    tm = min(tm, t)

    def body(dx_ref, w_ref, o_ref, g_ref, do_ref, dg_ref):
        da = _nt(dx_ref[...].astype(BF16), w_ref[...])
        g = g_ref[...].astype(F32)
        sg = _sigmoid(g)
        do_ref[...] = da * (g * sg)
        dg_ref[...] = (da * o_ref[...] * (sg * (1.0 + g * (1.0 - sg)))).astype(dg_ref.dtype)

    blk_a = pl.BlockSpec((tm, a), lambda i: (i, 0))
    return pl.pallas_call(
        body, name="out_proj_b_bwd", grid=(t // tm,),
        in_specs=[pl.BlockSpec((tm, d), lambda i: (i, 0)), pl.BlockSpec(w_out.shape, lambda i: (0, 0)), blk_a,
                  pl.BlockSpec((tm, a), lambda i: (i, 1))],
        out_specs=[blk_a, blk_a],
        out_shape=[jax.ShapeDtypeStruct((t, a), F32), jax.ShapeDtypeStruct((t, a), BF16)],
        compiler_params=_params("parallel"),
    )(dx2, w_out, o, qg)


def _proj_rms_bwd(x, cots, ws, gains, dres, *, name, tm=512):
    t, d = x.shape
    tm = min(tm, t)
    nk = len(cots)

    def body(*refs):
        x_ref = refs[0]
        cot_refs = refs[1:1 + nk]
        w_refs = refs[1 + nk:1 + 2 * nk]
        g_refs = refs[1 + 2 * nk:1 + 3 * nk]
        dres_ref = refs[1 + 3 * nk]
        dx_ref = refs[2 + 3 * nk]
        gg_refs = refs[3 + 3 * nk:]

        @pl.when(pl.program_id(0) == 0)
        def _():
            for gg in gg_refs:
                gg[...] = jnp.zeros_like(gg)

        xn, rstd = _rms_normalize(x_ref[...])
        dxn = None
        for cot_ref, w_ref, g_ref, gg in zip(cot_refs, w_refs, g_refs, gg_refs):
            dh = _nt(cot_ref[...].astype(BF16), w_ref[...])
            gg[...] += jnp.sum(dh * xn, axis=0, keepdims=True)
            term = dh * g_ref[...]
            dxn = term if dxn is None else dxn + term
        dx_ref[...] = dres_ref[...] + rstd * (dxn - xn * jnp.mean(dxn * xn, axis=-1, keepdims=True))

    blk = pl.BlockSpec((tm, d), lambda i: (i, 0))
    row = pl.BlockSpec((1, d), lambda i: (0, 0))
    return pl.pallas_call(
        body, name=name, grid=(t // tm,),
        in_specs=([blk] + [pl.BlockSpec((tm, c.shape[1]), lambda i: (i, 0)) for c in cots]
                  + [pl.BlockSpec(w.shape, lambda i: (0, 0)) for w in ws] + [row] * nk + [blk]),
        out_specs=[blk] + [row] * nk,
        out_shape=[jax.ShapeDtypeStruct((t, d), F32)] + [jax.ShapeDtypeStruct((1, d), F32)] * nk,
        compiler_params=_params("arbitrary"),
    )(x, *cots, *ws, *gains, dres)


def _suffix_ones(n):
    r = lax.broadcasted_iota(jnp.int32, (n, n), 0)
    c = lax.broadcasted_iota(jnp.int32, (n, n), 1)
    return (r >= c).astype(BF16)


def _suffix_sum(l, ones):
    return jnp.dot(l.astype(BF16), ones, preferred_element_type=F32)


def _log_one_minus_beta(z):
    return -(jnp.maximum(z, 0.0) + jnp.log(1.0 + jnp.exp(-jnp.abs(z))))


def _nt(a, b):
    return lax.dot_general(a, b, (((1,), (1,)), ((), ())), preferred_element_type=F32)


def _tn(a, b):
    return lax.dot_general(a, b, (((0,), (0,)), ((), ())), preferred_element_type=F32)


EXP_UNDERFLOW = 104.0
NORM_MARGIN = 1.01


def _key_norm_bounds(kv, *, tq):
    t, a2 = kv.shape
    a = a2 // 2
    heads = a // HEAD_DIM
    nq = t // tq

    per_step = min(4, nq)
    lanes = 128

    def body(k_ref, sel_ref, o_ref):
        k = k_ref[...].astype(F32)
        n2 = jnp.dot((k * k).astype(BF16), sel_ref[...], preferred_element_type=F32)
        o_ref[...] = jnp.sqrt(jnp.max(n2.reshape(per_step, tq, lanes), axis=1))

    head_of_col = lax.broadcasted_iota(jnp.int32, (a, lanes), 0) // HEAD_DIM
    sel = (head_of_col == lax.broadcasted_iota(jnp.int32, (a, lanes), 1)).astype(BF16)
    norms = pl.pallas_call(
        body, name="key_norms", grid=(nq // per_step,),
        in_specs=[pl.BlockSpec((per_step * tq, a), lambda i: (i, 0)), pl.BlockSpec((a, lanes), lambda i: (0, 0))],
        out_specs=pl.BlockSpec((None, per_step, lanes), lambda i: (i, 0, 0)),
        out_shape=jax.ShapeDtypeStruct((nq // per_step, per_step, lanes), F32),
        compiler_params=_params("parallel"),
    )(kv, sel)
    return lax.cummax(norms.reshape(nq, lanes)[:, :heads], axis=0).T.reshape(-1)


def _attention_fwd(qg, kv, *, tq=256, n_sub=1, hg=4):
    t, a2 = qg.shape
    a = a2 // 2
    hg = hg if a % (hg * HEAD_DIM) == 0 else 2
    gw = hg * HEAD_DIM
    ngroup = a // gw
    tq = min(tq, t)
    scale = HEAD_DIM ** -0.5
    nq = t // tq
    ts = tq // n_sub

    def body(kmax_ref, q_ref, k_ref, v_ref, ones_ref, o_ref, cin_ref, first_ref):
        p = pl.program_id(0)
        i = pl.program_id(1)
        ones = ones_ref[...]
        q_all = q_ref[...] * jnp.asarray(scale, BF16)
        q2 = q_all.astype(F32) * q_all.astype(F32)
        chains = [(h, r) for h in range(hg) for r in range(n_sub)]
        qs = [q_all[r * ts:(r + 1) * ts, h * HEAD_DIM:(h + 1) * HEAD_DIM] for h, r in chains]
        q_norm = [jnp.sqrt(jnp.sum(q2[r * ts:(r + 1) * ts, h * HEAD_DIM:(h + 1) * HEAD_DIM], axis=1, keepdims=True))
                  * NORM_MARGIN for h, r in chains]
        rows = lax.broadcasted_iota(jnp.int32, (ts, tq), 0)
        cols = lax.broadcasted_iota(jnp.int32, (ts, tq), 1)
        causal = [cols < rows + r * ts for r in range(n_sub)]
        block_of_lane = lax.broadcasted_iota(jnp.int32, (ts, nq), 1)
        cin_ref[...] = jnp.zeros_like(cin_ref)

        def any_weight_left_of(j, carry):
            jj = jnp.maximum(j - 1, 0)
            bound = None
            for n, (h, r) in enumerate(chains):
                top = jnp.max(carry[n][0] + q_norm[n] * kmax_ref[(hg * p + h) * nq + jj])
                bound = top if bound is None else jnp.maximum(bound, top)
            return bound > -EXP_UNDERFLOW

        def tile(j, carry, masked):
            start = pl.multiple_of(j * tq, tq)
            k_all = k_ref[pl.ds(start, tq), :]
            v_all = v_ref[pl.ds(start, tq), :]
            ks = [k_all[:, h * HEAD_DIM:(h + 1) * HEAD_DIM] for h in range(hg)]
            vs = [v_all[:, h * HEAD_DIM:(h + 1) * HEAD_DIM] for h in range(hg)]
            zs = [_nt(qs[n], ks[h]) for n, (h, r) in enumerate(chains)]
            ls = [_log_one_minus_beta(z) for z in zs]
            if masked:
                ls = [jnp.where(causal[r], l, 0.0) for l, (h, r) in zip(ls, chains)]
            r_locs = [_suffix_sum(l, ones) for l in ls]
            logws = [z + (carry[n][0] + r_loc) for n, (z, r_loc) in enumerate(zip(zs, r_locs))]
            if masked:
                logws = [jnp.where(causal[r], lw, -jnp.inf) for lw, (h, r) in zip(logws, chains)]
            ws = [jnp.exp(lw).astype(BF16) for lw in logws]
            out = []
            for n, (h, r) in enumerate(chains):
                acc = carry[n][1] + jnp.dot(ws[n], vs[h], preferred_element_type=F32)
                c = carry[n][0] + r_locs[n][:, 0:1]
                cin_ref[h, r * ts:(r + 1) * ts, :] = jnp.where(block_of_lane == j - 1, c,
                                                               cin_ref[h, r * ts:(r + 1) * ts, :])
                out.append((c, acc))
            return tuple(out)

        init = tuple((jnp.zeros((ts, 1), F32), jnp.zeros((ts, HEAD_DIM), F32)) for _ in chains)
        carry = tile(i, init, True)

        def visit(state):
            j, _, cr = state
            cr = tile(j - 1, cr, False)
            return j - 1, any_weight_left_of(j - 1, cr), cr

        first, _, carry = lax.while_loop(lambda s: (s[0] > 0) & s[1], visit,
                                         (i, any_weight_left_of(i, carry), carry))
        first_ref[p * nq + i] = first
        heads = [jnp.concatenate([carry[h * n_sub + r][1] for r in range(n_sub)], axis=0) for h in range(hg)]
        o_ref[...] = jnp.concatenate(heads, axis=1)

    return pl.pallas_call(
        body, name="attention_fwd",
        grid_spec=pltpu.PrefetchScalarGridSpec(
            num_scalar_prefetch=1, grid=(ngroup, nq),
            in_specs=[pl.BlockSpec((tq, gw), lambda p, i, km: (i, p)),
                      pl.BlockSpec((t, gw), lambda p, i, km: (0, p)),
                      pl.BlockSpec((t, gw), lambda p, i, km: (0, ngroup + p)),
                      pl.BlockSpec((tq, tq), lambda p, i, km: (0, 0))],
            out_specs=[pl.BlockSpec((tq, gw), lambda p, i, km: (i, p)),
                       pl.BlockSpec((None, hg, tq, nq), lambda p, i, km: (p, 0, i, 0)),
                       pl.BlockSpec(memory_space=pltpu.SMEM)]),
        out_shape=[jax.ShapeDtypeStruct((t, a), F32), jax.ShapeDtypeStruct((ngroup, hg, t, nq), F32),
                   jax.ShapeDtypeStruct((ngroup * nq,), jnp.int32)],
        compiler_params=_params("arbitrary", "arbitrary"),
    )(_key_norm_bounds(kv, tq=tq), qg, kv, kv, _suffix_ones(tq))


def _attention_bwd(qg, kv, cin, first, do, *, tq=256, hg=4):
    t, a2 = qg.shape
    a = a2 // 2
    hg = hg if a % (hg * HEAD_DIM) == 0 else 2
    gw = hg * HEAD_DIM
    ngroup = a // gw
    tq = min(tq, t)
    nq = t // tq
    scale = HEAD_DIM ** -0.5

    def body(first_ref, q_ref, kv_ref, cin_ref, do_ref, ones_ref, ones_t_ref, dq_ref, dkv_ref,
             k_ref, v_ref, dk_acc, dv_acc, sem):
        p = pl.program_id(0)
        i = pl.program_id(1)
        first = jnp.clip(first_ref[p * nq + i], 0, i)
        k_cols = pl.multiple_of(p * gw, gw)
        v_cols = pl.multiple_of((ngroup + p) * gw, gw)

        @pl.when(i == 0)
        def _():
            loads = [pltpu.make_async_copy(kv_ref.at[:, pl.ds(k_cols, gw)], k_ref, sem.at[0]),
                     pltpu.make_async_copy(kv_ref.at[:, pl.ds(v_cols, gw)], v_ref, sem.at[1])]
            for cp in loads:
                cp.start()
            dk_acc[...] = jnp.zeros_like(dk_acc)
            dv_acc[...] = jnp.zeros_like(dv_acc)
            for cp in loads:
                cp.wait()

        ones = ones_ref[...]
        ones_t = ones_t_ref[...]
        q_all = q_ref[...] * jnp.asarray(scale, BF16)
        do_bf = do_ref[...].astype(BF16)
        qs = [q_all[:, h * HEAD_DIM:(h + 1) * HEAD_DIM] for h in range(hg)]
        dos = [do_bf[:, h * HEAD_DIM:(h + 1) * HEAD_DIM] for h in range(hg)]
        rows = lax.broadcasted_iota(jnp.int32, (tq, tq), 0)
        cols = lax.broadcasted_iota(jnp.int32, (tq, tq), 1)
        causal = cols < rows
        block_of_lane = lax.broadcasted_iota(jnp.int32, (tq, nq), 1)

        def tile(j, carry, masked):
            start = pl.multiple_of(j * tq, tq)
            k_all = k_ref[pl.ds(start, tq), :]
            v_all = v_ref[pl.ds(start, tq), :]
            hs = range(hg)
            ks = [k_all[:, h * HEAD_DIM:(h + 1) * HEAD_DIM] for h in hs]
            vs = [v_all[:, h * HEAD_DIM:(h + 1) * HEAD_DIM] for h in hs]
            zs = [_nt(qs[h], ks[h]) for h in hs]
            das = [_nt(dos[h], vs[h]) for h in hs]
            ls = [_log_one_minus_beta(z) for z in zs]
            if masked:
                ls = [jnp.where(causal, l, 0.0) for l in ls]
            betas = [jnp.exp(z + l) for z, l in zip(zs, ls)]
            r_locs = [_suffix_sum(l, ones) for l in ls]
            if masked:
                logws = [jnp.where(causal, z + r_loc, -jnp.inf) for z, r_loc in zip(zs, r_locs)]
            else:
                cs = [jnp.sum(jnp.where(block_of_lane == j, cin_ref[h], 0.0), axis=1, keepdims=True) for h in hs]
                logws = [z + (c + r_loc) for z, c, r_loc in zip(zs, cs, r_locs)]
            ws = [jnp.exp(lw) for lw in logws]
            gs = [w * da for w, da in zip(ws, das)]
            g_pres = [jnp.dot(g.astype(BF16), ones_t, preferred_element_type=F32) for g in gs]
            dvs = [_tn(ws[h].astype(BF16), dos[h]) for h in hs]
            dzs = [gs[h] - betas[h] * (carry[h][0] + g_pres[h]) for h in hs]
            if masked:
                dzs = [jnp.where(causal, dz, 0.0) for dz in dzs]
            dzs = [dz.astype(BF16) for dz in dzs]
            dqs = [carry[h][1] + jnp.dot(dzs[h], ks[h], preferred_element_type=F32) for h in hs]
            dks = [_tn(dzs[h], qs[h]) for h in hs]
            dk_acc[pl.ds(start, tq), :] += jnp.concatenate(dks, axis=1)
            dv_acc[pl.ds(start, tq), :] += jnp.concatenate(dvs, axis=1)
            return tuple((carry[h][0] + g_pres[h][:, tq - 1:tq], dqs[h]) for h in hs)

        init = tuple((jnp.zeros((tq, 1), F32), jnp.zeros((tq, HEAD_DIM), F32)) for _ in range(hg))
        carry = lax.fori_loop(first, i, lambda j, cr: tile(j, cr, False), init)
        carry = tile(i, carry, True)
        dq_ref[...] = (jnp.concatenate([carry[h][1] for h in range(hg)], axis=1) * scale).astype(dq_ref.dtype)

        @pl.when(i == nq - 1)
        def _():
            ck = pltpu.make_async_copy(dk_acc, dkv_ref.at[:, pl.ds(k_cols, gw)], sem.at[0])
            cv = pltpu.make_async_copy(dv_acc, dkv_ref.at[:, pl.ds(v_cols, gw)], sem.at[1])
            ck.start()
            cv.start()
            ck.wait()
            cv.wait()

    blk = pl.BlockSpec((tq, gw), lambda p, i, fr: (i, p))
    tri = pl.BlockSpec((tq, tq), lambda p, i, fr: (0, 0))
    ones = _suffix_ones(tq)
    return pl.pallas_call(
        body, name="attention_bwd",
        grid_spec=pltpu.PrefetchScalarGridSpec(
            num_scalar_prefetch=1, grid=(ngroup, nq),
            in_specs=[blk, pl.BlockSpec(memory_space=pl.ANY),
                      pl.BlockSpec((None, hg, tq, nq), lambda p, i, fr: (p, 0, i, 0)),
                      blk, tri, tri],
            out_specs=[blk, pl.BlockSpec(memory_space=pl.ANY)],
            scratch_shapes=[pltpu.VMEM((t, gw), BF16), pltpu.VMEM((t, gw), BF16),
                            pltpu.VMEM((t, gw), F32), pltpu.VMEM((t, gw), F32),
                            pltpu.SemaphoreType.DMA((2,))]),
        out_shape=[jax.ShapeDtypeStruct((t, a), BF16), jax.ShapeDtypeStruct((t, a2), F32)],
        compiler_params=_params("arbitrary", "arbitrary"),
    )(first, qg, kv, cin, do, ones, ones.T)


def _other_chips(x, y):
    return [(1 - x, y), (x, 1 - y), (1 - x, 1 - y)]


def _gather_blocks(wide, small):
    def body(w_ref, s_ref, wo_ref, so_ref, send_sems, recv_sems, local_sems):
        srcs, outs = (w_ref, s_ref), (wo_ref, so_ref)
        _run_exchange(*_block_exchange(2, lambda k, slot: srcs[k], lambda k, slot: outs[k].at[slot],
                                       send_sems, recv_sems, local_sems))

    any_spec = pl.BlockSpec(memory_space=pl.ANY)
    return pl.pallas_call(
        body, name="gather_weights",
        in_specs=[any_spec, any_spec], out_specs=[any_spec, any_spec],
        out_shape=[jax.ShapeDtypeStruct((N_BLOCKS,) + wide.shape, wide.dtype),
                   jax.ShapeDtypeStruct((N_BLOCKS,) + small.shape, small.dtype)],
        scratch_shapes=_exchange_sems(2),
    )(wide, small)


def _norm_matmul_gather(x, gain, w, rest, *, tm=1024, tn=1024):
    t, d = x.shape
    n = w.shape[1]
    tm, tn = min(tm, t), min(tn, n)
    ni, nj = t // tm, n // tn

    def body(x_ref, g_ref, w_ref, rest_ref, o_ref, all_ref, send_sems, recv_sems, local_sems):
        i, j = pl.program_id(0), pl.program_id(1)
        copies = _block_exchange(1, lambda k, slot: rest_ref, lambda k, slot: all_ref.at[slot],
                                 send_sems, recv_sems, local_sems)

        @pl.when((i == 0) & (j == 0))
        def _():
            for cp in copies[0] + copies[1]:
                cp.start()

        xn, _ = _rms_normalize(x_ref[...])
        h = (xn * g_ref[...]).astype(BF16)
        o_ref[...] = jnp.dot(h, w_ref[...], preferred_element_type=F32).astype(o_ref.dtype)

        @pl.when((i == ni - 1) & (j == nj - 1))
        def _():
            _finish_exchange(*copies)

    any_spec = pl.BlockSpec(memory_space=pl.ANY)
    return pl.pallas_call(
        body, name="in_proj_a_gather", grid=(ni, nj),
        in_specs=[pl.BlockSpec((tm, d), lambda i, j: (i, 0)),
                  pl.BlockSpec((1, d), lambda i, j: (0, 0)),
                  pl.BlockSpec((d, tn), lambda i, j: (0, j)), any_spec],
        out_specs=[pl.BlockSpec((tm, tn), lambda i, j: (i, j)), any_spec],
        out_shape=[jax.ShapeDtypeStruct((t, n), BF16), jax.ShapeDtypeStruct((N_BLOCKS,) + rest.shape, rest.dtype)],
        scratch_shapes=_exchange_sems(1),
        compiler_params=_params("arbitrary", "arbitrary"),
    )(x, gain, w, rest)


def _block_exchange(n, source, landing, send_sems, recv_sems, local_sems):
    x, y, c = lax.axis_index("x"), lax.axis_index("y"), lax.axis_index("c")
    mine = 2 * x + y
    local = [pltpu.make_async_copy(source(k, mine), landing(k, mine), local_sems.at[k]) for k in range(n)]
    sends, recvs = [], []
    for p, (px, py) in enumerate(_other_chips(x, y)):
        for k in range(n):
            sems = dict(send_sem=send_sems.at[p * n + k], recv_sem=recv_sems.at[p * n + k],
                        device_id=(px, py, c), device_id_type=MESH)
            sends.append(pltpu.make_async_remote_copy(src_ref=source(k, 2 * px + py), dst_ref=landing(k, mine),
                                                      **sems))
            recvs.append(pltpu.make_async_remote_copy(src_ref=source(k, mine), dst_ref=landing(k, 2 * px + py),
                                                      **sems))
    return local, sends, recvs


def _run_exchange(local, sends, recvs):
    for cp in local + sends:
        cp.start()
    _finish_exchange(local, sends, recvs)


def _finish_exchange(local, sends, recvs):
    for cp in recvs:
        cp.wait_recv()
    for cp in sends:
        cp.wait_send()
    for cp in local:
        cp.wait()


def _exchange_sems(n):
    return [pltpu.SemaphoreType.DMA((3 * n,)), pltpu.SemaphoreType.DMA((3 * n,)), pltpu.SemaphoreType.DMA((n,))]


def _scatter_small(small):
    def body(s_ref, out_ref, send_sems, recv_sems, local_sems):
        _run_exchange(*_block_exchange(1, lambda k, slot: s_ref.at[slot], lambda k, slot: out_ref.at[slot],
                                       send_sems, recv_sems, local_sems))

    any_spec = pl.BlockSpec(memory_space=pl.ANY)
    return pl.pallas_call(
        body, name="scatter_small",
        in_specs=[any_spec], out_specs=any_spec,
        out_shape=jax.ShapeDtypeStruct(small.shape, F32),
        scratch_shapes=_exchange_sems(1),
    )(small)


def _rms_bwd_a_scatter(du, w_in, x, dx1, gain, wide, *, tm=512):
    t, d = x.shape
    tm = min(tm, t)
    nb = t // tm
    n = len(wide)
    offsets = [sum(w.shape[1] for w in wide[:k]) for k in range(n)]
    n_wide = sum(w.shape[1] for w in wide)

    def body(*refs):
        du_ref, w_ref, x_ref, dx1_ref, g_ref = refs[:5]
        wide_refs = refs[5:5 + n]
        dx_ref, gg_ref, land_ref, send_sems, recv_sems, local_sems = refs[5 + n:]
        i = pl.program_id(0)
        copies = _block_exchange(
            n, lambda k, slot: wide_refs[k].at[slot],
            lambda k, slot: land_ref.at[slot, pl.ds(offsets[k], wide[k].shape[1])],
            send_sems, recv_sems, local_sems)

        @pl.when(i == 0)
        def _():
            gg_ref[...] = jnp.zeros_like(gg_ref)
            for cp in copies[0] + copies[1]:
                cp.start()

        dh = _nt(du_ref[...], w_ref[...])
        xn, rstd = _rms_normalize(x_ref[...])
        gg_ref[...] += jnp.sum(dh * xn, axis=0, keepdims=True)
        dxn = dh * g_ref[...]
        dx_ref[...] = dx1_ref[...] + rstd * (dxn - xn * jnp.mean(dxn * xn, axis=-1, keepdims=True))

        @pl.when(i == nb - 1)
        def _():
            _finish_exchange(*copies)

    blk = pl.BlockSpec((tm, d), lambda i: (i, 0))
    row = pl.BlockSpec((1, d), lambda i: (0, 0))
    any_spec = pl.BlockSpec(memory_space=pl.ANY)
    return pl.pallas_call(
        body, name="rms_bwd_a_scatter", grid=(nb,),
        in_specs=[pl.BlockSpec((tm, du.shape[1]), lambda i: (i, 0)),
                  pl.BlockSpec(w_in.shape, lambda i: (0, 0)), blk, blk, row] + [any_spec] * n,
        out_specs=[blk, row, any_spec],
        out_shape=[jax.ShapeDtypeStruct((t, d), F32), jax.ShapeDtypeStruct((1, d), F32),
                   jax.ShapeDtypeStruct((N_BLOCKS, n_wide, d), BF16)],
        scratch_shapes=_exchange_sems(n),
        compiler_params=_params("arbitrary"),
    )(du, w_in, x, dx1, gain, *wide)


def _swap_with_sibling(part):
    def body(p_ref, out_ref, send_sem, recv_sem):
        x, y, c = lax.axis_index("x"), lax.axis_index("y"), lax.axis_index("c")
        cp = pltpu.make_async_remote_copy(src_ref=p_ref, dst_ref=out_ref, send_sem=send_sem, recv_sem=recv_sem,
                                          device_id=(x, y, 1 - c), device_id_type=MESH)
        cp.start()
        cp.wait()

    any_spec = pl.BlockSpec(memory_space=pl.ANY)
    return pl.pallas_call(
        body, name="swap_sibling",
        in_specs=[any_spec], out_specs=any_spec,
        out_shape=jax.ShapeDtypeStruct(part.shape, part.dtype),
        scratch_shapes=[pltpu.SemaphoreType.DMA, pltpu.SemaphoreType.DMA],
    )(part)


def _sum_slots(wide, small):
    _, n_wide, d = wide.shape
    tr = small.shape[1]
    nw = n_wide // tr

    def body(w_ref, s_ref, o_ref):
        i = pl.program_id(0)

        @pl.when(i < nw)
        def _():
            w = w_ref[...].astype(F32)
            o_ref[...] = ((w[0] + w[1]) + w[2]) + w[3]

        @pl.when(i == nw)
        def _():
            o_ref[...] = ((s_ref[0] + s_ref[1]) + s_ref[2]) + s_ref[3]

    return pl.pallas_call(
        body, name="sum_slots", grid=(nw + 1,),
        in_specs=[pl.BlockSpec((N_BLOCKS, tr, d), lambda i: (0, jnp.minimum(i, nw - 1), 0)),
                  pl.BlockSpec((N_BLOCKS, tr, d), lambda i: (0, 0, 0))],
        out_specs=pl.BlockSpec((tr, d), lambda i: (i, 0)),
        out_shape=jax.ShapeDtypeStruct((n_wide + tr, d), F32),
        compiler_params=_params("arbitrary"),
    )(wide, small)


def _adamw(part_mine, part_sibling, row0, w, m, v, *, name):
    r, d = w.shape
    tr = next(c for c in (ROW_TILE, 128, 64, 32, 16, 8) if r % c == 0 and row0 % c == 0)
    first_block = row0 // tr
    m_scale = 1.0 / (1.0 - ADAM_B1 ** ADAM_STEP)
    v_scale = 1.0 / (1.0 - ADAM_B2 ** ADAM_STEP)

    def body(a_ref, b_ref, w_ref, m_ref, v_ref, g_ref, d_ref, mo_ref, vo_ref):
        g = a_ref[...] + b_ref[...]
        m_new = ADAM_B1 * m_ref[...] + (1.0 - ADAM_B1) * g
        v_new = ADAM_B2 * v_ref[...] + (1.0 - ADAM_B2) * (g * g)
        g_ref[...] = g
        mo_ref[...] = m_new
        vo_ref[...] = v_new
        d_ref[...] = -ADAM_LR * ((m_new * m_scale) / (jnp.sqrt(v_new * v_scale) + ADAM_EPS) + ADAM_WD * w_ref[...])

    blk = pl.BlockSpec((tr, d), lambda i: (i, 0))
    part = pl.BlockSpec((tr, d), lambda i: (first_block + i, 0))
    return pl.pallas_call(
        body, name=name, grid=(r // tr,),
        in_specs=[part, part, blk, blk, blk], out_specs=[blk] * 4,
        out_shape=[jax.ShapeDtypeStruct((r, d), F32)] * 4,
        compiler_params=_params("parallel"),
    )(part_mine, part_sibling, w, m, v)


def _wide_views(w_in_a, w_out_a, w_kv, w_in_b, w_out_b):
    d = w_in_a.shape[-1]
    return [w_in_a[0], w_out_a[0], w_kv, w_in_b[0], w_out_b[0].reshape(-1, d)]


WIDE = ["w_in_a", "w_out_a", "w_kv", "w_in_b", "w_out_b"]


def _small_rows(n_wide):
    return ROW_TILE - n_wide % ROW_TILE if n_wide % ROW_TILE else ROW_TILE


def _pack_small(conv_a, norm_a, norm_kv, norm_b, norm_f, rows):
    d = norm_kv.shape[-1]
    parts = [jnp.concatenate([conv_a[0].reshape(-1), norm_a[0]])[None, :], norm_kv.reshape(1, d),
             norm_b.reshape(1, d), norm_f.reshape(1, d), jnp.zeros((rows - 4, d), F32)]
    return jnp.concatenate(parts, axis=0)


def _unpack_small(packed, shapes):
    dq = packed.shape[1] // N_BLOCKS
    return dict(conv_a=packed[0, :3 * dq].reshape(shapes["conv_a"]), norm_a=packed[0, 3 * dq:].reshape(shapes["norm_a"]),
                norm_kv=packed[1].reshape(shapes["norm_kv"]), norm_b=packed[2].reshape(shapes["norm_b"]),
                norm_f=packed[3].reshape(shapes["norm_f"]))


WEIGHTS = ["norm_a", "w_in_a", "conv_a", "w_out_a", "norm_kv", "w_kv", "norm_b", "w_in_b", "w_out_b", "norm_f"]


def kernel(x, norm_a, w_in_a, conv_a, w_out_a, norm_kv, w_kv, norm_b, w_in_b, w_out_b, norm_f, loss_target, m_norm_a, m_w_in_a, m_conv_a, m_w_out_a, m_norm_kv, m_w_kv, m_norm_b, m_w_in_b, m_w_out_b, m_norm_f, v_norm_a, v_w_in_a, v_conv_a, v_w_out_a, v_norm_kv, v_w_kv, v_norm_b, v_w_in_b, v_w_out_b, v_norm_f):
    d = x.shape[-1]
    dq = d // N_BLOCKS
    att = w_out_b.shape[1]
    xs = x[0]
    target = loss_target[0]
    shapes = dict(norm_a=norm_a.shape, w_in_a=w_in_a.shape, conv_a=conv_a.shape, w_out_a=w_out_a.shape,
                  norm_kv=norm_kv.shape, w_kv=w_kv.shape, norm_b=norm_b.shape, w_in_b=w_in_b.shape,
                  w_out_b=w_out_b.shape, norm_f=norm_f.shape)

    w_wide = _wide_views(w_in_a, w_out_a, w_kv, w_in_b, w_out_b)
    m_wide = _wide_views(m_w_in_a, m_w_out_a, m_w_kv, m_w_in_b, m_w_out_b)
    v_wide = _wide_views(v_w_in_a, v_w_out_a, v_w_kv, v_w_in_b, v_w_out_b)
    n_wide = sum(w.shape[0] for w in w_wide)
    n_small = _small_rows(n_wide)
    w_small = _pack_small(conv_a, norm_a, norm_kv, norm_b, norm_f, n_small)
    m_small = _pack_small(m_conv_a, m_norm_a, m_norm_kv, m_norm_b, m_norm_f, n_small)
    v_small = _pack_small(v_conv_a, v_norm_a, v_norm_kv, v_norm_b, v_norm_f, n_small)
    rest_bf = jnp.concatenate(w_wide[1:], axis=0).astype(BF16)
    in_a_all, small_all = _gather_blocks(w_wide[0].astype(BF16), w_small[:SMALL_ROWS])
    wf_in_a = jnp.concatenate([in_a_all[b] for b in range(N_BLOCKS)], axis=1)
    conv_full = jnp.concatenate([small_all[b, 0, :3 * dq].reshape(3, dq) for b in range(N_BLOCKS)], axis=1)
    gain_a = jnp.concatenate([small_all[b, 0, 3 * dq:] for b in range(N_BLOCKS)])[None, :]
    gain_kv = norm_kv.reshape(1, d)
    gain_b = norm_b.reshape(1, d)
    gain_f = norm_f.reshape(1, d)

    u, rest_all = _norm_matmul_gather(xs, gain_a, wf_in_a, rest_bf, tm=512, tn=4 * d)
    o2, o3 = dq, 2 * dq
    wf_out_a = rest_all[:, :o2].reshape(d, d)
    wf_kv = rest_all[:, o2:o3].reshape(d, 2 * att)
    wf_in_b = rest_all[:, o3:o3 + dq].reshape(d, 2 * att)
    wf_out_b = jnp.concatenate([rest_all[b, o3 + dq:].reshape(att, dq) for b in range(N_BLOCKS)], axis=1)
    mix = _mixer_fwd(u, conv_full)
    x1 = _matmul(mix, wf_out_a, res=xs, name="out_proj_a")
    kv, qg = _norm_matmuls(x1, [gain_kv, gain_b], [wf_kv, wf_in_b], name="kv_in_proj_b")
    o, cin, first = _attention_fwd(qg, kv)
    gated, dx2, loss_cols, g_norm_f = _out_proj_b_loss(o, qg, wf_out_b, x1, gain_f, target)
    loss = lax.psum(jnp.sum(loss_cols), ("x", "y", "c"))

    g_w_out_b = _matmul_tn(gated, dx2, name="grad_w_out_b")
    do, dg = _out_proj_b_bwd(dx2, wf_out_b, o, qg)
    cin_pairs = cin.reshape(-1, 2, *cin.shape[2:])
    first_pairs = jnp.repeat(first.reshape(cin.shape[0], -1), cin.shape[1] // 2, axis=0).reshape(-1)
    dq_att, dkv = _attention_bwd(qg, kv, cin_pairs, first_pairs, do, hg=2)
    dqg = jnp.concatenate([dq_att, dg], axis=1)
    g_w_in_b = _matmul_tn(x1, dqg, gain=gain_b, name="grad_w_in_b")
    g_w_kv = _matmul_tn(x1, dkv, gain=gain_kv, name="grad_w_kv")
    dx1, g_norm_b, g_norm_kv = _proj_rms_bwd(x1, [dqg, dkv], [wf_in_b, wf_kv], [gain_b, gain_kv], dx2,
                                             name="rms_bwd_b")
    g_w_out_a = _matmul_tn(mix, dx1, name="grad_w_out_a")
    dmix = _matmul(dx1, wf_out_a, trans_b=True, name="d_mix")
    du, g_conv = _mixer_bwd(u, dmix, conv_full)
    g_w_in_a = _matmul_tn(xs, du, gain=gain_a, block_cols=d, tn=2 * d, tt=512, name="grad_w_in_a")
    wide_blocks = [g_w_in_a, g_w_out_a.reshape(N_BLOCKS, dq, d), g_w_kv.reshape(N_BLOCKS, dq, d),
                   g_w_in_b.reshape(N_BLOCKS, dq, d),
                   g_w_out_b.reshape(att, N_BLOCKS, dq).transpose(1, 0, 2).reshape(N_BLOCKS, att // N_BLOCKS, d)]
    dx, g_norm_a, wide_landed = _rms_bwd_a_scatter(du, wf_in_a, xs, dx1, gain_a, wide_blocks)
    small_rows = jnp.concatenate([
        jnp.concatenate([g_conv[:3].reshape(3, N_BLOCKS, dq).transpose(1, 0, 2).reshape(N_BLOCKS, 3 * dq),
                         g_norm_a.reshape(N_BLOCKS, dq)], axis=1)[:, None, :],
        jnp.broadcast_to(g_norm_kv[None], (N_BLOCKS, 1, d)),
        jnp.broadcast_to(g_norm_b[None], (N_BLOCKS, 1, d)),
        jnp.broadcast_to(g_norm_f[None], (N_BLOCKS, 1, d)),
        jnp.zeros((N_BLOCKS, n_small - 4, d), F32)], axis=1)

    mine = _sum_slots(wide_landed, _scatter_small(small_rows))
    theirs = _swap_with_sibling(mine)
    results = [{}, {}, {}, {}]
    row0 = 0
    for name, w, m, v in zip(WIDE, w_wide, m_wide, v_wide):
        for res, val in zip(results, _adamw(mine, theirs, row0, w, m, v, name="adamw_" + name)):
            res[name] = val.reshape(shapes[name])
        row0 += w.shape[0]
    for res, val in zip(results, _adamw(mine, theirs, n_wide, w_small, m_small, v_small, name="adamw_small")):
        res.update(_unpack_small(val, shapes))
    return (loss, dx[None], *[res[n] for res in results for n in WEIGHTS])
```

```python
import jax
import jax.numpy as jnp
from jax import lax
from jax.experimental import pallas as pl
from jax.experimental.pallas import tpu as pltpu

F32 = jnp.float32
BF16 = jnp.bfloat16
MESH = pl.DeviceIdType.MESH

HEAD_DIM = 64
RMS_EPS = 1e-6
ADAM_LR = 0.001
ADAM_B1 = 0.9
ADAM_B2 = 0.999
ADAM_EPS = 1e-08
ADAM_WD = 0.01
ADAM_STEP = 10
N_BLOCKS = 4
SMALL_ROWS = 8
ROW_TILE = 256
VMEM_LIMIT_V7X = 56 * 1024 * 1024
VMEM_LIMIT_ATTENTION_BWD_V7X = 62 * 1024 * 1024


def _params(*sem):
    return pltpu.CompilerParams(dimension_semantics=sem if sem else None, vmem_limit_bytes=VMEM_LIMIT_V7X)


def _sigmoid(x):
    return 1.0 / (1.0 + jnp.exp(-x))


def _rms_normalize(x):
    rstd = lax.rsqrt(jnp.mean(x * x, axis=-1, keepdims=True) + RMS_EPS)
    return x * rstd, rstd


def _norm_matmuls(x, gains, ws, *, name, tm=512):
    t, d = x.shape
    tm = min(tm, t)
    nk = len(ws)

    def body(*refs):
        x_ref = refs[0]
        xn, _ = _rms_normalize(x_ref[...])
        for g_ref, w_ref, o_ref in zip(refs[1:1 + nk], refs[1 + nk:1 + 2 * nk], refs[1 + 2 * nk:]):
            h = (xn * g_ref[...]).astype(BF16)
            o_ref[...] = jnp.dot(h, w_ref[...], preferred_element_type=F32).astype(o_ref.dtype)

    return pl.pallas_call(
        body, name=name, grid=(t // tm,),
        in_specs=([pl.BlockSpec((tm, d), lambda i: (i, 0))] + [pl.BlockSpec((1, d), lambda i: (0, 0))] * nk
                  + [pl.BlockSpec(w.shape, lambda i: (0, 0)) for w in ws]),
        out_specs=[pl.BlockSpec((tm, w.shape[1]), lambda i: (i, 0)) for w in ws],
        out_shape=[jax.ShapeDtypeStruct((t, w.shape[1]), BF16) for w in ws],
        compiler_params=_params("parallel"),
    )(x, *gains, *ws)


def _matmul(a, w, *, name, trans_b=False, res=None, out_dtype=F32, tm=1024, tn=1024, tk=1024):
    t, k = a.shape
    n = w.shape[0] if trans_b else w.shape[1]
    tm, tn, tk = min(tm, t), min(tn, n), min(tk, k)
    nk = k // tk

    def body(*refs):
        if res is None:
            a_ref, w_ref, o_ref, acc = refs
        else:
            a_ref, w_ref, r_ref, o_ref, acc = refs
        kk = pl.program_id(2)
        av = a_ref[...].astype(BF16)
        if trans_b:
            prod = lax.dot_general(av, w_ref[...], (((1,), (1,)), ((), ())), preferred_element_type=F32)
        else:
            prod = jnp.dot(av, w_ref[...], preferred_element_type=F32)
        if nk == 1:
            if res is not None:
                prod = prod + r_ref[...]
            o_ref[...] = prod.astype(o_ref.dtype)
            return

        @pl.when(kk == 0)
        def _():
            acc[...] = jnp.zeros_like(acc)

        acc[...] += prod

        @pl.when(kk == nk - 1)
        def _():
            r = acc[...]
            if res is not None:
                r = r + r_ref[...]
            o_ref[...] = r.astype(o_ref.dtype)

    in_specs = [pl.BlockSpec((tm, tk), lambda i, j, kk: (i, kk)),
                pl.BlockSpec((tn, tk), lambda i, j, kk: (j, kk)) if trans_b
                else pl.BlockSpec((tk, tn), lambda i, j, kk: (kk, j))]
    args = [a, w]
    if res is not None:
        in_specs.append(pl.BlockSpec((tm, tn), lambda i, j, kk: (i, j)))
        args.append(res)
    return pl.pallas_call(
        body, name=name, grid=(t // tm, n // tn, nk),
        in_specs=in_specs,
        out_specs=pl.BlockSpec((tm, tn), lambda i, j, kk: (i, j)),
        out_shape=jax.ShapeDtypeStruct((t, n), out_dtype),
        scratch_shapes=[pltpu.VMEM((tm, tn), F32)],
        compiler_params=_params("parallel", "parallel", "arbitrary"),
    )(*args)


def _matmul_tn(a, b, *, name, gain=None, block_cols=None, tt=1024, tn=1024):
    t, k = a.shape
    n = b.shape[1]
    tt, tn = min(tt, t), min(tn, n)
    nt = t // tt
    per_step = tn // block_cols if block_cols else 1

    def body(*refs):
        if gain is None:
            a_ref, b_ref, o_ref, acc = refs
            av = a_ref[...].astype(BF16)
        else:
            a_ref, g_ref, b_ref, o_ref, acc = refs
            xn, _ = _rms_normalize(a_ref[...])
            av = (xn * g_ref[...]).astype(BF16)

        @pl.when(pl.program_id(1) == 0)
        def _():
            acc[...] = jnp.zeros_like(acc)

        acc[...] += lax.dot_general(av, b_ref[...].astype(BF16), (((0,), (0,)), ((), ())),
                                    preferred_element_type=F32)

        @pl.when(pl.program_id(1) == nt - 1)
        def _():
            if block_cols:
                for c in range(per_step):
                    o_ref[c] = acc[:, c * block_cols:(c + 1) * block_cols].astype(o_ref.dtype)
            else:
                o_ref[...] = acc[...].astype(o_ref.dtype)

    in_specs = [pl.BlockSpec((tt, k), lambda j, s: (s, 0))]
    args = [a]
    if gain is not None:
        in_specs.append(pl.BlockSpec((1, k), lambda j, s: (0, 0)))
        args.append(gain)
    in_specs.append(pl.BlockSpec((tt, tn), lambda j, s: (s, j)))
    args.append(b)
    if block_cols:
        out_spec = pl.BlockSpec((per_step, k, block_cols), lambda j, s: (j, 0, 0))
        out_shape = jax.ShapeDtypeStruct((n // block_cols, k, block_cols), BF16)
    else:
        out_spec = pl.BlockSpec((k, tn), lambda j, s: (0, j))
        out_shape = jax.ShapeDtypeStruct((k, n), BF16)
    return pl.pallas_call(
        body, name=name, grid=(n // tn, nt),
        in_specs=in_specs, out_specs=out_spec, out_shape=out_shape,
        scratch_shapes=[pltpu.VMEM((k, tn), F32)],
        compiler_params=_params("parallel", "arbitrary"),
    )(*args)


HALO = 16


def _shift_down(v, halo, k):
    rows = lax.broadcasted_iota(jnp.int32, v.shape, 0)
    out = pltpu.roll(v, k, 0)
    for r in range(k):
        out = jnp.where(rows == r, halo[HALO - k + r:HALO - k + r + 1, :], out)
    return out


def _shift_up(v, halo, k):
    n = v.shape[0]
    rows = lax.broadcasted_iota(jnp.int32, v.shape, 0)
    out = pltpu.roll(v, n - k, 0)
    for r in range(k):
        out = jnp.where(rows == n - k + r, halo[r:r + 1, :], out)
    return out


def _mixer_fwd(u, conv, *, tm=256):
    t, d4 = u.shape
    d = d4 // 4
    tm = min(tm, t)
    hb = tm // HALO

    def body(b_ref, c_ref, x_ref, g_ref, ch_ref, xh_ref, w_ref, m_ref):
        i = pl.program_id(0)
        y1 = c_ref[...].astype(F32) * x_ref[...].astype(F32)
        prev = ch_ref[...].astype(F32) * xh_ref[...].astype(F32)
        prev = jnp.where(i == 0, 0.0, prev)
        w = w_ref[...]
        yc = w[2:3, :] * y1 + w[1:2, :] * _shift_down(y1, prev, 1) + w[0:1, :] * _shift_down(y1, prev, 2)
        g = g_ref[...].astype(F32)
        m_ref[...] = (b_ref[...].astype(F32) * yc * (g * _sigmoid(g))).astype(m_ref.dtype)

    def col(c):
        return pl.BlockSpec((tm, d), lambda i: (i, c))

    def prev_rows(c):
        return pl.BlockSpec((HALO, d), lambda i: (jnp.maximum(i * hb - 1, 0), c))

    return pl.pallas_call(
        body, name="mixer_fwd", grid=(t // tm,),
        in_specs=[col(0), col(1), col(2), col(3), prev_rows(1), prev_rows(2), pl.BlockSpec((3, d), lambda i: (0, 0))],
        out_specs=pl.BlockSpec((tm, d), lambda i: (i, 0)),
        out_shape=jax.ShapeDtypeStruct((t, d), BF16),
        compiler_params=_params("parallel"),
    )(u, u, u, u, u, u, conv)


def _mixer_bwd(u, dm, conv, *, tm=256):
    t, d4 = u.shape
    d = d4 // 4
    tm = min(tm, t)
    hb = tm // HALO
    nb = t // tm

    def body(b_ref, c_ref, x_ref, g_ref, dm_ref, ch_ref, xh_ref, bn_ref, gn_ref, dmn_ref, w_ref, du_ref, gw_ref):
        i = pl.program_id(0)
        w = w_ref[...]
        b = b_ref[...].astype(F32)
        c = c_ref[...].astype(F32)
        xin = x_ref[...].astype(F32)
        g = g_ref[...].astype(F32)
        dm_v = dm_ref[...]
        y1 = c * xin
        prev = jnp.where(i == 0, 0.0, ch_ref[...].astype(F32) * xh_ref[...].astype(F32))
        y1m1 = _shift_down(y1, prev, 1)
        y1m2 = _shift_down(y1, prev, 2)
        yc = w[2:3, :] * y1 + w[1:2, :] * y1m1 + w[0:1, :] * y1m2
        sg = _sigmoid(g)
        s = g * sg
        ds = sg * (1.0 + g * (1.0 - sg))
        dyc = dm_v * b * s
        gn = gn_ref[...].astype(F32)
        nxt = dmn_ref[...] * bn_ref[...].astype(F32) * (gn * _sigmoid(gn))
        nxt = jnp.where(i == nb - 1, 0.0, nxt)
        dy1 = w[2:3, :] * dyc + w[1:2, :] * _shift_up(dyc, nxt, 1) + w[0:1, :] * _shift_up(dyc, nxt, 2)
        du_ref[:, 0:d] = (dm_v * yc * s).astype(du_ref.dtype)
        du_ref[:, d:2 * d] = (dy1 * xin).astype(du_ref.dtype)
        du_ref[:, 2 * d:3 * d] = (dy1 * c).astype(du_ref.dtype)
        du_ref[:, 3 * d:4 * d] = (dm_v * b * yc * ds).astype(du_ref.dtype)

        @pl.when(i == 0)
        def _():
            gw_ref[...] = jnp.zeros_like(gw_ref)

        gw_ref[0:1, :] += jnp.sum(dyc * y1m2, axis=0, keepdims=True)
        gw_ref[1:2, :] += jnp.sum(dyc * y1m1, axis=0, keepdims=True)
        gw_ref[2:3, :] += jnp.sum(dyc * y1, axis=0, keepdims=True)

    def col(c):
        return pl.BlockSpec((tm, d), lambda i: (i, c))

    def prev_rows(c):
        return pl.BlockSpec((HALO, d), lambda i: (jnp.maximum(i * hb - 1, 0), c))

    def next_rows(c):
        return pl.BlockSpec((HALO, d), lambda i: (jnp.minimum((i + 1) * hb, nb * hb - 1), c))

    return pl.pallas_call(
        body, name="mixer_bwd", grid=(nb,),
        in_specs=[col(0), col(1), col(2), col(3), pl.BlockSpec((tm, d), lambda i: (i, 0)),
                  prev_rows(1), prev_rows(2), next_rows(0), next_rows(3),
                  pl.BlockSpec((HALO, d), lambda i: (jnp.minimum((i + 1) * hb, nb * hb - 1), 0)),
                  pl.BlockSpec((3, d), lambda i: (0, 0))],
        out_specs=[pl.BlockSpec((tm, d4), lambda i: (i, 0)), pl.BlockSpec((SMALL_ROWS, d), lambda i: (0, 0))],
        out_shape=[jax.ShapeDtypeStruct((t, d4), BF16), jax.ShapeDtypeStruct((SMALL_ROWS, d), F32)],
        compiler_params=_params("arbitrary"),
    )(u, u, u, u, dm, u, u, u, u, dm, conv)


def _out_proj_b_loss(o, qg, w_out, x1, gain, target, *, tm=512):
    t, a = o.shape
    d = x1.shape[1]
    tm = min(tm, t)

    def body(o_ref, g_ref, w_ref, x_ref, gain_ref, y_ref, gated_ref, dx_ref, loss_ref, gg_ref):
        @pl.when(pl.program_id(0) == 0)
        def _():
            loss_ref[...] = jnp.zeros_like(loss_ref)
            gg_ref[...] = jnp.zeros_like(gg_ref)

        g = g_ref[...].astype(F32)
        gated = (o_ref[...] * (g * _sigmoid(g))).astype(BF16)
        gated_ref[...] = gated
        x2 = x_ref[...] + jnp.dot(gated, w_ref[...], preferred_element_type=F32)
        xn, rstd = _rms_normalize(x2)
        gv = gain_ref[...]
        err = xn * gv - y_ref[...]
        loss_ref[...] += jnp.sum(err * err, axis=0, keepdims=True) * (0.5 / d)
        dy = err * (1.0 / d)
        gg_ref[...] += jnp.sum(dy * xn, axis=0, keepdims=True)
        dxn = dy * gv
        dx_ref[...] = rstd * (dxn - xn * jnp.mean(dxn * xn, axis=-1, keepdims=True))

    blk = pl.BlockSpec((tm, d), lambda i: (i, 0))
    blk_a = pl.BlockSpec((tm, a), lambda i: (i, 0))
    row = pl.BlockSpec((1, d), lambda i: (0, 0))
    return pl.pallas_call(
        body, name="out_proj_b_loss", grid=(t // tm,),
        in_specs=[blk_a, pl.BlockSpec((tm, a), lambda i: (i, 1)), pl.BlockSpec(w_out.shape, lambda i: (0, 0)),
                  blk, row, blk],
        out_specs=[blk_a, blk, row, row],
        out_shape=[jax.ShapeDtypeStruct((t, a), BF16), jax.ShapeDtypeStruct((t, d), F32),
                   jax.ShapeDtypeStruct((1, d), F32), jax.ShapeDtypeStruct((1, d), F32)],
        compiler_params=_params("arbitrary"),
    )(o, qg, w_out, x1, gain, target)


def _out_proj_b_bwd(dx2, w_out, o, qg, *, tm=512):
    t, a = o.shape
    d = dx2.shape[1]
    tm = min(tm, t)

    def body(dx_ref, w_ref, o_ref, g_ref, do_ref, dg_ref):
        da = _nt(dx_ref[...].astype(BF16), w_ref[...])
        g = g_ref[...].astype(F32)
        sg = _sigmoid(g)
        do_ref[...] = da * (g * sg)
        dg_ref[...] = (da * o_ref[...] * (sg * (1.0 + g * (1.0 - sg)))).astype(dg_ref.dtype)

    blk_a = pl.BlockSpec((tm, a), lambda i: (i, 0))
    return pl.pallas_call(
        body, name="out_proj_b_bwd", grid=(t // tm,),
        in_specs=[pl.BlockSpec((tm, d), lambda i: (i, 0)), pl.BlockSpec(w_out.shape, lambda i: (0, 0)), blk_a,
                  pl.BlockSpec((tm, a), lambda i: (i, 1))],
        out_specs=[blk_a, blk_a],
        out_shape=[jax.ShapeDtypeStruct((t, a), F32), jax.ShapeDtypeStruct((t, a), BF16)],
        compiler_params=_params("parallel"),
    )(dx2, w_out, o, qg)


def _proj_rms_bwd(x, cots, ws, gains, dres, *, name, tm=512):
    t, d = x.shape
    tm = min(tm, t)
    nk = len(cots)

    def body(*refs):
        x_ref = refs[0]
        cot_refs = refs[1:1 + nk]
        w_refs = refs[1 + nk:1 + 2 * nk]
        g_refs = refs[1 + 2 * nk:1 + 3 * nk]
        dres_ref = refs[1 + 3 * nk]
        dx_ref = refs[2 + 3 * nk]
        gg_refs = refs[3 + 3 * nk:]

        @pl.when(pl.program_id(0) == 0)
        def _():
            for gg in gg_refs:
                gg[...] = jnp.zeros_like(gg)

        xn, rstd = _rms_normalize(x_ref[...])
        dxn = None
        for cot_ref, w_ref, g_ref, gg in zip(cot_refs, w_refs, g_refs, gg_refs):
            dh = _nt(cot_ref[...].astype(BF16), w_ref[...])
            gg[...] += jnp.sum(dh * xn, axis=0, keepdims=True)
            term = dh * g_ref[...]
            dxn = term if dxn is None else dxn + term
        dx_ref[...] = dres_ref[...] + rstd * (dxn - xn * jnp.mean(dxn * xn, axis=-1, keepdims=True))

    blk = pl.BlockSpec((tm, d), lambda i: (i, 0))
    row = pl.BlockSpec((1, d), lambda i: (0, 0))
    return pl.pallas_call(
        body, name=name, grid=(t // tm,),
        in_specs=([blk] + [pl.BlockSpec((tm, c.shape[1]), lambda i: (i, 0)) for c in cots]
                  + [pl.BlockSpec(w.shape, lambda i: (0, 0)) for w in ws] + [row] * nk + [blk]),
        out_specs=[blk] + [row] * nk,
        out_shape=[jax.ShapeDtypeStruct((t, d), F32)] + [jax.ShapeDtypeStruct((1, d), F32)] * nk,
        compiler_params=_params("arbitrary"),
    )(x, *cots, *ws, *gains, dres)


def _suffix_ones(n):
    r = lax.broadcasted_iota(jnp.int32, (n, n), 0)
    c = lax.broadcasted_iota(jnp.int32, (n, n), 1)
    return (r >= c).astype(BF16)


def _suffix_sum(l, ones):
    return jnp.dot(l.astype(BF16), ones, preferred_element_type=F32)


def _log_one_minus_beta(z):
    return -(jnp.maximum(z, 0.0) + jnp.log(1.0 + jnp.exp(-jnp.abs(z))))


def _nt(a, b):
    return lax.dot_general(a, b, (((1,), (1,)), ((), ())), preferred_element_type=F32)


def _tn(a, b):
    return lax.dot_general(a, b, (((0,), (0,)), ((), ())), preferred_element_type=F32)


EXP_UNDERFLOW = 104.0
NORM_MARGIN = 1.01


def _key_norm_bounds(kv, *, tq):
    t, a2 = kv.shape
    a = a2 // 2
    heads = a // HEAD_DIM
    nq = t // tq

    per_step = min(4, nq)
    lanes = 128

    def body(k_ref, sel_ref, o_ref):
        k = k_ref[...].astype(F32)
        n2 = jnp.dot((k * k).astype(BF16), sel_ref[...], preferred_element_type=F32)
        o_ref[...] = jnp.sqrt(jnp.max(n2.reshape(per_step, tq, lanes), axis=1))

    head_of_col = lax.broadcasted_iota(jnp.int32, (a, lanes), 0) // HEAD_DIM
    sel = (head_of_col == lax.broadcasted_iota(jnp.int32, (a, lanes), 1)).astype(BF16)
    norms = pl.pallas_call(
        body, name="key_norms", grid=(nq // per_step,),
        in_specs=[pl.BlockSpec((per_step * tq, a), lambda i: (i, 0)), pl.BlockSpec((a, lanes), lambda i: (0, 0))],
        out_specs=pl.BlockSpec((None, per_step, lanes), lambda i: (i, 0, 0)),
        out_shape=jax.ShapeDtypeStruct((nq // per_step, per_step, lanes), F32),
        compiler_params=_params("parallel"),
    )(kv, sel)
    return lax.cummax(norms.reshape(nq, lanes)[:, :heads], axis=0).T.reshape(-1)


def _attention_fwd(qg, kv, *, tq=256, n_sub=1, hg=4):
    t, a2 = qg.shape
    a = a2 // 2
    hg = hg if a % (hg * HEAD_DIM) == 0 else 2
    gw = hg * HEAD_DIM
    ngroup = a // gw
    tq = min(tq, t)
    scale = HEAD_DIM ** -0.5
    nq = t // tq
    ts = tq // n_sub

    def body(kmax_ref, q_ref, k_ref, v_ref, ones_ref, o_ref, cin_ref, first_ref):
        p = pl.program_id(0)
        i = pl.program_id(1)
        ones = ones_ref[...]
        q_all = q_ref[...] * jnp.asarray(scale, BF16)
        q2 = q_all.astype(F32) * q_all.astype(F32)
        chains = [(h, r) for h in range(hg) for r in range(n_sub)]
        qs = [q_all[r * ts:(r + 1) * ts, h * HEAD_DIM:(h + 1) * HEAD_DIM] for h, r in chains]
        q_norm = [jnp.sqrt(jnp.sum(q2[r * ts:(r + 1) * ts, h * HEAD_DIM:(h + 1) * HEAD_DIM], axis=1, keepdims=True))
                  * NORM_MARGIN for h, r in chains]
        rows = lax.broadcasted_iota(jnp.int32, (ts, tq), 0)
        cols = lax.broadcasted_iota(jnp.int32, (ts, tq), 1)
        causal = [cols < rows + r * ts for r in range(n_sub)]
        block_of_lane = lax.broadcasted_iota(jnp.int32, (ts, nq), 1)
        cin_ref[...] = jnp.zeros_like(cin_ref)

        def any_weight_left_of(j, carry):
            jj = jnp.maximum(j - 1, 0)
            bound = None
            for n, (h, r) in enumerate(chains):
                top = jnp.max(carry[n][0] + q_norm[n] * kmax_ref[(hg * p + h) * nq + jj])
                bound = top if bound is None else jnp.maximum(bound, top)
            return bound > -EXP_UNDERFLOW

        def tile(j, carry, masked):
            start = pl.multiple_of(j * tq, tq)
            k_all = k_ref[pl.ds(start, tq), :]
            v_all = v_ref[pl.ds(start, tq), :]
            ks = [k_all[:, h * HEAD_DIM:(h + 1) * HEAD_DIM] for h in range(hg)]
            vs = [v_all[:, h * HEAD_DIM:(h + 1) * HEAD_DIM] for h in range(hg)]
            zs = [_nt(qs[n], ks[h]) for n, (h, r) in enumerate(chains)]
            ls = [_log_one_minus_beta(z) for z in zs]
            if masked:
                ls = [jnp.where(causal[r], l, 0.0) for l, (h, r) in zip(ls, chains)]
            r_locs = [_suffix_sum(l, ones) for l in ls]
            logws = [z + (carry[n][0] + r_loc) for n, (z, r_loc) in enumerate(zip(zs, r_locs))]
            if masked:
                logws = [jnp.where(causal[r], lw, -jnp.inf) for lw, (h, r) in zip(logws, chains)]
            ws = [jnp.exp(lw).astype(BF16) for lw in logws]
            out = []
            for n, (h, r) in enumerate(chains):
                acc = carry[n][1] + jnp.dot(ws[n], vs[h], preferred_element_type=F32)
                c = carry[n][0] + r_locs[n][:, 0:1]
                cin_ref[h, r * ts:(r + 1) * ts, :] = jnp.where(block_of_lane == j - 1, c,
                                                               cin_ref[h, r * ts:(r + 1) * ts, :])
                out.append((c, acc))
            return tuple(out)

        init = tuple((jnp.zeros((ts, 1), F32), jnp.zeros((ts, HEAD_DIM), F32)) for _ in chains)
        carry = tile(i, init, True)

        def visit(state):
            j, _, cr = state
            cr = tile(j - 1, cr, False)
            return j - 1, any_weight_left_of(j - 1, cr), cr

        first, _, carry = lax.while_loop(lambda s: (s[0] > 0) & s[1], visit,
                                         (i, any_weight_left_of(i, carry), carry))
        first_ref[p * nq + i] = first
        heads = [jnp.concatenate([carry[h * n_sub + r][1] for r in range(n_sub)], axis=0) for h in range(hg)]
        o_ref[...] = jnp.concatenate(heads, axis=1)

    return pl.pallas_call(
        body, name="attention_fwd",
        grid_spec=pltpu.PrefetchScalarGridSpec(
            num_scalar_prefetch=1, grid=(ngroup, nq),
            in_specs=[pl.BlockSpec((tq, gw), lambda p, i, km: (i, p)),
                      pl.BlockSpec((t, gw), lambda p, i, km: (0, p)),
                      pl.BlockSpec((t, gw), lambda p, i, km: (0, ngroup + p)),
                      pl.BlockSpec((tq, tq), lambda p, i, km: (0, 0))],
            out_specs=[pl.BlockSpec((tq, gw), lambda p, i, km: (i, p)),
                       pl.BlockSpec((None, hg, tq, nq), lambda p, i, km: (p, 0, i, 0)),
                       pl.BlockSpec(memory_space=pltpu.SMEM)]),
        out_shape=[jax.ShapeDtypeStruct((t, a), F32), jax.ShapeDtypeStruct((ngroup, hg, t, nq), F32),
                   jax.ShapeDtypeStruct((ngroup * nq,), jnp.int32)],
        compiler_params=_params("arbitrary", "arbitrary"),
    )(_key_norm_bounds(kv, tq=tq), qg, kv, kv, _suffix_ones(tq))


def _attention_bwd(qg, kv, cin, first, do, *, tq=256, hg=4):
    t, a2 = qg.shape
    a = a2 // 2
    hg = hg if a % (hg * HEAD_DIM) == 0 else 2
    gw = hg * HEAD_DIM
    ngroup = a // gw
    tq = min(tq, t)
    nq = t // tq
    scale = HEAD_DIM ** -0.5

    def body(first_ref, q_ref, kv_ref, cin_ref, do_ref, ones_ref, ones_t_ref, dq_ref, dkv_ref,
             k_ref, v_ref, dk_acc, dv_acc, sem):
        p = pl.program_id(0)
        i = pl.program_id(1)
        first = jnp.clip(first_ref[p * nq + i], 0, i)
        k_cols = pl.multiple_of(p * gw, gw)
        v_cols = pl.multiple_of((ngroup + p) * gw, gw)

        @pl.when(i == 0)
        def _():
            loads = [pltpu.make_async_copy(kv_ref.at[:, pl.ds(k_cols, gw)], k_ref, sem.at[0]),
                     pltpu.make_async_copy(kv_ref.at[:, pl.ds(v_cols, gw)], v_ref, sem.at[1])]
            for cp in loads:
                cp.start()
            dk_acc[...] = jnp.zeros_like(dk_acc)
            dv_acc[...] = jnp.zeros_like(dv_acc)
            for cp in loads:
                cp.wait()

        ones = ones_ref[...]
        ones_t = ones_t_ref[...]
        q_all = q_ref[...] * jnp.asarray(scale, BF16)
        do_bf = do_ref[...].astype(BF16)
        qs = [q_all[:, h * HEAD_DIM:(h + 1) * HEAD_DIM] for h in range(hg)]
        dos = [do_bf[:, h * HEAD_DIM:(h + 1) * HEAD_DIM] for h in range(hg)]
        rows = lax.broadcasted_iota(jnp.int32, (tq, tq), 0)
        cols = lax.broadcasted_iota(jnp.int32, (tq, tq), 1)
        causal = cols < rows
        block_of_lane = lax.broadcasted_iota(jnp.int32, (tq, nq), 1)

        def tile(j, carry, masked):
            start = pl.multiple_of(j * tq, tq)
            k_all = k_ref[pl.ds(start, tq), :]
            v_all = v_ref[pl.ds(start, tq), :]
            hs = range(hg)
            ks = [k_all[:, h * HEAD_DIM:(h + 1) * HEAD_DIM] for h in hs]
            vs = [v_all[:, h * HEAD_DIM:(h + 1) * HEAD_DIM] for h in hs]
            zs = [_nt(qs[h], ks[h]) for h in hs]
            das = [_nt(dos[h], vs[h]) for h in hs]
            ls = [_log_one_minus_beta(z) for z in zs]
            if masked:
                ls = [jnp.where(causal, l, 0.0) for l in ls]
            betas = [jnp.exp(z + l) for z, l in zip(zs, ls)]
            r_locs = [_suffix_sum(l, ones) for l in ls]
            if masked:
                logws = [jnp.where(causal, z + r_loc, -jnp.inf) for z, r_loc in zip(zs, r_locs)]
            else:
                cs = [jnp.sum(jnp.where(block_of_lane == j, cin_ref[h], 0.0), axis=1, keepdims=True) for h in hs]
                logws = [z + (c + r_loc) for z, c, r_loc in zip(zs, cs, r_locs)]
            ws = [jnp.exp(lw) for lw in logws]
            gs = [w * da for w, da in zip(ws, das)]
            g_pres = [jnp.dot(g.astype(BF16), ones_t, preferred_element_type=F32) for g in gs]
            dvs = [_tn(ws[h].astype(BF16), dos[h]) for h in hs]
            dzs = [gs[h] - betas[h] * (carry[h][0] + g_pres[h]) for h in hs]
            if masked:
                dzs = [jnp.where(causal, dz, 0.0) for dz in dzs]
            dzs = [dz.astype(BF16) for dz in dzs]
            dqs = [carry[h][1] + jnp.dot(dzs[h], ks[h], preferred_element_type=F32) for h in hs]
            dks = [_tn(dzs[h], qs[h]) for h in hs]
            dk_acc[pl.ds(start, tq), :] += jnp.concatenate(dks, axis=1)
            dv_acc[pl.ds(start, tq), :] += jnp.concatenate(dvs, axis=1)
            return tuple((carry[h][0] + g_pres[h][:, tq - 1:tq], dqs[h]) for h in hs)

        init = tuple((jnp.zeros((tq, 1), F32), jnp.zeros((tq, HEAD_DIM), F32)) for _ in range(hg))
        carry = lax.fori_loop(first, i, lambda j, cr: tile(j, cr, False), init)
        carry = tile(i, carry, True)
        dq_ref[...] = (jnp.concatenate([carry[h][1] for h in range(hg)], axis=1) * scale).astype(dq_ref.dtype)

        @pl.when(i == nq - 1)
        def _():
            ck = pltpu.make_async_copy(dk_acc, dkv_ref.at[:, pl.ds(k_cols, gw)], sem.at[0])
            cv = pltpu.make_async_copy(dv_acc, dkv_ref.at[:, pl.ds(v_cols, gw)], sem.at[1])
            ck.start()
            cv.start()
            ck.wait()
            cv.wait()

    blk = pl.BlockSpec((tq, gw), lambda p, i, fr: (i, p))
    tri = pl.BlockSpec((tq, tq), lambda p, i, fr: (0, 0))
    ones = _suffix_ones(tq)
    return pl.pallas_call(
        body, name="attention_bwd",
        grid_spec=pltpu.PrefetchScalarGridSpec(
            num_scalar_prefetch=1, grid=(ngroup, nq),
            in_specs=[blk, pl.BlockSpec(memory_space=pl.ANY),
                      pl.BlockSpec((None, hg, tq, nq), lambda p, i, fr: (p, 0, i, 0)),
                      blk, tri, tri],
            out_specs=[blk, pl.BlockSpec(memory_space=pl.ANY)],
            scratch_shapes=[pltpu.VMEM((t, gw), BF16), pltpu.VMEM((t, gw), BF16),
                            pltpu.VMEM((t, gw), F32), pltpu.VMEM((t, gw), F32),
                            pltpu.SemaphoreType.DMA((2,))]),
        out_shape=[jax.ShapeDtypeStruct((t, a), BF16), jax.ShapeDtypeStruct((t, a2), F32)],
        compiler_params=pltpu.CompilerParams(dimension_semantics=("arbitrary", "arbitrary"),
                                             vmem_limit_bytes=VMEM_LIMIT_ATTENTION_BWD_V7X),
    )(first, qg, kv, cin, do, ones, ones.T)


def _other_chips(x, y):
    return [(1 - x, y), (x, 1 - y), (1 - x, 1 - y)]


def _gather_blocks(wide, small):
    half = wide.shape[0] // 2

    def body(w_ref, s_ref, wo_ref, so_ref, send_sems, recv_sems, local_sems, pass_send, pass_recv):
        x, y, c = lax.axis_index("x"), lax.axis_index("y"), lax.axis_index("c")
        mine = 2 * x + y
        my_rows = pl.ds(pl.multiple_of(c * half, half), half)
        sibling_rows = pl.ds(pl.multiple_of((1 - c) * half, half), half)
        srcs = (w_ref.at[my_rows], s_ref)
        local, sends, recvs = _block_exchange(
            2, lambda k, slot: srcs[k], lambda k, slot: (wo_ref.at[slot, my_rows], so_ref.at[slot])[k],
            send_sems, recv_sems, local_sems)
        local.append(pltpu.make_async_copy(w_ref.at[sibling_rows], wo_ref.at[mine, sibling_rows], local_sems.at[2]))
        for cp in local + sends:
            cp.start()

        def passed(p, slot, rows):
            return pltpu.make_async_remote_copy(
                src_ref=wo_ref.at[slot, rows], dst_ref=wo_ref.at[slot, rows], send_sem=pass_send.at[p],
                recv_sem=pass_recv.at[p], device_id=(x, y, 1 - c), device_id_type=MESH)

        slots = [2 * px + py for px, py in _other_chips(x, y)]
        passes = []
        for p, slot in enumerate(slots):
            recvs[2 * p].wait_recv()
            passes.append(passed(p, slot, my_rows))
            passes[-1].start()
        for p, slot in enumerate(slots):
            recvs[2 * p + 1].wait_recv()
            passed(p, slot, sibling_rows).wait_recv()
        for cp in sends + passes:
            cp.wait_send()
        for cp in local:
            cp.wait()

    any_spec = pl.BlockSpec(memory_space=pl.ANY)
    return pl.pallas_call(
        body, name="gather_weights",
        in_specs=[any_spec, any_spec], out_specs=[any_spec, any_spec],
        out_shape=[jax.ShapeDtypeStruct((N_BLOCKS,) + wide.shape, wide.dtype),
                   jax.ShapeDtypeStruct((N_BLOCKS,) + small.shape, small.dtype)],
        scratch_shapes=[pltpu.SemaphoreType.DMA((6,)), pltpu.SemaphoreType.DMA((6,)), pltpu.SemaphoreType.DMA((3,)),
                        pltpu.SemaphoreType.DMA((3,)), pltpu.SemaphoreType.DMA((3,))],
    )(wide, small)


def _norm_matmul_gather(x, gain, w, rest, *, tm=1024, tn=1024):
    t, d = x.shape
    n = w.shape[1]
    tm, tn = min(tm, t), min(tn, n)
    ni, nj = t // tm, n // tn

    def body(x_ref, g_ref, w_ref, rest_ref, o_ref, all_ref, send_sems, recv_sems, local_sems):
        i, j = pl.program_id(0), pl.program_id(1)
        copies = _block_exchange(1, lambda k, slot: rest_ref, lambda k, slot: all_ref.at[slot],
                                 send_sems, recv_sems, local_sems)

        @pl.when((i == 0) & (j == 0))
        def _():
            for cp in copies[0] + copies[1]:
                cp.start()

        xn, _ = _rms_normalize(x_ref[...])
        h = (xn * g_ref[...]).astype(BF16)
        o_ref[...] = jnp.dot(h, w_ref[...], preferred_element_type=F32).astype(o_ref.dtype)

        @pl.when((i == ni - 1) & (j == nj - 1))
        def _():
            _finish_exchange(*copies)

    any_spec = pl.BlockSpec(memory_space=pl.ANY)
    return pl.pallas_call(
        body, name="in_proj_a_gather", grid=(ni, nj),
        in_specs=[pl.BlockSpec((tm, d), lambda i, j: (i, 0)),
                  pl.BlockSpec((1, d), lambda i, j: (0, 0)),
                  pl.BlockSpec((d, tn), lambda i, j: (0, j)), any_spec],
        out_specs=[pl.BlockSpec((tm, tn), lambda i, j: (i, j)), any_spec],
        out_shape=[jax.ShapeDtypeStruct((t, n), BF16), jax.ShapeDtypeStruct((N_BLOCKS,) + rest.shape, rest.dtype)],
        scratch_shapes=_exchange_sems(1),
        compiler_params=_params("arbitrary", "arbitrary"),
    )(x, gain, w, rest)


def _block_exchange(n, source, landing, send_sems, recv_sems, local_sems):
    x, y, c = lax.axis_index("x"), lax.axis_index("y"), lax.axis_index("c")
    mine = 2 * x + y
    local = [pltpu.make_async_copy(source(k, mine), landing(k, mine), local_sems.at[k]) for k in range(n)]
    sends, recvs = [], []
    for p, (px, py) in enumerate(_other_chips(x, y)):
        for k in range(n):
            sems = dict(send_sem=send_sems.at[p * n + k], recv_sem=recv_sems.at[p * n + k],
                        device_id=(px, py, c), device_id_type=MESH)
            sends.append(pltpu.make_async_remote_copy(src_ref=source(k, 2 * px + py), dst_ref=landing(k, mine),
                                                      **sems))
            recvs.append(pltpu.make_async_remote_copy(src_ref=source(k, mine), dst_ref=landing(k, 2 * px + py),
                                                      **sems))
    return local, sends, recvs


def _run_exchange(local, sends, recvs):
    for cp in local + sends:
        cp.start()
    _finish_exchange(local, sends, recvs)


def _finish_exchange(local, sends, recvs):
    for cp in recvs:
        cp.wait_recv()
    for cp in sends:
        cp.wait_send()
    for cp in local:
        cp.wait()


def _exchange_sems(n):
    return [pltpu.SemaphoreType.DMA((3 * n,)), pltpu.SemaphoreType.DMA((3 * n,)), pltpu.SemaphoreType.DMA((n,))]


def _scatter_small(small):
    def body(s_ref, out_ref, send_sems, recv_sems, local_sems):
        _run_exchange(*_block_exchange(1, lambda k, slot: s_ref.at[slot], lambda k, slot: out_ref.at[slot],
                                       send_sems, recv_sems, local_sems))

    any_spec = pl.BlockSpec(memory_space=pl.ANY)
    return pl.pallas_call(
        body, name="scatter_small",
        in_specs=[any_spec], out_specs=any_spec,
        out_shape=jax.ShapeDtypeStruct(small.shape, F32),
        scratch_shapes=_exchange_sems(1),
    )(small)


def _rms_bwd_a_scatter(du, w_in, x, dx1, gain, wide, *, tm=512):
    t, d = x.shape
    tm = min(tm, t)
    nb = t // tm
    n = len(wide)
    offsets = [sum(w.shape[1] for w in wide[:k]) for k in range(n)]
    n_wide = sum(w.shape[1] for w in wide)

    def body(*refs):
        du_ref, w_ref, x_ref, dx1_ref, g_ref = refs[:5]
        wide_refs = refs[5:5 + n]
        dx_ref, gg_ref, land_ref, send_sems, recv_sems, local_sems = refs[5 + n:]
        i = pl.program_id(0)
        copies = _block_exchange(
            n, lambda k, slot: wide_refs[k].at[slot],
            lambda k, slot: land_ref.at[slot, pl.ds(offsets[k], wide[k].shape[1])],
            send_sems, recv_sems, local_sems)

        @pl.when(i == 0)
        def _():
            gg_ref[...] = jnp.zeros_like(gg_ref)
            for cp in copies[0] + copies[1]:
                cp.start()

        dh = _nt(du_ref[...], w_ref[...])
        xn, rstd = _rms_normalize(x_ref[...])
        gg_ref[...] += jnp.sum(dh * xn, axis=0, keepdims=True)
        dxn = dh * g_ref[...]
        dx_ref[...] = dx1_ref[...] + rstd * (dxn - xn * jnp.mean(dxn * xn, axis=-1, keepdims=True))

        @pl.when(i == nb - 1)
        def _():
            _finish_exchange(*copies)

    blk = pl.BlockSpec((tm, d), lambda i: (i, 0))
    row = pl.BlockSpec((1, d), lambda i: (0, 0))
    any_spec = pl.BlockSpec(memory_space=pl.ANY)
    return pl.pallas_call(
        body, name="rms_bwd_a_scatter", grid=(nb,),
        in_specs=[pl.BlockSpec((tm, du.shape[1]), lambda i: (i, 0)),
                  pl.BlockSpec(w_in.shape, lambda i: (0, 0)), blk, blk, row] + [any_spec] * n,
        out_specs=[blk, row, any_spec],
        out_shape=[jax.ShapeDtypeStruct((t, d), F32), jax.ShapeDtypeStruct((1, d), F32),
                   jax.ShapeDtypeStruct((N_BLOCKS, n_wide, d), BF16)],
        scratch_shapes=_exchange_sems(n),
        compiler_params=_params("arbitrary"),
    )(du, w_in, x, dx1, gain, *wide)


def _swap_with_sibling(part):
    def body(p_ref, out_ref, send_sem, recv_sem):
        x, y, c = lax.axis_index("x"), lax.axis_index("y"), lax.axis_index("c")
        cp = pltpu.make_async_remote_copy(src_ref=p_ref, dst_ref=out_ref, send_sem=send_sem, recv_sem=recv_sem,
                                          device_id=(x, y, 1 - c), device_id_type=MESH)
        cp.start()
        cp.wait()

    any_spec = pl.BlockSpec(memory_space=pl.ANY)
    return pl.pallas_call(
        body, name="swap_sibling",
        in_specs=[any_spec], out_specs=any_spec,
        out_shape=jax.ShapeDtypeStruct(part.shape, part.dtype),
        scratch_shapes=[pltpu.SemaphoreType.DMA, pltpu.SemaphoreType.DMA],
    )(part)


def _sum_slots(wide, small, n_small):
    _, n_wide, d = wide.shape
    tr = n_small
    nw = n_wide // tr

    def body(w_ref, s_ref, o_ref):
        i = pl.program_id(0)

        @pl.when(i < nw)
        def _():
            w = w_ref[...].astype(F32)
            o_ref[...] = ((w[0] + w[1]) + w[2]) + w[3]

        @pl.when(i == nw)
        def _():
            o_ref[...] = jnp.zeros_like(o_ref)
            o_ref[0:SMALL_ROWS, :] = ((s_ref[0] + s_ref[1]) + s_ref[2]) + s_ref[3]

    return pl.pallas_call(
        body, name="sum_slots", grid=(nw + 1,),
        in_specs=[pl.BlockSpec((N_BLOCKS, tr, d), lambda i: (0, jnp.minimum(i, nw - 1), 0)),
                  pl.BlockSpec((N_BLOCKS, SMALL_ROWS, d), lambda i: (0, 0, 0))],
        out_specs=pl.BlockSpec((tr, d), lambda i: (i, 0)),
        out_shape=jax.ShapeDtypeStruct((n_wide + tr, d), F32),
        compiler_params=_params("arbitrary"),
    )(wide, small)


def _adamw(part_mine, part_sibling, row0, w, m, v, *, name):
    r, d = w.shape
    tr = next(c for c in (ROW_TILE, 128, 64, 32, 16, 8) if r % c == 0 and row0 % c == 0)
    first_block = row0 // tr
    m_scale = 1.0 / (1.0 - ADAM_B1 ** ADAM_STEP)
    v_scale = 1.0 / (1.0 - ADAM_B2 ** ADAM_STEP)

    def body(a_ref, b_ref, w_ref, m_ref, v_ref, g_ref, d_ref, mo_ref, vo_ref):
        g = a_ref[...] + b_ref[...]
        m_new = ADAM_B1 * m_ref[...] + (1.0 - ADAM_B1) * g
        v_new = ADAM_B2 * v_ref[...] + (1.0 - ADAM_B2) * (g * g)
        g_ref[...] = g
        mo_ref[...] = m_new
        vo_ref[...] = v_new
        d_ref[...] = -ADAM_LR * ((m_new * m_scale) / (jnp.sqrt(v_new * v_scale) + ADAM_EPS) + ADAM_WD * w_ref[...])

    blk = pl.BlockSpec((tr, d), lambda i: (i, 0))
    part = pl.BlockSpec((tr, d), lambda i: (first_block + i, 0))
    return pl.pallas_call(
        body, name=name, grid=(r // tr,),
        in_specs=[part, part, blk, blk, blk], out_specs=[blk] * 4,
        out_shape=[jax.ShapeDtypeStruct((r, d), F32)] * 4,
        compiler_params=_params("parallel"),
    )(part_mine, part_sibling, w, m, v)


def _wide_views(w_in_a, w_out_a, w_kv, w_in_b, w_out_b):
    d = w_in_a.shape[-1]
    return [w_in_a[0], w_out_a[0], w_kv, w_in_b[0], w_out_b[0].reshape(-1, d)]


WIDE = ["w_in_a", "w_out_a", "w_kv", "w_in_b", "w_out_b"]


def _small_rows(n_wide):
    return ROW_TILE - n_wide % ROW_TILE if n_wide % ROW_TILE else ROW_TILE


def _pack_small(conv_a, norm_a, norm_kv, norm_b, norm_f, rows):
    d = norm_kv.shape[-1]
    parts = [jnp.concatenate([conv_a[0].reshape(-1), norm_a[0]])[None, :], norm_kv.reshape(1, d),
             norm_b.reshape(1, d), norm_f.reshape(1, d), jnp.zeros((rows - 4, d), F32)]
    return jnp.concatenate(parts, axis=0)


def _unpack_small(packed, shapes):
    dq = packed.shape[1] // N_BLOCKS
    return dict(conv_a=packed[0, :3 * dq].reshape(shapes["conv_a"]), norm_a=packed[0, 3 * dq:].reshape(shapes["norm_a"]),
                norm_kv=packed[1].reshape(shapes["norm_kv"]), norm_b=packed[2].reshape(shapes["norm_b"]),
                norm_f=packed[3].reshape(shapes["norm_f"]))


WEIGHTS = ["norm_a", "w_in_a", "conv_a", "w_out_a", "norm_kv", "w_kv", "norm_b", "w_in_b", "w_out_b", "norm_f"]


def kernel(x, norm_a, w_in_a, conv_a, w_out_a, norm_kv, w_kv, norm_b, w_in_b, w_out_b, norm_f, loss_target, m_norm_a, m_w_in_a, m_conv_a, m_w_out_a, m_norm_kv, m_w_kv, m_norm_b, m_w_in_b, m_w_out_b, m_norm_f, v_norm_a, v_w_in_a, v_conv_a, v_w_out_a, v_norm_kv, v_w_kv, v_norm_b, v_w_in_b, v_w_out_b, v_norm_f):
    d = x.shape[-1]
    dq = d // N_BLOCKS
    att = w_out_b.shape[1]
    xs = x[0]
    target = loss_target[0]
    shapes = dict(norm_a=norm_a.shape, w_in_a=w_in_a.shape, conv_a=conv_a.shape, w_out_a=w_out_a.shape,
                  norm_kv=norm_kv.shape, w_kv=w_kv.shape, norm_b=norm_b.shape, w_in_b=w_in_b.shape,
                  w_out_b=w_out_b.shape, norm_f=norm_f.shape)

    w_wide = _wide_views(w_in_a, w_out_a, w_kv, w_in_b, w_out_b)
    m_wide = _wide_views(m_w_in_a, m_w_out_a, m_w_kv, m_w_in_b, m_w_out_b)
    v_wide = _wide_views(v_w_in_a, v_w_out_a, v_w_kv, v_w_in_b, v_w_out_b)
    n_wide = sum(w.shape[0] for w in w_wide)
    n_small = _small_rows(n_wide)
    w_small = _pack_small(conv_a, norm_a, norm_kv, norm_b, norm_f, n_small)
    m_small = _pack_small(m_conv_a, m_norm_a, m_norm_kv, m_norm_b, m_norm_f, n_small)
    v_small = _pack_small(v_conv_a, v_norm_a, v_norm_kv, v_norm_b, v_norm_f, n_small)
    rest_bf = jnp.concatenate(w_wide[1:], axis=0).astype(BF16)
    in_a_all, small_all = _gather_blocks(w_wide[0].astype(BF16), w_small[:SMALL_ROWS])
    wf_in_a = jnp.concatenate([in_a_all[b] for b in range(N_BLOCKS)], axis=1)
    conv_full = jnp.concatenate([small_all[b, 0, :3 * dq].reshape(3, dq) for b in range(N_BLOCKS)], axis=1)
    gain_a = jnp.concatenate([small_all[b, 0, 3 * dq:] for b in range(N_BLOCKS)])[None, :]
    gain_kv = norm_kv.reshape(1, d)
    gain_b = norm_b.reshape(1, d)
    gain_f = norm_f.reshape(1, d)

    u, rest_all = _norm_matmul_gather(xs, gain_a, wf_in_a, rest_bf, tm=512, tn=4 * d)
    o2, o3 = dq, 2 * dq
    wf_out_a = rest_all[:, :o2].reshape(d, d)
    wf_kv = rest_all[:, o2:o3].reshape(d, 2 * att)
    wf_in_b = rest_all[:, o3:o3 + dq].reshape(d, 2 * att)
    wf_out_b = jnp.concatenate([rest_all[b, o3 + dq:].reshape(att, dq) for b in range(N_BLOCKS)], axis=1)
    mix = _mixer_fwd(u, conv_full)
    x1 = _matmul(mix, wf_out_a, res=xs, name="out_proj_a")
    kv, qg = _norm_matmuls(x1, [gain_kv, gain_b], [wf_kv, wf_in_b], name="kv_in_proj_b")
    o, cin, first = _attention_fwd(qg, kv)
    gated, dx2, loss_cols, g_norm_f = _out_proj_b_loss(o, qg, wf_out_b, x1, gain_f, target)
    loss = lax.psum(jnp.sum(loss_cols), ("x", "y", "c"))

    g_w_out_b = _matmul_tn(gated, dx2, name="grad_w_out_b")
    do, dg = _out_proj_b_bwd(dx2, wf_out_b, o, qg)
    dq_att, dkv = _attention_bwd(qg, kv, cin, first, do)
    dqg = jnp.concatenate([dq_att, dg], axis=1)
    g_w_in_b = _matmul_tn(x1, dqg, gain=gain_b, name="grad_w_in_b")
    g_w_kv = _matmul_tn(x1, dkv, gain=gain_kv, name="grad_w_kv")
    dx1, g_norm_b, g_norm_kv = _proj_rms_bwd(x1, [dqg, dkv], [wf_in_b, wf_kv], [gain_b, gain_kv], dx2,
                                             name="rms_bwd_b")
    g_w_out_a = _matmul_tn(mix, dx1, name="grad_w_out_a")
    dmix = _matmul(dx1, wf_out_a, trans_b=True, name="d_mix")
    du, g_conv = _mixer_bwd(u, dmix, conv_full)
    g_w_in_a = _matmul_tn(xs, du, gain=gain_a, block_cols=d, tn=2 * d, tt=512, name="grad_w_in_a")
    wide_blocks = [g_w_in_a, g_w_out_a.reshape(N_BLOCKS, dq, d), g_w_kv.reshape(N_BLOCKS, dq, d),
                   g_w_in_b.reshape(N_BLOCKS, dq, d),
                   g_w_out_b.reshape(att, N_BLOCKS, dq).transpose(1, 0, 2).reshape(N_BLOCKS, att // N_BLOCKS, d)]
    dx, g_norm_a, wide_landed = _rms_bwd_a_scatter(du, wf_in_a, xs, dx1, gain_a, wide_blocks)
    small_rows = jnp.concatenate([
        jnp.concatenate([g_conv[:3].reshape(3, N_BLOCKS, dq).transpose(1, 0, 2).reshape(N_BLOCKS, 3 * dq),
                         g_norm_a.reshape(N_BLOCKS, dq)], axis=1)[:, None, :],
        jnp.broadcast_to(g_norm_kv[None], (N_BLOCKS, 1, d)),
        jnp.broadcast_to(g_norm_b[None], (N_BLOCKS, 1, d)),
        jnp.broadcast_to(g_norm_f[None], (N_BLOCKS, 1, d)),
        jnp.zeros((N_BLOCKS, SMALL_ROWS - 4, d), F32)], axis=1)

    mine = _sum_slots(wide_landed, _scatter_small(small_rows), n_small)
    theirs = _swap_with_sibling(mine)
    results = [{}, {}, {}, {}]
    row0 = 0
    for name, w, m, v in zip(WIDE, w_wide, m_wide, v_wide):
        for res, val in zip(results, _adamw(mine, theirs, row0, w, m, v, name="adamw_" + name)):
            res[name] = val.reshape(shapes[name])
        row0 += w.shape[0]
    for res, val in zip(results, _adamw(mine, theirs, n_wide, w_small, m_small, v_small, name="adamw_small")):
        res.update(_unpack_small(val, shapes))
    return (loss, dx[None], *[res[n] for res in results for n in WEIGHTS])
```

```python
import jax
import jax.numpy as jnp
from jax import lax
from jax.experimental import pallas as pl
from jax.experimental.pallas import tpu as pltpu

F32 = jnp.float32
BF16 = jnp.bfloat16
MESH = pl.DeviceIdType.MESH

HEAD_DIM = 64
RMS_EPS = 1e-6
ADAM_LR = 0.001
ADAM_B1 = 0.9
ADAM_B2 = 0.999
ADAM_EPS = 1e-08
ADAM_WD = 0.01
ADAM_STEP = 10
N_BLOCKS = 4
SMALL_ROWS = 8
ROW_TILE = 256
VMEM_LIMIT_V7X = 56 * 1024 * 1024
VMEM_LIMIT_ATTENTION_BWD_V7X = 62 * 1024 * 1024


def _params(*sem):
    return pltpu.CompilerParams(dimension_semantics=sem if sem else None, vmem_limit_bytes=VMEM_LIMIT_V7X)


def _sigmoid(x):
    return 1.0 / (1.0 + jnp.exp(-x))


def _rms_normalize(x):
    rstd = lax.rsqrt(jnp.mean(x * x, axis=-1, keepdims=True) + RMS_EPS)
    return x * rstd, rstd


def _norm_matmuls(x, gains, ws, *, name, tm=512):
    t, d = x.shape
    tm = min(tm, t)
    nk = len(ws)

    def body(*refs):
        x_ref = refs[0]
        xn, _ = _rms_normalize(x_ref[...])
        for g_ref, w_ref, o_ref in zip(refs[1:1 + nk], refs[1 + nk:1 + 2 * nk], refs[1 + 2 * nk:]):
            h = (xn * g_ref[...]).astype(BF16)
            o_ref[...] = jnp.dot(h, w_ref[...], preferred_element_type=F32).astype(o_ref.dtype)

    return pl.pallas_call(
        body, name=name, grid=(t // tm,),
        in_specs=([pl.BlockSpec((tm, d), lambda i: (i, 0))] + [pl.BlockSpec((1, d), lambda i: (0, 0))] * nk
                  + [pl.BlockSpec(w.shape, lambda i: (0, 0)) for w in ws]),
        out_specs=[pl.BlockSpec((tm, w.shape[1]), lambda i: (i, 0)) for w in ws],
        out_shape=[jax.ShapeDtypeStruct((t, w.shape[1]), BF16) for w in ws],
        compiler_params=_params("parallel"),
    )(x, *gains, *ws)


def _matmul(a, w, *, name, trans_b=False, res=None, out_dtype=F32, tm=1024, tn=1024, tk=1024):
    t, k = a.shape
    n = w.shape[0] if trans_b else w.shape[1]
    tm, tn, tk = min(tm, t), min(tn, n), min(tk, k)
    nk = k // tk

    def body(*refs):
        if res is None:
            a_ref, w_ref, o_ref, acc = refs
        else:
            a_ref, w_ref, r_ref, o_ref, acc = refs
        kk = pl.program_id(2)
        av = a_ref[...].astype(BF16)
        if trans_b:
            prod = lax.dot_general(av, w_ref[...], (((1,), (1,)), ((), ())), preferred_element_type=F32)
        else:
            prod = jnp.dot(av, w_ref[...], preferred_element_type=F32)
        if nk == 1:
            if res is not None:
                prod = prod + r_ref[...]
            o_ref[...] = prod.astype(o_ref.dtype)
            return

        @pl.when(kk == 0)
        def _():
            acc[...] = jnp.zeros_like(acc)

        acc[...] += prod

        @pl.when(kk == nk - 1)
        def _():
            r = acc[...]
            if res is not None:
                r = r + r_ref[...]
            o_ref[...] = r.astype(o_ref.dtype)

    in_specs = [pl.BlockSpec((tm, tk), lambda i, j, kk: (i, kk)),
                pl.BlockSpec((tn, tk), lambda i, j, kk: (j, kk)) if trans_b
                else pl.BlockSpec((tk, tn), lambda i, j, kk: (kk, j))]
    args = [a, w]
    if res is not None:
        in_specs.append(pl.BlockSpec((tm, tn), lambda i, j, kk: (i, j)))
        args.append(res)
    return pl.pallas_call(
        body, name=name, grid=(t // tm, n // tn, nk),
        in_specs=in_specs,
        out_specs=pl.BlockSpec((tm, tn), lambda i, j, kk: (i, j)),
        out_shape=jax.ShapeDtypeStruct((t, n), out_dtype),
        scratch_shapes=[pltpu.VMEM((tm, tn), F32)],
        compiler_params=_params("parallel", "parallel", "arbitrary"),
    )(*args)


def _matmul_tn(a, b, *, name, gain=None, block_cols=None, tt=1024, tn=1024):
    t, k = a.shape
    n = b.shape[1]
    tt, tn = min(tt, t), min(tn, n)
    nt = t // tt
    per_step = tn // block_cols if block_cols else 1

    def body(*refs):
        if gain is None:
            a_ref, b_ref, o_ref, acc = refs
            av = a_ref[...].astype(BF16)
        else:
            a_ref, g_ref, b_ref, o_ref, acc = refs
            xn, _ = _rms_normalize(a_ref[...])
            av = (xn * g_ref[...]).astype(BF16)

        @pl.when(pl.program_id(1) == 0)
        def _():
            acc[...] = jnp.zeros_like(acc)

        acc[...] += lax.dot_general(av, b_ref[...].astype(BF16), (((0,), (0,)), ((), ())),
                                    preferred_element_type=F32)

        @pl.when(pl.program_id(1) == nt - 1)
        def _():
            if block_cols:
                for c in range(per_step):
                    o_ref[c] = acc[:, c * block_cols:(c + 1) * block_cols].astype(o_ref.dtype)
            else:
                o_ref[...] = acc[...].astype(o_ref.dtype)

    in_specs = [pl.BlockSpec((tt, k), lambda j, s: (s, 0))]
    args = [a]
    if gain is not None:
        in_specs.append(pl.BlockSpec((1, k), lambda j, s: (0, 0)))
        args.append(gain)
    in_specs.append(pl.BlockSpec((tt, tn), lambda j, s: (s, j)))
    args.append(b)
    if block_cols:
        out_spec = pl.BlockSpec((per_step, k, block_cols), lambda j, s: (j, 0, 0))
        out_shape = jax.ShapeDtypeStruct((n // block_cols, k, block_cols), BF16)
    else:
        out_spec = pl.BlockSpec((k, tn), lambda j, s: (0, j))
        out_shape = jax.ShapeDtypeStruct((k, n), BF16)
    return pl.pallas_call(
        body, name=name, grid=(n // tn, nt),
        in_specs=in_specs, out_specs=out_spec, out_shape=out_shape,
        scratch_shapes=[pltpu.VMEM((k, tn), F32)],
        compiler_params=_params("parallel", "arbitrary"),
    )(*args)


HALO = 16


def _shift_down(v, halo, k):
    rows = lax.broadcasted_iota(jnp.int32, v.shape, 0)
    out = pltpu.roll(v, k, 0)
    for r in range(k):
        out = jnp.where(rows == r, halo[HALO - k + r:HALO - k + r + 1, :], out)
    return out


def _shift_up(v, halo, k):
    n = v.shape[0]
    rows = lax.broadcasted_iota(jnp.int32, v.shape, 0)
    out = pltpu.roll(v, n - k, 0)
    for r in range(k):
        out = jnp.where(rows == n - k + r, halo[r:r + 1, :], out)
    return out


def _mixer_fwd(u, conv, *, tm=512):
    t, d4 = u.shape
    d = d4 // 4
    tm = min(tm, t)
    hb = tm // HALO

    def body(b_ref, c_ref, x_ref, g_ref, ch_ref, xh_ref, w_ref, m_ref):
        i = pl.program_id(0)
        y1 = c_ref[...].astype(F32) * x_ref[...].astype(F32)
        prev = ch_ref[...].astype(F32) * xh_ref[...].astype(F32)
        prev = jnp.where(i == 0, 0.0, prev)
        w = w_ref[...]
        yc = w[2:3, :] * y1 + w[1:2, :] * _shift_down(y1, prev, 1) + w[0:1, :] * _shift_down(y1, prev, 2)
        g = g_ref[...].astype(F32)
        m_ref[...] = (b_ref[...].astype(F32) * yc * (g * _sigmoid(g))).astype(m_ref.dtype)

    def col(c):
        return pl.BlockSpec((tm, d), lambda i: (i, c))

    def prev_rows(c):
        return pl.BlockSpec((HALO, d), lambda i: (jnp.maximum(i * hb - 1, 0), c))

    return pl.pallas_call(
        body, name="mixer_fwd", grid=(t // tm,),
        in_specs=[col(0), col(1), col(2), col(3), prev_rows(1), prev_rows(2), pl.BlockSpec((3, d), lambda i: (0, 0))],
        out_specs=pl.BlockSpec((tm, d), lambda i: (i, 0)),
        out_shape=jax.ShapeDtypeStruct((t, d), BF16),
        compiler_params=_params("parallel"),
    )(u, u, u, u, u, u, conv)


def _mixer_bwd(u, dm, conv, *, tm=512):
    t, d4 = u.shape
    d = d4 // 4
    tm = min(tm, t)
    hb = tm // HALO
    nb = t // tm

    def body(b_ref, c_ref, x_ref, g_ref, dm_ref, ch_ref, xh_ref, bn_ref, gn_ref, dmn_ref, w_ref, du_ref, gw_ref):
        i = pl.program_id(0)
        w = w_ref[...]
        b = b_ref[...].astype(F32)
        c = c_ref[...].astype(F32)
        xin = x_ref[...].astype(F32)
        g = g_ref[...].astype(F32)
        dm_v = dm_ref[...]
        y1 = c * xin
        prev = jnp.where(i == 0, 0.0, ch_ref[...].astype(F32) * xh_ref[...].astype(F32))
        y1m1 = _shift_down(y1, prev, 1)
        y1m2 = _shift_down(y1, prev, 2)
        yc = w[2:3, :] * y1 + w[1:2, :] * y1m1 + w[0:1, :] * y1m2
        sg = _sigmoid(g)
        s = g * sg
        ds = sg * (1.0 + g * (1.0 - sg))
        dyc = dm_v * b * s
        gn = gn_ref[...].astype(F32)
        nxt = dmn_ref[...] * bn_ref[...].astype(F32) * (gn * _sigmoid(gn))
        nxt = jnp.where(i == nb - 1, 0.0, nxt)
        dy1 = w[2:3, :] * dyc + w[1:2, :] * _shift_up(dyc, nxt, 1) + w[0:1, :] * _shift_up(dyc, nxt, 2)
        du_ref[:, 0:d] = (dm_v * yc * s).astype(du_ref.dtype)
        du_ref[:, d:2 * d] = (dy1 * xin).astype(du_ref.dtype)
        du_ref[:, 2 * d:3 * d] = (dy1 * c).astype(du_ref.dtype)
        du_ref[:, 3 * d:4 * d] = (dm_v * b * yc * ds).astype(du_ref.dtype)

        @pl.when(i == 0)
        def _():
            gw_ref[...] = jnp.zeros_like(gw_ref)

        gw_ref[0:1, :] += jnp.sum(dyc * y1m2, axis=0, keepdims=True)
        gw_ref[1:2, :] += jnp.sum(dyc * y1m1, axis=0, keepdims=True)
        gw_ref[2:3, :] += jnp.sum(dyc * y1, axis=0, keepdims=True)

    def col(c):
        return pl.BlockSpec((tm, d), lambda i: (i, c))

    def prev_rows(c):
        return pl.BlockSpec((HALO, d), lambda i: (jnp.maximum(i * hb - 1, 0), c))

    def next_rows(c):
        return pl.BlockSpec((HALO, d), lambda i: (jnp.minimum((i + 1) * hb, nb * hb - 1), c))

    return pl.pallas_call(
        body, name="mixer_bwd", grid=(nb,),
        in_specs=[col(0), col(1), col(2), col(3), pl.BlockSpec((tm, d), lambda i: (i, 0)),
                  prev_rows(1), prev_rows(2), next_rows(0), next_rows(3),
                  pl.BlockSpec((HALO, d), lambda i: (jnp.minimum((i + 1) * hb, nb * hb - 1), 0)),
                  pl.BlockSpec((3, d), lambda i: (0, 0))],
        out_specs=[pl.BlockSpec((tm, d4), lambda i: (i, 0)), pl.BlockSpec((SMALL_ROWS, d), lambda i: (0, 0))],
        out_shape=[jax.ShapeDtypeStruct((t, d4), BF16), jax.ShapeDtypeStruct((SMALL_ROWS, d), F32)],
        compiler_params=_params("arbitrary"),
    )(u, u, u, u, dm, u, u, u, u, dm, conv)


def _out_proj_b_loss(o, qg, w_out, x1, gain, target, *, tm=512):
    t, a = o.shape
    d = x1.shape[1]
    tm = min(tm, t)

    def body(o_ref, g_ref, w_ref, x_ref, gain_ref, y_ref, gated_ref, dx_ref, loss_ref, gg_ref):
        @pl.when(pl.program_id(0) == 0)
        def _():
            loss_ref[...] = jnp.zeros_like(loss_ref)
            gg_ref[...] = jnp.zeros_like(gg_ref)

        g = g_ref[...].astype(F32)
        gated = (o_ref[...] * (g * _sigmoid(g))).astype(BF16)
        gated_ref[...] = gated
        x2 = x_ref[...] + jnp.dot(gated, w_ref[...], preferred_element_type=F32)
        xn, rstd = _rms_normalize(x2)
        gv = gain_ref[...]
        err = xn * gv - y_ref[...]
        loss_ref[...] += jnp.sum(err * err, axis=0, keepdims=True) * (0.5 / d)
        dy = err * (1.0 / d)
        gg_ref[...] += jnp.sum(dy * xn, axis=0, keepdims=True)
        dxn = dy * gv
        dx_ref[...] = rstd * (dxn - xn * jnp.mean(dxn * xn, axis=-1, keepdims=True))

    blk = pl.BlockSpec((tm, d), lambda i: (i, 0))
    blk_a = pl.BlockSpec((tm, a), lambda i: (i, 0))
    row = pl.BlockSpec((1, d), lambda i: (0, 0))
    return pl.pallas_call(
        body, name="out_proj_b_loss", grid=(t // tm,),
        in_specs=[blk_a, pl.BlockSpec((tm, a), lambda i: (i, 1)), pl.BlockSpec(w_out.shape, lambda i: (0, 0)),
                  blk, row, blk],
        out_specs=[blk_a, blk, row, row],
        out_shape=[jax.ShapeDtypeStruct((t, a), BF16), jax.ShapeDtypeStruct((t, d), F32),
                   jax.ShapeDtypeStruct((1, d), F32), jax.ShapeDtypeStruct((1, d), F32)],
        compiler_params=_params("arbitrary"),
    )(o, qg, w_out, x1, gain, target)


def _out_proj_b_bwd(dx2, w_out, o, qg, *, tm=512):
    t, a = o.shape
    d = dx2.shape[1]
    tm = min(tm, t)

    def body(dx_ref, w_ref, o_ref, g_ref, do_ref, dg_ref):
        da = _nt(dx_ref[...].astype(BF16), w_ref[...])
        g = g_ref[...].astype(F32)
        sg = _sigmoid(g)
        do_ref[...] = da * (g * sg)
        dg_ref[...] = (da * o_ref[...] * (sg * (1.0 + g * (1.0 - sg)))).astype(dg_ref.dtype)

    blk_a = pl.BlockSpec((tm, a), lambda i: (i, 0))
    return pl.pallas_call(
        body, name="out_proj_b_bwd", grid=(t // tm,),
        in_specs=[pl.BlockSpec((tm, d), lambda i: (i, 0)), pl.BlockSpec(w_out.shape, lambda i: (0, 0)), blk_a,
                  pl.BlockSpec((tm, a), lambda i: (i, 1))],
        out_specs=[blk_a, blk_a],
        out_shape=[jax.ShapeDtypeStruct((t, a), F32), jax.ShapeDtypeStruct((t, a), BF16)],
        compiler_params=_params("parallel"),
    )(dx2, w_out, o, qg)


def _proj_rms_bwd(x, cots, ws, gains, dres, *, name, tm=512):
    t, d = x.shape
    tm = min(tm, t)
    nk = len(cots)

    def body(*refs):
        x_ref = refs[0]
        cot_refs = refs[1:1 + nk]
        w_refs = refs[1 + nk:1 + 2 * nk]
        g_refs = refs[1 + 2 * nk:1 + 3 * nk]
        dres_ref = refs[1 + 3 * nk]
        dx_ref = refs[2 + 3 * nk]
        gg_refs = refs[3 + 3 * nk:]

        @pl.when(pl.program_id(0) == 0)
        def _():
            for gg in gg_refs:
                gg[...] = jnp.zeros_like(gg)

        xn, rstd = _rms_normalize(x_ref[...])
        dxn = None
        for cot_ref, w_ref, g_ref, gg in zip(cot_refs, w_refs, g_refs, gg_refs):
            dh = _nt(cot_ref[...].astype(BF16), w_ref[...])
            gg[...] += jnp.sum(dh * xn, axis=0, keepdims=True)
            term = dh * g_ref[...]
            dxn = term if dxn is None else dxn + term
        dx_ref[...] = dres_ref[...] + rstd * (dxn - xn * jnp.mean(dxn * xn, axis=-1, keepdims=True))

    blk = pl.BlockSpec((tm, d), lambda i: (i, 0))
    row = pl.BlockSpec((1, d), lambda i: (0, 0))
    return pl.pallas_call(
        body, name=name, grid=(t // tm,),
        in_specs=([blk] + [pl.BlockSpec((tm, c.shape[1]), lambda i: (i, 0)) for c in cots]
                  + [pl.BlockSpec(w.shape, lambda i: (0, 0)) for w in ws] + [row] * nk + [blk]),
        out_specs=[blk] + [row] * nk,
        out_shape=[jax.ShapeDtypeStruct((t, d), F32)] + [jax.ShapeDtypeStruct((1, d), F32)] * nk,
        compiler_params=_params("arbitrary"),
    )(x, *cots, *ws, *gains, dres)


def _suffix_ones(n):
    r = lax.broadcasted_iota(jnp.int32, (n, n), 0)
    c = lax.broadcasted_iota(jnp.int32, (n, n), 1)
    return (r >= c).astype(BF16)


def _suffix_sum(l, ones):
    return jnp.dot(l.astype(BF16), ones, preferred_element_type=F32)


def _log_one_minus_beta(z):
    return -(jnp.maximum(z, 0.0) + jnp.log(1.0 + jnp.exp(-jnp.abs(z))))


def _nt(a, b):
    return lax.dot_general(a, b, (((1,), (1,)), ((), ())), preferred_element_type=F32)


def _tn(a, b):
    return lax.dot_general(a, b, (((0,), (0,)), ((), ())), preferred_element_type=F32)


EXP_UNDERFLOW = 104.0
NORM_MARGIN = 1.01


def _key_norm_bounds(kv, *, tq):
    t, a2 = kv.shape
    a = a2 // 2
    heads = a // HEAD_DIM
    nq = t // tq

    per_step = min(4, nq)
    lanes = 128

    def body(k_ref, sel_ref, o_ref):
        k = k_ref[...].astype(F32)
        n2 = jnp.dot((k * k).astype(BF16), sel_ref[...], preferred_element_type=F32)
        o_ref[...] = jnp.sqrt(jnp.max(n2.reshape(per_step, tq, lanes), axis=1))

    head_of_col = lax.broadcasted_iota(jnp.int32, (a, lanes), 0) // HEAD_DIM
    sel = (head_of_col == lax.broadcasted_iota(jnp.int32, (a, lanes), 1)).astype(BF16)
    norms = pl.pallas_call(
        body, name="key_norms", grid=(nq // per_step,),
        in_specs=[pl.BlockSpec((per_step * tq, a), lambda i: (i, 0)), pl.BlockSpec((a, lanes), lambda i: (0, 0))],
        out_specs=pl.BlockSpec((None, per_step, lanes), lambda i: (i, 0, 0)),
        out_shape=jax.ShapeDtypeStruct((nq // per_step, per_step, lanes), F32),
        compiler_params=_params("parallel"),
    )(kv, sel)
    return lax.cummax(norms.reshape(nq, lanes)[:, :heads], axis=0).T.reshape(-1)


def _attention_fwd(qg, kv, *, tq=256, n_sub=1, hg=4):
    t, a2 = qg.shape
    a = a2 // 2
    hg = hg if a % (hg * HEAD_DIM) == 0 else 2
    gw = hg * HEAD_DIM
    ngroup = a // gw
    tq = min(tq, t)
    scale = HEAD_DIM ** -0.5
    nq = t // tq
    ts = tq // n_sub

    def body(kmax_ref, q_ref, k_ref, v_ref, ones_ref, o_ref, cin_ref, first_ref):
        p = pl.program_id(0)
        i = pl.program_id(1)
        ones = ones_ref[...]
        q_all = q_ref[...] * jnp.asarray(scale, BF16)
        q2 = q_all.astype(F32) * q_all.astype(F32)
        chains = [(h, r) for h in range(hg) for r in range(n_sub)]
        qs = [q_all[r * ts:(r + 1) * ts, h * HEAD_DIM:(h + 1) * HEAD_DIM] for h, r in chains]
        q_norm = [jnp.sqrt(jnp.sum(q2[r * ts:(r + 1) * ts, h * HEAD_DIM:(h + 1) * HEAD_DIM], axis=1, keepdims=True))
                  * NORM_MARGIN for h, r in chains]
        rows = lax.broadcasted_iota(jnp.int32, (ts, tq), 0)
        cols = lax.broadcasted_iota(jnp.int32, (ts, tq), 1)
        causal = [cols < rows + r * ts for r in range(n_sub)]
        block_of_lane = lax.broadcasted_iota(jnp.int32, (ts, nq), 1)
        cin_ref[...] = jnp.zeros_like(cin_ref)

        def any_weight_left_of(j, carry):
            jj = jnp.maximum(j - 1, 0)
            bound = None
            for n, (h, r) in enumerate(chains):
                top = jnp.max(carry[n][0] + q_norm[n] * kmax_ref[(hg * p + h) * nq + jj])
                bound = top if bound is None else jnp.maximum(bound, top)
            return bound > -EXP_UNDERFLOW

        def tile(j, carry, masked):
            start = pl.multiple_of(j * tq, tq)
            k_all = k_ref[pl.ds(start, tq), :]
            v_all = v_ref[pl.ds(start, tq), :]
            ks = [k_all[:, h * HEAD_DIM:(h + 1) * HEAD_DIM] for h in range(hg)]
            vs = [v_all[:, h * HEAD_DIM:(h + 1) * HEAD_DIM] for h in range(hg)]
            zs = [_nt(qs[n], ks[h]) for n, (h, r) in enumerate(chains)]
            ls = [_log_one_minus_beta(z) for z in zs]
            if masked:
                ls = [jnp.where(causal[r], l, 0.0) for l, (h, r) in zip(ls, chains)]
            r_locs = [_suffix_sum(l, ones) for l in ls]
            logws = [z + (carry[n][0] + r_loc) for n, (z, r_loc) in enumerate(zip(zs, r_locs))]
            if masked:
                logws = [jnp.where(causal[r], lw, -jnp.inf) for lw, (h, r) in zip(logws, chains)]
            ws = [jnp.exp(lw).astype(BF16) for lw in logws]
            out = []
            for n, (h, r) in enumerate(chains):
                acc = carry[n][1] + jnp.dot(ws[n], vs[h], preferred_element_type=F32)
                c = carry[n][0] + r_locs[n][:, 0:1]
                cin_ref[h, r * ts:(r + 1) * ts, :] = jnp.where(block_of_lane == j - 1, c,
                                                               cin_ref[h, r * ts:(r + 1) * ts, :])
                out.append((c, acc))
            return tuple(out)

        init = tuple((jnp.zeros((ts, 1), F32), jnp.zeros((ts, HEAD_DIM), F32)) for _ in chains)
        carry = tile(i, init, True)

        def visit(state):
            j, _, cr = state
            cr = tile(j - 1, cr, False)
            return j - 1, any_weight_left_of(j - 1, cr), cr

        first, _, carry = lax.while_loop(lambda s: (s[0] > 0) & s[1], visit,
                                         (i, any_weight_left_of(i, carry), carry))
        first_ref[p * nq + i] = first
        heads = [jnp.concatenate([carry[h * n_sub + r][1] for r in range(n_sub)], axis=0) for h in range(hg)]
        o_ref[...] = jnp.concatenate(heads, axis=1)

    return pl.pallas_call(
        body, name="attention_fwd",
        grid_spec=pltpu.PrefetchScalarGridSpec(
            num_scalar_prefetch=1, grid=(ngroup, nq),
            in_specs=[pl.BlockSpec((tq, gw), lambda p, i, km: (i, p)),
                      pl.BlockSpec((t, gw), lambda p, i, km: (0, p)),
                      pl.BlockSpec((t, gw), lambda p, i, km: (0, ngroup + p)),
                      pl.BlockSpec((tq, tq), lambda p, i, km: (0, 0))],
            out_specs=[pl.BlockSpec((tq, gw), lambda p, i, km: (i, p)),
                       pl.BlockSpec((None, hg, tq, nq), lambda p, i, km: (p, 0, i, 0)),
                       pl.BlockSpec(memory_space=pltpu.SMEM)]),
        out_shape=[jax.ShapeDtypeStruct((t, a), F32), jax.ShapeDtypeStruct((ngroup, hg, t, nq), F32),
                   jax.ShapeDtypeStruct((ngroup * nq,), jnp.int32)],
        compiler_params=_params("arbitrary", "arbitrary"),
    )(_key_norm_bounds(kv, tq=tq), qg, kv, kv, _suffix_ones(tq))


def _attention_bwd(qg, kv, cin, first, do, *, tq=256, hg=4):
    t, a2 = qg.shape
    a = a2 // 2
    hg = hg if a % (hg * HEAD_DIM) == 0 else 2
    gw = hg * HEAD_DIM
    ngroup = a // gw
    tq = min(tq, t)
    nq = t // tq
    scale = HEAD_DIM ** -0.5

    def body(first_ref, q_ref, kv_ref, cin_ref, do_ref, ones_ref, ones_t_ref, dq_ref, dkv_ref,
             k_ref, v_ref, dk_acc, dv_acc, sem):
        p = pl.program_id(0)
        i = pl.program_id(1)
        first = jnp.clip(first_ref[p * nq + i], 0, i)
        k_cols = pl.multiple_of(p * gw, gw)
        v_cols = pl.multiple_of((ngroup + p) * gw, gw)

        @pl.when(i == 0)
        def _():
            loads = [pltpu.make_async_copy(kv_ref.at[:, pl.ds(k_cols, gw)], k_ref, sem.at[0]),
                     pltpu.make_async_copy(kv_ref.at[:, pl.ds(v_cols, gw)], v_ref, sem.at[1])]
            for cp in loads:
                cp.start()
            dk_acc[...] = jnp.zeros_like(dk_acc)
            dv_acc[...] = jnp.zeros_like(dv_acc)
            for cp in loads:
                cp.wait()

        ones = ones_ref[...]
        ones_t = ones_t_ref[...]
        q_all = q_ref[...] * jnp.asarray(scale, BF16)
        do_bf = do_ref[...].astype(BF16)
        qs = [q_all[:, h * HEAD_DIM:(h + 1) * HEAD_DIM] for h in range(hg)]
        dos = [do_bf[:, h * HEAD_DIM:(h + 1) * HEAD_DIM] for h in range(hg)]
        rows = lax.broadcasted_iota(jnp.int32, (tq, tq), 0)
        cols = lax.broadcasted_iota(jnp.int32, (tq, tq), 1)
        causal = cols < rows
        block_of_lane = lax.broadcasted_iota(jnp.int32, (tq, nq), 1)

        def tile(j, carry, masked):
            start = pl.multiple_of(j * tq, tq)
            k_all = k_ref[pl.ds(start, tq), :]
            v_all = v_ref[pl.ds(start, tq), :]
            hs = range(hg)
            ks = [k_all[:, h * HEAD_DIM:(h + 1) * HEAD_DIM] for h in hs]
            vs = [v_all[:, h * HEAD_DIM:(h + 1) * HEAD_DIM] for h in hs]
            zs = [_nt(qs[h], ks[h]) for h in hs]
            das = [_nt(dos[h], vs[h]) for h in hs]
            ls = [_log_one_minus_beta(z) for z in zs]
            if masked:
                ls = [jnp.where(causal, l, 0.0) for l in ls]
            betas = [jnp.exp(z + l) for z, l in zip(zs, ls)]
            r_locs = [_suffix_sum(l, ones) for l in ls]
            if masked:
                logws = [jnp.where(causal, z + r_loc, -jnp.inf) for z, r_loc in zip(zs, r_locs)]
            else:
                cs = [jnp.sum(jnp.where(block_of_lane == j, cin_ref[h], 0.0), axis=1, keepdims=True) for h in hs]
                logws = [z + (c + r_loc) for z, c, r_loc in zip(zs, cs, r_locs)]
            ws = [jnp.exp(lw) for lw in logws]
            gs = [w * da for w, da in zip(ws, das)]
            g_pres = [jnp.dot(g.astype(BF16), ones_t, preferred_element_type=F32) for g in gs]
            dvs = [_tn(ws[h].astype(BF16), dos[h]) for h in hs]
            dzs = [gs[h] - betas[h] * (carry[h][0] + g_pres[h]) for h in hs]
            if masked:
                dzs = [jnp.where(causal, dz, 0.0) for dz in dzs]
            dzs = [dz.astype(BF16) for dz in dzs]
            dqs = [carry[h][1] + jnp.dot(dzs[h], ks[h], preferred_element_type=F32) for h in hs]
            dks = [_tn(dzs[h], qs[h]) for h in hs]
            dk_acc[pl.ds(start, tq), :] += jnp.concatenate(dks, axis=1)
            dv_acc[pl.ds(start, tq), :] += jnp.concatenate(dvs, axis=1)
            return tuple((carry[h][0] + g_pres[h][:, tq - 1:tq], dqs[h]) for h in hs)

        init = tuple((jnp.zeros((tq, 1), F32), jnp.zeros((tq, HEAD_DIM), F32)) for _ in range(hg))
        carry = lax.fori_loop(first, i, lambda j, cr: tile(j, cr, False), init)
        carry = tile(i, carry, True)
        dq_ref[...] = (jnp.concatenate([carry[h][1] for h in range(hg)], axis=1) * scale).astype(dq_ref.dtype)

        @pl.when(i == nq - 1)
        def _():
            ck = pltpu.make_async_copy(dk_acc, dkv_ref.at[:, pl.ds(k_cols, gw)], sem.at[0])
            cv = pltpu.make_async_copy(dv_acc, dkv_ref.at[:, pl.ds(v_cols, gw)], sem.at[1])
            ck.start()
            cv.start()
            ck.wait()
            cv.wait()

    blk = pl.BlockSpec((tq, gw), lambda p, i, fr: (i, p))
    tri = pl.BlockSpec((tq, tq), lambda p, i, fr: (0, 0))
    ones = _suffix_ones(tq)
    return pl.pallas_call(
        body, name="attention_bwd",
        grid_spec=pltpu.PrefetchScalarGridSpec(
            num_scalar_prefetch=1, grid=(ngroup, nq),
            in_specs=[blk, pl.BlockSpec(memory_space=pl.ANY),
                      pl.BlockSpec((None, hg, tq, nq), lambda p, i, fr: (p, 0, i, 0)),
                      blk, tri, tri],
            out_specs=[blk, pl.BlockSpec(memory_space=pl.ANY)],
            scratch_shapes=[pltpu.VMEM((t, gw), BF16), pltpu.VMEM((t, gw), BF16),
                            pltpu.VMEM((t, gw), F32), pltpu.VMEM((t, gw), F32),
                            pltpu.SemaphoreType.DMA((2,))]),
        out_shape=[jax.ShapeDtypeStruct((t, a), BF16), jax.ShapeDtypeStruct((t, a2), F32)],
        compiler_params=pltpu.CompilerParams(dimension_semantics=("arbitrary", "arbitrary"),
                                             vmem_limit_bytes=VMEM_LIMIT_ATTENTION_BWD_V7X),
    )(first, qg, kv, cin, do, ones, ones.T)


def _other_chips(x, y):
    return [(1 - x, y), (x, 1 - y), (1 - x, 1 - y)]


def _gather_blocks(wide, small):
    half = wide.shape[0] // 2

    def body(w_ref, s_ref, wo_ref, so_ref, send_sems, recv_sems, local_sems, pass_send, pass_recv):
        x, y, c = lax.axis_index("x"), lax.axis_index("y"), lax.axis_index("c")
        mine = 2 * x + y
        my_rows = pl.ds(pl.multiple_of(c * half, half), half)
        sibling_rows = pl.ds(pl.multiple_of((1 - c) * half, half), half)
        srcs = (w_ref.at[my_rows], s_ref)
        local, sends, recvs = _block_exchange(
            2, lambda k, slot: srcs[k], lambda k, slot: (wo_ref.at[slot, my_rows], so_ref.at[slot])[k],
            send_sems, recv_sems, local_sems)
        local.append(pltpu.make_async_copy(w_ref.at[sibling_rows], wo_ref.at[mine, sibling_rows], local_sems.at[2]))
        for cp in local + sends:
            cp.start()

        def passed(p, slot, rows):
            return pltpu.make_async_remote_copy(
                src_ref=wo_ref.at[slot, rows], dst_ref=wo_ref.at[slot, rows], send_sem=pass_send.at[p],
                recv_sem=pass_recv.at[p], device_id=(x, y, 1 - c), device_id_type=MESH)

        slots = [2 * px + py for px, py in _other_chips(x, y)]
        passes = []
        for p, slot in enumerate(slots):
            recvs[2 * p].wait_recv()
            passes.append(passed(p, slot, my_rows))
            passes[-1].start()
        for p, slot in enumerate(slots):
            recvs[2 * p + 1].wait_recv()
            passed(p, slot, sibling_rows).wait_recv()
        for cp in sends + passes:
            cp.wait_send()
        for cp in local:
            cp.wait()

    any_spec = pl.BlockSpec(memory_space=pl.ANY)
    return pl.pallas_call(
        body, name="gather_weights",
        in_specs=[any_spec, any_spec], out_specs=[any_spec, any_spec],
        out_shape=[jax.ShapeDtypeStruct((N_BLOCKS,) + wide.shape, wide.dtype),
                   jax.ShapeDtypeStruct((N_BLOCKS,) + small.shape, small.dtype)],
        scratch_shapes=[pltpu.SemaphoreType.DMA((6,)), pltpu.SemaphoreType.DMA((6,)), pltpu.SemaphoreType.DMA((3,)),
                        pltpu.SemaphoreType.DMA((3,)), pltpu.SemaphoreType.DMA((3,))],
    )(wide, small)


def _norm_matmul_gather(x, gain, w, rest, *, tm=1024, tn=1024):
    t, d = x.shape
    n = w.shape[1]
    tm, tn = min(tm, t), min(tn, n)
    ni, nj = t // tm, n // tn

    def body(x_ref, g_ref, w_ref, rest_ref, o_ref, all_ref, send_sems, recv_sems, local_sems):
        i, j = pl.program_id(0), pl.program_id(1)
        copies = _block_exchange(1, lambda k, slot: rest_ref, lambda k, slot: all_ref.at[slot],
                                 send_sems, recv_sems, local_sems)

        @pl.when((i == 0) & (j == 0))
        def _():
            for cp in copies[0] + copies[1]:
                cp.start()

        xn, _ = _rms_normalize(x_ref[...])
        h = (xn * g_ref[...]).astype(BF16)
        o_ref[...] = jnp.dot(h, w_ref[...], preferred_element_type=F32).astype(o_ref.dtype)

        @pl.when((i == ni - 1) & (j == nj - 1))
        def _():
            _finish_exchange(*copies)

    any_spec = pl.BlockSpec(memory_space=pl.ANY)
    return pl.pallas_call(
        body, name="in_proj_a_gather", grid=(ni, nj),
        in_specs=[pl.BlockSpec((tm, d), lambda i, j: (i, 0)),
                  pl.BlockSpec((1, d), lambda i, j: (0, 0)),
                  pl.BlockSpec((d, tn), lambda i, j: (0, j)), any_spec],
        out_specs=[pl.BlockSpec((tm, tn), lambda i, j: (i, j)), any_spec],
        out_shape=[jax.ShapeDtypeStruct((t, n), BF16), jax.ShapeDtypeStruct((N_BLOCKS,) + rest.shape, rest.dtype)],
        scratch_shapes=_exchange_sems(1),
        compiler_params=_params("arbitrary", "arbitrary"),
    )(x, gain, w, rest)


def _block_exchange(n, source, landing, send_sems, recv_sems, local_sems):
    x, y, c = lax.axis_index("x"), lax.axis_index("y"), lax.axis_index("c")
    mine = 2 * x + y
    local = [pltpu.make_async_copy(source(k, mine), landing(k, mine), local_sems.at[k]) for k in range(n)]
    sends, recvs = [], []
    for p, (px, py) in enumerate(_other_chips(x, y)):
        for k in range(n):
            sems = dict(send_sem=send_sems.at[p * n + k], recv_sem=recv_sems.at[p * n + k],
                        device_id=(px, py, c), device_id_type=MESH)
            sends.append(pltpu.make_async_remote_copy(src_ref=source(k, 2 * px + py), dst_ref=landing(k, mine),
                                                      **sems))
            recvs.append(pltpu.make_async_remote_copy(src_ref=source(k, mine), dst_ref=landing(k, 2 * px + py),
                                                      **sems))
    return local, sends, recvs


def _run_exchange(local, sends, recvs):
    for cp in local + sends:
        cp.start()
    _finish_exchange(local, sends, recvs)


def _finish_exchange(local, sends, recvs):
    for cp in recvs:
        cp.wait_recv()
    for cp in sends:
        cp.wait_send()
    for cp in local:
        cp.wait()


def _exchange_sems(n):
    return [pltpu.SemaphoreType.DMA((3 * n,)), pltpu.SemaphoreType.DMA((3 * n,)), pltpu.SemaphoreType.DMA((n,))]


def _scatter_small(small):
    def body(s_ref, out_ref, send_sems, recv_sems, local_sems):
        _run_exchange(*_block_exchange(1, lambda k, slot: s_ref.at[slot], lambda k, slot: out_ref.at[slot],
                                       send_sems, recv_sems, local_sems))

    any_spec = pl.BlockSpec(memory_space=pl.ANY)
    return pl.pallas_call(
        body, name="scatter_small",
        in_specs=[any_spec], out_specs=any_spec,
        out_shape=jax.ShapeDtypeStruct(small.shape, F32),
        scratch_shapes=_exchange_sems(1),
    )(small)


def _rms_bwd_a_scatter(du, w_in, x, dx1, gain, wide, *, tm=512):
    t, d = x.shape
    tm = min(tm, t)
    nb = t // tm
    n = len(wide)
    offsets = [sum(w.shape[1] for w in wide[:k]) for k in range(n)]
    n_wide = sum(w.shape[1] for w in wide)

    def body(*refs):
        du_ref, w_ref, x_ref, dx1_ref, g_ref = refs[:5]
        wide_refs = refs[5:5 + n]
        dx_ref, gg_ref, land_ref, send_sems, recv_sems, local_sems = refs[5 + n:]
        i = pl.program_id(0)
        copies = _block_exchange(
            n, lambda k, slot: wide_refs[k].at[slot],
            lambda k, slot: land_ref.at[slot, pl.ds(offsets[k], wide[k].shape[1])],
            send_sems, recv_sems, local_sems)

        @pl.when(i == 0)
        def _():
            gg_ref[...] = jnp.zeros_like(gg_ref)
            for cp in copies[0] + copies[1]:
                cp.start()

        dh = _nt(du_ref[...], w_ref[...])
        xn, rstd = _rms_normalize(x_ref[...])
        gg_ref[...] += jnp.sum(dh * xn, axis=0, keepdims=True)
        dxn = dh * g_ref[...]
        dx_ref[...] = dx1_ref[...] + rstd * (dxn - xn * jnp.mean(dxn * xn, axis=-1, keepdims=True))

        @pl.when(i == nb - 1)
        def _():
            _finish_exchange(*copies)

    blk = pl.BlockSpec((tm, d), lambda i: (i, 0))
    row = pl.BlockSpec((1, d), lambda i: (0, 0))
    any_spec = pl.BlockSpec(memory_space=pl.ANY)
    return pl.pallas_call(
        body, name="rms_bwd_a_scatter", grid=(nb,),
        in_specs=[pl.BlockSpec((tm, du.shape[1]), lambda i: (i, 0)),
                  pl.BlockSpec(w_in.shape, lambda i: (0, 0)), blk, blk, row] + [any_spec] * n,
        out_specs=[blk, row, any_spec],
        out_shape=[jax.ShapeDtypeStruct((t, d), F32), jax.ShapeDtypeStruct((1, d), F32),
                   jax.ShapeDtypeStruct((N_BLOCKS, n_wide, d), BF16)],
        scratch_shapes=_exchange_sems(n),
        compiler_params=_params("arbitrary"),
    )(du, w_in, x, dx1, gain, *wide)


def _swap_with_sibling(part):
    def body(p_ref, out_ref, send_sem, recv_sem):
        x, y, c = lax.axis_index("x"), lax.axis_index("y"), lax.axis_index("c")
        cp = pltpu.make_async_remote_copy(src_ref=p_ref, dst_ref=out_ref, send_sem=send_sem, recv_sem=recv_sem,
                                          device_id=(x, y, 1 - c), device_id_type=MESH)
        cp.start()
        cp.wait()

    any_spec = pl.BlockSpec(memory_space=pl.ANY)
    return pl.pallas_call(
        body, name="swap_sibling",
        in_specs=[any_spec], out_specs=any_spec,
        out_shape=jax.ShapeDtypeStruct(part.shape, part.dtype),
        scratch_shapes=[pltpu.SemaphoreType.DMA, pltpu.SemaphoreType.DMA],
    )(part)


def _sum_slots(wide, small, n_small):
    _, n_wide, d = wide.shape
    tr = n_small
    nw = n_wide // tr

    def body(w_ref, s_ref, o_ref):
        i = pl.program_id(0)

        @pl.when(i < nw)
        def _():
            w = w_ref[...].astype(F32)
            o_ref[...] = ((w[0] + w[1]) + w[2]) + w[3]

        @pl.when(i == nw)
        def _():
            o_ref[...] = jnp.zeros_like(o_ref)
            o_ref[0:SMALL_ROWS, :] = ((s_ref[0] + s_ref[1]) + s_ref[2]) + s_ref[3]

    return pl.pallas_call(
        body, name="sum_slots", grid=(nw + 1,),
        in_specs=[pl.BlockSpec((N_BLOCKS, tr, d), lambda i: (0, jnp.minimum(i, nw - 1), 0)),
                  pl.BlockSpec((N_BLOCKS, SMALL_ROWS, d), lambda i: (0, 0, 0))],
        out_specs=pl.BlockSpec((tr, d), lambda i: (i, 0)),
        out_shape=jax.ShapeDtypeStruct((n_wide + tr, d), F32),
        compiler_params=_params("arbitrary"),
    )(wide, small)


def _adamw(part_mine, part_sibling, row0, w, m, v, *, name):
    r, d = w.shape
    tr = next(c for c in (ROW_TILE, 128, 64, 32, 16, 8) if r % c == 0 and row0 % c == 0)
    first_block = row0 // tr
    m_scale = 1.0 / (1.0 - ADAM_B1 ** ADAM_STEP)
    v_scale = 1.0 / (1.0 - ADAM_B2 ** ADAM_STEP)

    def body(a_ref, b_ref, w_ref, m_ref, v_ref, g_ref, d_ref, mo_ref, vo_ref):
        g = a_ref[...] + b_ref[...]
        m_new = ADAM_B1 * m_ref[...] + (1.0 - ADAM_B1) * g
        v_new = ADAM_B2 * v_ref[...] + (1.0 - ADAM_B2) * (g * g)
        g_ref[...] = g
        mo_ref[...] = m_new
        vo_ref[...] = v_new
        d_ref[...] = -ADAM_LR * ((m_new * m_scale) / (jnp.sqrt(v_new * v_scale) + ADAM_EPS) + ADAM_WD * w_ref[...])

    blk = pl.BlockSpec((tr, d), lambda i: (i, 0))
    part = pl.BlockSpec((tr, d), lambda i: (first_block + i, 0))
    return pl.pallas_call(
        body, name=name, grid=(r // tr,),
        in_specs=[part, part, blk, blk, blk], out_specs=[blk] * 4,
        out_shape=[jax.ShapeDtypeStruct((r, d), F32)] * 4,
        compiler_params=_params("parallel"),
    )(part_mine, part_sibling, w, m, v)


def _wide_views(w_in_a, w_out_a, w_kv, w_in_b, w_out_b):
    d = w_in_a.shape[-1]
    return [w_in_a[0], w_out_a[0], w_kv, w_in_b[0], w_out_b[0].reshape(-1, d)]


WIDE = ["w_in_a", "w_out_a", "w_kv", "w_in_b", "w_out_b"]


def _small_rows(n_wide):
    return ROW_TILE - n_wide % ROW_TILE if n_wide % ROW_TILE else ROW_TILE


def _pack_small(conv_a, norm_a, norm_kv, norm_b, norm_f, rows):
    d = norm_kv.shape[-1]
    parts = [jnp.concatenate([conv_a[0].reshape(-1), norm_a[0]])[None, :], norm_kv.reshape(1, d),
             norm_b.reshape(1, d), norm_f.reshape(1, d), jnp.zeros((rows - 4, d), F32)]
    return jnp.concatenate(parts, axis=0)


def _unpack_small(packed, shapes):
    dq = packed.shape[1] // N_BLOCKS
    return dict(conv_a=packed[0, :3 * dq].reshape(shapes["conv_a"]), norm_a=packed[0, 3 * dq:].reshape(shapes["norm_a"]),
                norm_kv=packed[1].reshape(shapes["norm_kv"]), norm_b=packed[2].reshape(shapes["norm_b"]),
                norm_f=packed[3].reshape(shapes["norm_f"]))


WEIGHTS = ["norm_a", "w_in_a", "conv_a", "w_out_a", "norm_kv", "w_kv", "norm_b", "w_in_b", "w_out_b", "norm_f"]


def kernel(x, norm_a, w_in_a, conv_a, w_out_a, norm_kv, w_kv, norm_b, w_in_b, w_out_b, norm_f, loss_target, m_norm_a, m_w_in_a, m_conv_a, m_w_out_a, m_norm_kv, m_w_kv, m_norm_b, m_w_in_b, m_w_out_b, m_norm_f, v_norm_a, v_w_in_a, v_conv_a, v_w_out_a, v_norm_kv, v_w_kv, v_norm_b, v_w_in_b, v_w_out_b, v_norm_f):
    d = x.shape[-1]
    dq = d // N_BLOCKS
    att = w_out_b.shape[1]
    xs = x[0]
    target = loss_target[0]
    shapes = dict(norm_a=norm_a.shape, w_in_a=w_in_a.shape, conv_a=conv_a.shape, w_out_a=w_out_a.shape,
                  norm_kv=norm_kv.shape, w_kv=w_kv.shape, norm_b=norm_b.shape, w_in_b=w_in_b.shape,
                  w_out_b=w_out_b.shape, norm_f=norm_f.shape)

    w_wide = _wide_views(w_in_a, w_out_a, w_kv, w_in_b, w_out_b)
    m_wide = _wide_views(m_w_in_a, m_w_out_a, m_w_kv, m_w_in_b, m_w_out_b)
    v_wide = _wide_views(v_w_in_a, v_w_out_a, v_w_kv, v_w_in_b, v_w_out_b)
    n_wide = sum(w.shape[0] for w in w_wide)
    n_small = _small_rows(n_wide)
    w_small = _pack_small(conv_a, norm_a, norm_kv, norm_b, norm_f, n_small)
    m_small = _pack_small(m_conv_a, m_norm_a, m_norm_kv, m_norm_b, m_norm_f, n_small)
    v_small = _pack_small(v_conv_a, v_norm_a, v_norm_kv, v_norm_b, v_norm_f, n_small)
    rest_bf = jnp.concatenate(w_wide[1:], axis=0).astype(BF16)
    in_a_all, small_all = _gather_blocks(w_wide[0].astype(BF16), w_small[:SMALL_ROWS])
    wf_in_a = jnp.concatenate([in_a_all[b] for b in range(N_BLOCKS)], axis=1)
    conv_full = jnp.concatenate([small_all[b, 0, :3 * dq].reshape(3, dq) for b in range(N_BLOCKS)], axis=1)
    gain_a = jnp.concatenate([small_all[b, 0, 3 * dq:] for b in range(N_BLOCKS)])[None, :]
    gain_kv = norm_kv.reshape(1, d)
    gain_b = norm_b.reshape(1, d)
    gain_f = norm_f.reshape(1, d)

    u, rest_all = _norm_matmul_gather(xs, gain_a, wf_in_a, rest_bf, tm=512, tn=4 * d)
    o2, o3 = dq, 2 * dq
    wf_out_a = rest_all[:, :o2].reshape(d, d)
    wf_kv = rest_all[:, o2:o3].reshape(d, 2 * att)
    wf_in_b = rest_all[:, o3:o3 + dq].reshape(d, 2 * att)
    wf_out_b = jnp.concatenate([rest_all[b, o3 + dq:].reshape(att, dq) for b in range(N_BLOCKS)], axis=1)
    mix = _mixer_fwd(u, conv_full)
    x1 = _matmul(mix, wf_out_a, res=xs, name="out_proj_a")
    kv, qg = _norm_matmuls(x1, [gain_kv, gain_b], [wf_kv, wf_in_b], name="kv_in_proj_b")
    o, cin, first = _attention_fwd(qg, kv)
    gated, dx2, loss_cols, g_norm_f = _out_proj_b_loss(o, qg, wf_out_b, x1, gain_f, target)
    loss = lax.psum(jnp.sum(loss_cols), ("x", "y", "c"))

    g_w_out_b = _matmul_tn(gated, dx2, name="grad_w_out_b")
    do, dg = _out_proj_b_bwd(dx2, wf_out_b, o, qg)
    dq_att, dkv = _attention_bwd(qg, kv, cin, first, do)
    dqg = jnp.concatenate([dq_att, dg], axis=1)
    g_w_in_b = _matmul_tn(x1, dqg, gain=gain_b, name="grad_w_in_b")
    g_w_kv = _matmul_tn(x1, dkv, gain=gain_kv, name="grad_w_kv")
    dx1, g_norm_b, g_norm_kv = _proj_rms_bwd(x1, [dqg, dkv], [wf_in_b, wf_kv], [gain_b, gain_kv], dx2,
                                             name="rms_bwd_b")
    g_w_out_a = _matmul_tn(mix, dx1, name="grad_w_out_a")
    dmix = _matmul(dx1, wf_out_a, trans_b=True, name="d_mix")
    du, g_conv = _mixer_bwd(u, dmix, conv_full)
    g_w_in_a = _matmul_tn(xs, du, gain=gain_a, block_cols=d, tn=2 * d, tt=512, name="grad_w_in_a")
    wide_blocks = [g_w_in_a, g_w_out_a.reshape(N_BLOCKS, dq, d), g_w_kv.reshape(N_BLOCKS, dq, d),
                   g_w_in_b.reshape(N_BLOCKS, dq, d),
                   g_w_out_b.reshape(att, N_BLOCKS, dq).transpose(1, 0, 2).reshape(N_BLOCKS, att // N_BLOCKS, d)]
    dx, g_norm_a, wide_landed = _rms_bwd_a_scatter(du, wf_in_a, xs, dx1, gain_a, wide_blocks)
    small_rows = jnp.concatenate([
        jnp.concatenate([g_conv[:3].reshape(3, N_BLOCKS, dq).transpose(1, 0, 2).reshape(N_BLOCKS, 3 * dq),
                         g_norm_a.reshape(N_BLOCKS, dq)], axis=1)[:, None, :],
        jnp.broadcast_to(g_norm_kv[None], (N_BLOCKS, 1, d)),
        jnp.broadcast_to(g_norm_b[None], (N_BLOCKS, 1, d)),
        jnp.broadcast_to(g_norm_f[None], (N_BLOCKS, 1, d)),
        jnp.zeros((N_BLOCKS, SMALL_ROWS - 4, d), F32)], axis=1)

    mine = _sum_slots(wide_landed, _scatter_small(small_rows), n_small)
    theirs = _swap_with_sibling(mine)
    results = [{}, {}, {}, {}]
    row0 = 0
    for name, w, m, v in zip(WIDE, w_wide, m_wide, v_wide):
        for res, val in zip(results, _adamw(mine, theirs, row0, w, m, v, name="adamw_" + name)):
            res[name] = val.reshape(shapes[name])
        row0 += w.shape[0]
    for res, val in zip(results, _adamw(mine, theirs, n_wide, w_small, m_small, v_small, name="adamw_small")):
        res.update(_unpack_small(val, shapes))
    return (loss, dx[None], *[res[n] for res in results for n in WEIGHTS])
```
